```python
import math
import jax
import jax.numpy as jnp
from jax import lax
import numpy as np

D_MODEL = 2048
BATCH = 4
SEQ = 2048
DEPTH = 2

MEM_LEN = 256
CHUNK = 64
CONV_K = 4
NORM_EPS = 1e-6
L2_EPS = 1e-6

N_EVEN = (DEPTH + 1) // 2
N_ODD = DEPTH // 2

A_HEADS = 8
A_DK = 128
A_DV = D_MODEL // 2 // A_HEADS
A_QK = A_HEADS * A_DK
A_V = A_HEADS * A_DV
B_HEADS = 8
B_DK = 128
B_DV = D_MODEL // 2 // B_HEADS
B_QK = B_HEADS * B_DK
B_V = B_HEADS * B_DV
HY_SIZES = (A_QK, A_QK, A_V, A_V, B_QK, B_QK, B_V, B_V, B_HEADS, B_HEADS)
HY_PROJ = sum(HY_SIZES)
HY_MIX = A_V + B_V

SSD_DINNER = 2 * D_MODEL
SSD_HEADDIM = 64
SSD_HEADS = SSD_DINNER // SSD_HEADDIM
SSD_GROUPS = 8
SSD_DSTATE = 128
SSD_CONV_DIM = SSD_DINNER + 2 * SSD_GROUPS * SSD_DSTATE
SSD_PROJ = SSD_DINNER + SSD_CONV_DIM + SSD_HEADS

XA_HEADS = 4
XA_HEAD_DIM = 128
XA_DIM = XA_HEADS * XA_HEAD_DIM

D_FF = ((8 * D_MODEL + 3 * 256 - 1) // (3 * 256)) * 256

kernel_name = 'hybrid_hgrn2_gdn_ssd_trunk'


def _split(t, sizes):
    idx = [int(s) for s in np.cumsum(sizes)[:-1]]
    return jnp.split(t, idx, axis=-1)


def rmsnorm(x, w):
    xf = x.astype(jnp.float32)
    y = xf * lax.rsqrt(jnp.mean(xf * xf, axis=-1, keepdims=True) + NORM_EPS)
    return (y * w.astype(jnp.float32)).astype(x.dtype)


def heads(t, n):
    return t.reshape(t.shape[0], t.shape[1], n, -1)


def l2norm(t):
    return t * lax.rsqrt(jnp.sum(t * t, axis=-1, keepdims=True) + L2_EPS)


def gated_head_norm(o, w, gate):
    y = o * lax.rsqrt(jnp.mean(o * o, axis=-1, keepdims=True) + NORM_EPS)
    y = y.reshape(o.shape[0], o.shape[1], -1)
    return y * w.astype(jnp.float32) * jax.nn.silu(gate)


def causal_depthwise_conv(x, w):
    k_len, ch = w.shape
    return lax.conv_general_dilated(
        x, w.astype(x.dtype)[:, None, :], window_strides=(1,), padding=[(k_len - 1, 0)],
        dimension_numbers=('NWC', 'WIO', 'NWC'), feature_group_count=ch)


def _to_chunks(t):
    b, s, h, d = t.shape
    return t.reshape(b, s // CHUNK, CHUNK, h, d).transpose(1, 0, 3, 2, 4)


def _to_chunks_scalar(t):
    b, s, h = t.shape
    return t.reshape(b, s // CHUNK, CHUNK, h).transpose(1, 0, 3, 2)


def _from_chunks(t):
    nc, b, h, c, d = t.shape
    return t.transpose(1, 0, 3, 2, 4).reshape(b, nc * c, h, d)


def hgrn2_chunked(q, k, v, log_f):
    bsz, _, nh, dk = q.shape
    dv = v.shape[-1]
    qc, kc, vc = _to_chunks(q), _to_chunks(k), _to_chunks(v)
    gcum = jnp.cumsum(_to_chunks(log_f), axis=3)
    causal = jnp.tril(jnp.ones((CHUNK, CHUNK), bool))

    def step(state, inp):
        q_, k_, v_, g_ = inp
        diff = g_[:, :, :, None, :] - g_[:, :, None, :, :]
        decay = jnp.exp(jnp.where(causal[:, :, None], diff, -jnp.inf))
        attn = jnp.einsum('bhik,bhjk,bhijk->bhij', q_, k_, decay)
        o = jnp.einsum('bhij,bhjv->bhiv', attn, v_) + \
            jnp.einsum('bhik,bhkv->bhiv', q_ * jnp.exp(g_), state)
        g_last = g_[:, :, -1:, :]
        state = state * jnp.exp(g_last[:, :, 0, :])[..., None] + \
            jnp.einsum('bhjk,bhjv->bhkv', k_ * jnp.exp(g_last - g_), v_)
        return state, o

    s0 = jnp.zeros((bsz, nh, dk, dv), jnp.float32)
    _, o = lax.scan(step, s0, (qc, kc, vc, gcum))
    return _from_chunks(o)


def gated_delta_chunked(q, k, v, beta, g):
    bsz, _, nh, dk = q.shape
    dv = v.shape[-1]
    qc, kc, vc = _to_chunks(q), _to_chunks(k), _to_chunks(v)
    bc = _to_chunks_scalar(beta)
    gam = jnp.cumsum(_to_chunks_scalar(g), axis=-1)
    lower = jnp.tril(jnp.ones((CHUNK, CHUNK), bool))
    strict = jnp.tril(jnp.ones((CHUNK, CHUNK), jnp.float32), -1)
    decay = jnp.exp(jnp.where(lower, gam[..., :, None] - gam[..., None, :], -jnp.inf))
    kb = kc * bc[..., None]
    a_mat = jnp.einsum('zbhik,zbhjk->zbhij', kb, kc) * decay * strict
    eye = jnp.eye(CHUNK, dtype=jnp.float32)
    t_mat = lax.linalg.triangular_solve(eye + a_mat, jnp.broadcast_to(eye, a_mat.shape),
                                        left_side=True, lower=True, unit_diagonal=True)
    w = jnp.einsum('zbhij,zbhjk->zbhik', t_mat, kb * jnp.exp(gam)[..., None])
    u = jnp.einsum('zbhij,zbhjv->zbhiv', t_mat, vc * bc[..., None])
    qk = jnp.einsum('zbhik,zbhjk->zbhij', qc, kc) * decay
    q_dec = qc * jnp.exp(gam)[..., None]
    g_last = gam[..., -1]
    k_dec = kc * jnp.exp(g_last[..., None] - gam)[..., None]

    def step(state, inp):
        w_, u_, qk_, qd_, kd_, gl_ = inp
        v_new = u_ - jnp.einsum('bhck,bhkv->bhcv', w_, state)
        o = jnp.einsum('bhck,bhkv->bhcv', qd_, state) + jnp.einsum('bhij,bhjv->bhiv', qk_, v_new)
        state = state * jnp.exp(gl_)[..., None, None] + jnp.einsum('bhck,bhcv->bhkv', kd_, v_new)
        return state, o

    s0 = jnp.zeros((bsz, nh, dk, dv), jnp.float32)
    _, o = lax.scan(step, s0, (w, u, qk, q_dec, k_dec, g_last))
    return _from_chunks(o)


def ssd_chunked(x, dt, a_neg, b_in, c_in):
    bsz, seq, nh, hp = x.shape
    ng, ns = b_in.shape[2], b_in.shape[3]
    nr = nh // ng
    nc = seq // CHUNK
    xc = (x * dt[..., None]).reshape(bsz, nc, CHUNK, ng, nr, hp).transpose(1, 0, 2, 3, 4, 5)
    ac = (dt * a_neg).reshape(bsz, nc, CHUNK, ng, nr).transpose(1, 0, 3, 4, 2)
    bc = b_in.reshape(bsz, nc, CHUNK, ng, ns).transpose(1, 0, 2, 3, 4)
    cc = c_in.reshape(bsz, nc, CHUNK, ng, ns).transpose(1, 0, 2, 3, 4)
    acum = jnp.cumsum(ac, axis=-1)
    lower = jnp.tril(jnp.ones((CHUNK, CHUNK), bool))
    lmat = jnp.exp(jnp.where(lower, acum[..., :, None] - acum[..., None, :], -jnp.inf))
    cb = jnp.einsum('zbign,zbjgn->zbgij', cc, bc)
    y_diag = jnp.einsum('zbgrij,zbjgrp->zbigrp', cb[:, :, :, None] * lmat, xc)
    decay_states = jnp.exp(acum[..., -1:] - acum).transpose(0, 1, 4, 2, 3)
    chunk_states = jnp.einsum('zbjgn,zbjgrp->zbgrpn', bc, xc * decay_states[..., None])
    chunk_decay = jnp.exp(acum[..., -1])

    def step(h, inp):
        st, dec = inp
        return h * dec[..., None, None] + st, h

    h0 = jnp.zeros((bsz, ng, nr, hp, ns), jnp.float32)
    _, h_prev = lax.scan(step, h0, (chunk_states, chunk_decay))
    y_off = jnp.einsum('zbign,zbgrpn->zbigrp', cc, h_prev) * \
        jnp.exp(acum).transpose(0, 1, 4, 2, 3)[..., None]
    y = (y_diag + y_off).transpose(1, 0, 2, 3, 4, 5)
    return y.reshape(bsz, seq, nh, hp)


def hgrn2_gdn_mixer(u, w_in, lb, hgrn_norm, gdn_conv_w, gdn_a_log, gdn_dt_bias, gdn_norm, w_out):
    proj = jnp.matmul(u, w_in).astype(jnp.float32)
    a_q, a_f, a_i, a_g, b_q, b_k, b_v, b_g, b_beta, b_a = _split(proj, HY_SIZES)
    lb = lb.astype(jnp.float32)
    log_f = jnp.log(lb + (1.0 - lb) * jax.nn.sigmoid(a_f))
    k_a = (1.0 - lb) * jax.nn.sigmoid(-a_f)
    o_a = hgrn2_chunked(heads(jax.nn.silu(a_q), A_HEADS), heads(k_a, A_HEADS),
                        heads(a_i, A_HEADS), heads(log_f, A_HEADS))
    o_a = gated_head_norm(o_a, hgrn_norm, a_g)
    qkv = jax.nn.silu(causal_depthwise_conv(jnp.concatenate([b_q, b_k, b_v], axis=-1), gdn_conv_w))
    q_b, k_b, v_b = _split(qkv, (B_QK, B_QK, B_V))
    q_b = l2norm(heads(q_b, B_HEADS)) * (B_DK ** -0.5)
    k_b = l2norm(heads(k_b, B_HEADS))
    beta = jax.nn.sigmoid(b_beta)
    g = -jnp.exp(gdn_a_log.astype(jnp.float32)) * jax.nn.softplus(b_a + gdn_dt_bias.astype(jnp.float32))
    o_b = gated_delta_chunked(q_b, k_b, heads(v_b, B_HEADS), beta, g)
    o_b = gated_head_norm(o_b, gdn_norm, b_g)
    return jnp.matmul(jnp.concatenate([o_a, o_b], axis=-1), w_out.astype(jnp.float32))


def mamba2_mixer(u, w_in, conv_w, conv_b, dt_bias, a_log, d_skip, norm_w, w_out):
    proj = jnp.matmul(u, w_in).astype(jnp.float32)
    z, xbc, dt_raw = _split(proj, (SSD_DINNER, SSD_CONV_DIM, SSD_HEADS))
    xbc = jax.nn.silu(causal_depthwise_conv(xbc, conv_w) + conv_b.astype(jnp.float32))
    xs, b_in, c_in = _split(xbc, (SSD_DINNER, SSD_GROUPS * SSD_DSTATE, SSD_GROUPS * SSD_DSTATE))
    bsz, seq, _ = u.shape
    dt = jax.nn.softplus(dt_raw + dt_bias.astype(jnp.float32))
    a_neg = -jnp.exp(a_log.astype(jnp.float32))
    xh = xs.reshape(bsz, seq, SSD_HEADS, SSD_HEADDIM)
    y = ssd_chunked(xh, dt, a_neg,
                    b_in.reshape(bsz, seq, SSD_GROUPS, SSD_DSTATE),
                    c_in.reshape(bsz, seq, SSD_GROUPS, SSD_DSTATE))
    y = (y + d_skip.astype(jnp.float32)[:, None] * xh).reshape(bsz, seq, SSD_DINNER)
    y = rmsnorm(y * jax.nn.silu(z), norm_w)
    return jnp.matmul(y, w_out.astype(jnp.float32))


def memory_cross_attention(h, m, wq, wk, wv, wo):
    bsz, seq, _ = h.shape
    q = jnp.matmul(h, wq).reshape(bsz, seq, XA_HEADS, XA_HEAD_DIM)
    k = jnp.matmul(m, wk).reshape(bsz, m.shape[1], XA_HEADS, XA_HEAD_DIM)
    v = jnp.matmul(m, wv).reshape(bsz, m.shape[1], XA_HEADS, XA_HEAD_DIM)
    s = jnp.einsum('bshd,bmhd->bhsm', q.astype(jnp.float32), k.astype(jnp.float32)) * (XA_HEAD_DIM ** -0.5)
    p = jax.nn.softmax(s, axis=-1)
    o = jnp.einsum('bhsm,bmhd->bshd', p, v.astype(jnp.float32)).reshape(bsz, seq, XA_DIM)
    return jnp.matmul(o, wo.astype(jnp.float32))


def swiglu(h, w_gate, w_up, w_down):
    return jnp.matmul(jax.nn.silu(jnp.matmul(h, w_gate)) * jnp.matmul(h, w_up), w_down)


def _normal(key, shape, scale):
    return scale * jax.random.normal(key, shape, jnp.float32)


def _dense(key, shape):
    return _normal(key, shape, shape[-2] ** -0.5)


def _gain(key, shape):
    return 1.0 + _normal(key, shape, 0.02)


def _dt_bias(key, shape):
    dt = jnp.exp(jax.random.uniform(key, shape, jnp.float32, math.log(1e-3), math.log(1e-1)))
    return dt + jnp.log(-jnp.expm1(-dt))


def _a_log(key, shape):
    return jnp.log(jax.random.uniform(key, shape, jnp.float32, 1.0, 16.0))


def setup_inputs(seed: int = 0) -> dict:
    key = jax.random.key(seed)
    ks = jax.random.split(key, 32)
    return {
        'x': _normal(ks[0], (BATCH, SEQ, D_MODEL), 1.0),
        'mem': _normal(ks[1], (BATCH, MEM_LEN, D_MODEL), 1.0),
        'norm_mix': _gain(ks[2], (DEPTH, D_MODEL)),
        'norm_xattn': _gain(ks[3], (DEPTH, D_MODEL)),
        'norm_mem': _gain(ks[4], (DEPTH, D_MODEL)),
        'norm_ffn': _gain(ks[5], (DEPTH, D_MODEL)),
        'norm_final': _gain(ks[6], (D_MODEL,)),
        'hy_w_in': _dense(ks[7], (N_EVEN, D_MODEL, HY_PROJ)),
        'hgrn_lb_logits': _normal(ks[8], (N_EVEN + 1, A_QK), 0.1),
        'hgrn_norm': _gain(ks[9], (N_EVEN, A_V)),
        'gdn_conv_w': _normal(ks[10], (N_EVEN, CONV_K, B_QK + B_QK + B_V), CONV_K ** -0.5),
        'gdn_a_log': _a_log(ks[11], (N_EVEN, B_HEADS)),
        'gdn_dt_bias': _dt_bias(ks[12], (N_EVEN, B_HEADS)),
        'gdn_norm': _gain(ks[13], (N_EVEN, B_V)),
        'hy_w_out': _dense(ks[14], (N_EVEN, HY_MIX, D_MODEL)),
        'ssd_w_in': _dense(ks[15], (N_ODD, D_MODEL, SSD_PROJ)),
        'ssd_conv_w': _normal(ks[16], (N_ODD, CONV_K, SSD_CONV_DIM), CONV_K ** -0.5),
        'ssd_conv_b': _normal(ks[17], (N_ODD, SSD_CONV_DIM), 0.02),
        'ssd_dt_bias': _dt_bias(ks[18], (N_ODD, SSD_HEADS)),
        'ssd_a_log': _a_log(ks[19], (N_ODD, SSD_HEADS)),
        'ssd_d': 1.0 + _normal(ks[20], (N_ODD, SSD_HEADS), 0.1),
        'ssd_norm': _gain(ks[21], (N_ODD, SSD_DINNER)),
        'ssd_w_out': _dense(ks[22], (N_ODD, SSD_DINNER, D_MODEL)),
        'xa_wq': _dense(ks[23], (DEPTH, D_MODEL, XA_DIM)),
        'xa_wk': _dense(ks[24], (DEPTH, D_MODEL, XA_DIM)),
        'xa_wv': _dense(ks[25], (DEPTH, D_MODEL, XA_DIM)),
        'xa_wo': _dense(ks[26], (DEPTH, XA_DIM, D_MODEL)),
        'ffn_w_gate': _dense(ks[27], (DEPTH, D_MODEL, D_FF)),
        'ffn_w_up': _dense(ks[28], (DEPTH, D_MODEL, D_FF)),
        'ffn_w_down': _dense(ks[29], (DEPTH, D_FF, D_MODEL)),
    }


def reference(x, mem, norm_mix, norm_xattn, norm_mem, norm_ffn, norm_final,
              hy_w_in, hgrn_lb_logits, hgrn_norm, gdn_conv_w, gdn_a_log, gdn_dt_bias, gdn_norm, hy_w_out,
              ssd_w_in, ssd_conv_w, ssd_conv_b, ssd_dt_bias, ssd_a_log, ssd_d, ssd_norm, ssd_w_out,
              xa_wq, xa_wk, xa_wv, xa_wo, ffn_w_gate, ffn_w_up, ffn_w_down):
    lower_bounds = jnp.cumsum(jax.nn.softmax(hgrn_lb_logits.astype(jnp.float32), axis=0), axis=0)
    h = x
    for layer in range(DEPTH):
        u = rmsnorm(h, norm_mix[layer])
        if layer % 2 == 0:
            e = layer // 2
            mix = hgrn2_gdn_mixer(u, hy_w_in[e], lower_bounds[e], hgrn_norm[e], gdn_conv_w[e],
                                  gdn_a_log[e], gdn_dt_bias[e], gdn_norm[e], hy_w_out[e])
        else:
            o = layer // 2
            mix = mamba2_mixer(u, ssd_w_in[o], ssd_conv_w[o], ssd_conv_b[o], ssd_dt_bias[o],
                               ssd_a_log[o], ssd_d[o], ssd_norm[o], ssd_w_out[o])
        h = h + mix.astype(h.dtype)
        xa = memory_cross_attention(rmsnorm(h, norm_xattn[layer]), rmsnorm(mem, norm_mem[layer]),
                                    xa_wq[layer], xa_wk[layer], xa_wv[layer], xa_wo[layer])
        h = h + xa.astype(h.dtype)
        ff = swiglu(rmsnorm(h, norm_ffn[layer]), ffn_w_gate[layer], ffn_w_up[layer], ffn_w_down[layer])
        h = h + ff.astype(h.dtype)
    return rmsnorm(h, norm_final)
```

```python
import functools

import numpy as np
import jax
import jax.numpy as jnp
from jax import lax
from jax.experimental import pallas as pl
from jax.experimental.pallas import tpu as pltpu

F32 = jnp.float32
BF16 = jnp.bfloat16

NORM_EPS = 1e-6
L2_EPS = 1e-6
CHUNK = 64
CONV_K = 4
LANES = 128
SUBLANES = 8
HEAD_DIM = 128
N_HEADS = 8
SSD_HEADDIM = 64
SSD_GROUPS = 8
SSD_HEADS_PER_GROUP = 8
SSD_GROUP_WIDTH = SSD_HEADDIM * SSD_HEADS_PER_GROUP
SSD_DSTATE = 128
XA_HEADS = 4
XA_HEAD_DIM = 128
VMEM_LIMIT_BYTES = 56 * 1024 * 1024


def _cparams(*sem):
    return pltpu.CompilerParams(dimension_semantics=sem, vmem_limit_bytes=VMEM_LIMIT_BYTES)


def _dot(a, b):
    return jnp.dot(a, b, preferred_element_type=F32)


def _dot_nt(a, b):
    return lax.dot_general(a, b, (((1,), (1,)), ((), ())), preferred_element_type=F32)


def _dot_tn(a, b):
    return lax.dot_general(a, b, (((0,), (0,)), ((), ())), preferred_element_type=F32)


def _split3(x):
    hi = x.astype(BF16)
    r1 = x - hi.astype(F32)
    mid = r1.astype(BF16)
    lo = (r1 - mid.astype(F32)).astype(BF16)
    return hi, mid, lo


def _sel_dot(m01, x):
    hi, mid, lo = _split3(x)
    return _dot(m01, hi) + _dot(m01, mid) + _dot(m01, lo)


def _dot_sel(x, m01):
    hi, mid, lo = _split3(x)
    return _dot(hi, m01) + _dot(mid, m01) + _dot(lo, m01)


def _bmm3(a, b):
    ah = a.astype(BF16)
    al = (a - ah.astype(F32)).astype(BF16)
    bh = b.astype(BF16)
    bl = (b - bh.astype(F32)).astype(BF16)
    mm = functools.partial(jnp.einsum, 'cij,cjk->cik', preferred_element_type=F32)
    return mm(ah, bh) + mm(ah, bl) + mm(al, bh)


def _silu(x):
    return x * jax.nn.sigmoid(x)


def _softplus(x):
    return jnp.maximum(x, 0.0) + jnp.log(1.0 + jnp.exp(-jnp.abs(x)))


def _rms_rows(x, gain):
    ms = jnp.mean(x * x, axis=-1, keepdims=True)
    return x * lax.rsqrt(ms + NORM_EPS) * gain


def _causal_conv(ext_ref, x, w, rows):
    ext_ref[pl.ds(SUBLANES, rows), :] = x
    off = SUBLANES - (CONV_K - 1)
    y = w[0:1, :] * ext_ref[pl.ds(off, rows), :]
    for k in range(1, CONV_K):
        y = y + w[k:k + 1, :] * ext_ref[pl.ds(off + k, rows), :]
    ext_ref[pl.ds(0, SUBLANES), :] = ext_ref[pl.ds(rows, SUBLANES), :]
    return y


def _norm_matmul_kernel(x_ref, g_ref, w_ref, wt_ref, o_ref, ot_ref, xn_ref):
    @pl.when(pl.program_id(1) == 0)
    def _():
        xn = _rms_rows(x_ref[...], g_ref[...]).astype(BF16)
        xn_ref[...] = xn
        ot_ref[...] = _dot(xn, wt_ref[...])

    o_ref[...] = _dot(xn_ref[...], w_ref[...])


def norm_matmul(x, gain, w_main, w_tail, tm, tn):
    m, k = x.shape
    n = w_main.shape[1]
    return pl.pallas_call(
        _norm_matmul_kernel,
        grid=(m // tm, n // tn),
        in_specs=[
            pl.BlockSpec((tm, k), lambda i, j: (i, 0)),
            pl.BlockSpec((1, k), lambda i, j: (0, 0)),
            pl.BlockSpec((k, tn), lambda i, j: (0, j)),
            pl.BlockSpec((k, LANES), lambda i, j: (0, 0)),
        ],
        out_specs=[
            pl.BlockSpec((tm, tn), lambda i, j: (i, j)),
            pl.BlockSpec((tm, LANES), lambda i, j: (i, 0)),
        ],
        out_shape=[jax.ShapeDtypeStruct((m, n), F32), jax.ShapeDtypeStruct((m, LANES), F32)],
        scratch_shapes=[pltpu.VMEM((tm, k), BF16)],
        compiler_params=_cparams("parallel", "arbitrary"),
        name="norm_matmul",
    )(x, gain.reshape(1, k), w_main, w_tail)


def _proj_residual_kernel(a_ref, g_ref, w_ref, r_ref, o_ref, an_ref, *, normalize):
    @pl.when(pl.program_id(1) == 0)
    def _():
        if normalize:
            an_ref[...] = _rms_rows(a_ref[...], g_ref[...]).astype(BF16)
        else:
            an_ref[...] = a_ref[...].astype(BF16)

    o_ref[...] = r_ref[...] + _dot(an_ref[...], w_ref[...])


def proj_residual(a, gain, w, resid, tm, tn, normalize):
    m, k = a.shape
    n = w.shape[1]
    return pl.pallas_call(
        functools.partial(_proj_residual_kernel, normalize=normalize),
        grid=(m // tm, n // tn),
        in_specs=[
            pl.BlockSpec((tm, k), lambda i, j: (i, 0)),
            pl.BlockSpec((1, k), lambda i, j: (0, 0)),
            pl.BlockSpec((k, tn), lambda i, j: (0, j)),
            pl.BlockSpec((tm, tn), lambda i, j: (i, j)),
        ],
        out_specs=pl.BlockSpec((tm, tn), lambda i, j: (i, j)),
        out_shape=jax.ShapeDtypeStruct((m, n), F32),
        scratch_shapes=[pltpu.VMEM((tm, k), BF16)],
        compiler_params=_cparams("parallel", "arbitrary"),
        name="proj_residual",
    )(a, gain.reshape(1, k), w, resid)


def _ffn_kernel(x_ref, g_ref, wg_ref, wu_ref, wd_ref, gf_ref, o_ref, xn_ref, *, final_norm):
    f = pl.program_id(1)

    @pl.when(f == 0)
    def _():
        x = x_ref[...]
        xn_ref[...] = _rms_rows(x, g_ref[...]).astype(BF16)
        o_ref[...] = x

    xn = xn_ref[...]
    act = (_silu(_dot(xn, wg_ref[...])) * _dot(xn, wu_ref[...])).astype(BF16)
    o_ref[...] += _dot(act, wd_ref[...])

    if final_norm:
        @pl.when(f == pl.num_programs(1) - 1)
        def _():
            o_ref[...] = _rms_rows(o_ref[...], gf_ref[...])


def ffn_block(x, gain, wg, wu, wd, gain_final, tm, tf, final_norm):
    m, d = x.shape
    dff = wg.shape[1]
    return pl.pallas_call(
        functools.partial(_ffn_kernel, final_norm=final_norm),
        grid=(m // tm, dff // tf),
        in_specs=[
            pl.BlockSpec((tm, d), lambda i, f: (i, 0)),
            pl.BlockSpec((1, d), lambda i, f: (0, 0)),
            pl.BlockSpec((d, tf), lambda i, f: (0, f)),
            pl.BlockSpec((d, tf), lambda i, f: (0, f)),
            pl.BlockSpec((tf, d), lambda i, f: (f, 0)),
            pl.BlockSpec((1, d), lambda i, f: (0, 0)),
        ],
        out_specs=pl.BlockSpec((tm, d), lambda i, f: (i, 0)),
        out_shape=jax.ShapeDtypeStruct((m, d), F32),
        scratch_shapes=[pltpu.VMEM((tm, d), BF16)],
        compiler_params=_cparams("parallel", "arbitrary"),
        name="ffn_block",
    )(x, gain.reshape(1, d), wg, wu, wd, gain_final.reshape(1, d))


def _mem_kv_kernel(m_ref, g_ref, wk_ref, wv_ref, k_ref, v_ref):
    mn = _rms_rows(m_ref[0], g_ref[...]).astype(BF16)
    k_ref[0] = _dot(mn, wk_ref[...]).astype(BF16)
    v_ref[0] = _dot(mn, wv_ref[...]).astype(BF16)


def mem_kv(mem, gain, wk, wv):
    b, ml, d = mem.shape
    xa = wk.shape[1]
    return pl.pallas_call(
        _mem_kv_kernel,
        grid=(b,),
        in_specs=[
            pl.BlockSpec((1, ml, d), lambda i: (i, 0, 0)),
            pl.BlockSpec((1, d), lambda i: (0, 0)),
            pl.BlockSpec((d, xa), lambda i: (0, 0)),
            pl.BlockSpec((d, xa), lambda i: (0, 0)),
        ],
        out_specs=[pl.BlockSpec((1, ml, xa), lambda i: (i, 0, 0))] * 2,
        out_shape=[jax.ShapeDtypeStruct((b, ml, xa), BF16)] * 2,
        compiler_params=_cparams("parallel"),
        name="mem_kv",
    )(mem, gain.reshape(1, d), wk, wv)


def _xattn_kernel(x_ref, g_ref, wq_ref, k_ref, v_ref, wo_ref, o_ref):
    x = x_ref[0]
    xn = _rms_rows(x, g_ref[...]).astype(BF16)
    q = (_dot(xn, wq_ref[...]) * (XA_HEAD_DIM ** -0.5)).astype(BF16)
    outs = []
    for h in range(XA_HEADS):
        sl = slice(h * XA_HEAD_DIM, (h + 1) * XA_HEAD_DIM)
        s = _dot_nt(q[:, sl], k_ref[0, :, sl])
        p = jnp.exp(s - jnp.max(s, axis=-1, keepdims=True))
        den = jnp.sum(p, axis=-1, keepdims=True)
        outs.append(_dot(p.astype(BF16), v_ref[0, :, sl]) / den)
    o = jnp.concatenate(outs, axis=-1).astype(BF16)
    o_ref[0] = x + _dot(o, wo_ref[...])


def xattn_block(x, gain, wq, k, v, wo, tm):
    b, s, d = x.shape
    ml, xa = k.shape[1], k.shape[2]
    return pl.pallas_call(
        _xattn_kernel,
        grid=(b, s // tm),
        in_specs=[
            pl.BlockSpec((1, tm, d), lambda i, t: (i, t, 0)),
            pl.BlockSpec((1, d), lambda i, t: (0, 0)),
            pl.BlockSpec((d, xa), lambda i, t: (0, 0)),
            pl.BlockSpec((1, ml, xa), lambda i, t: (i, 0, 0)),
            pl.BlockSpec((1, ml, xa), lambda i, t: (i, 0, 0)),
            pl.BlockSpec((xa, d), lambda i, t: (0, 0)),
        ],
        out_specs=pl.BlockSpec((1, tm, d), lambda i, t: (i, t, 0)),
        out_shape=jax.ShapeDtypeStruct((b, s, d), F32),
        compiler_params=_cparams("parallel", "parallel"),
        name="xattn_block",
    )(x, gain.reshape(1, d), wq, k, v, wo)


_HGRN_LEVELS = (32, 16, 8, 4, 2, 1)


def _hgrn_tables():
    c = CHUNK
    r = np.arange(c)[:, None]
    t = np.arange(c)[None, :]
    blocks = [(t <= r), (t > r)]
    masks = []
    for s in _HGRN_LEVELS:
        a = (r // (2 * s)) * (2 * s)
        mid = a + s - 1
        right = r >= a + s
        blocks.append(np.where(right, (t > mid) & (t <= r), (t > r) & (t <= mid)))
        i, j = r, t
        masks.append((i // (2 * s) == j // (2 * s)) & (i % (2 * s) >= s) & (j % (2 * s) < s))
    masks.append(r == t)
    sums = np.concatenate(blocks, axis=0).astype(np.float32)
    return jnp.asarray(sums, BF16), jnp.asarray(np.stack(masks).astype(np.float32))


def _hgrn_kernel(q_ref, f_ref, i_ref, gate_ref, lbl_ref, nw_ref, sums_ref, masks_ref, y_ref, st_ref,
                 *, layer, n_chunks):
    @pl.when(pl.program_id(2) == 0)
    def _():
        st_ref[...] = jnp.zeros_like(st_ref)

    logits = lbl_ref[...]
    ex = jnp.exp(logits - jnp.max(logits, axis=0, keepdims=True))
    sm = ex / jnp.sum(ex, axis=0, keepdims=True)
    lb = jnp.sum(sm[0:layer + 1, :], axis=0, keepdims=True)
    nw = nw_ref[...]
    sums = sums_ref[...]
    c = CHUNK

    def chunk(ci, carry):
        rows = pl.ds(pl.multiple_of(ci * c, c), c)
        a_q = q_ref[0, rows, :]
        a_f = f_ref[0, rows, :]
        v = i_ref[0, rows, :]
        q = _silu(a_q)
        log_f = jnp.log(lb + (1.0 - lb) * jax.nn.sigmoid(a_f))
        k = (1.0 - lb) * jax.nn.sigmoid(-a_f)
        e = _sel_dot(sums, log_f)
        g = e[0:c]
        sfx = e[c:2 * c]
        attn = jnp.zeros((c, c), F32)
        for lv in range(len(_HGRN_LEVELS)):
            sc = jnp.exp(e[(2 + lv) * c:(3 + lv) * c])
            attn = attn + masks_ref[lv] * _dot_nt((q * sc).astype(BF16), (k * sc).astype(BF16))
        attn = attn + masks_ref[len(_HGRN_LEVELS)] * _dot_nt(q.astype(BF16), k.astype(BF16))
        st = st_ref[...]
        vb = v.astype(BF16)
        o = _dot(attn.astype(BF16), vb) + _dot_nt((q * jnp.exp(g)).astype(BF16), st.astype(BF16))
        g_last = g[c - 1:c, :]
        st_ref[...] = st * jnp.exp(g_last) + _dot_tn(vb, (k * jnp.exp(sfx)).astype(BF16))
        y = o * lax.rsqrt(jnp.mean(o * o, axis=-1, keepdims=True) + NORM_EPS)
        y_ref[0, rows, :] = (y * nw * _silu(gate_ref[0, rows, :])).astype(y_ref.dtype)
        return carry

    lax.fori_loop(0, n_chunks, chunk, 0)


def hgrn_heads(proj, lb_logits, norm_w, layer, col0, tt):
    b, s, _ = proj.shape
    nl = lb_logits.shape[0]
    sums, masks = _hgrn_tables()
    blk = lambda o: pl.BlockSpec((1, tt, HEAD_DIM), lambda i, h, t: (i, t, col0 + o + h))
    return pl.pallas_call(
        functools.partial(_hgrn_kernel, layer=layer, n_chunks=tt // CHUNK),
        grid=(b, N_HEADS, s // tt),
        in_specs=[
            blk(0), blk(N_HEADS), blk(2 * N_HEADS), blk(3 * N_HEADS),
            pl.BlockSpec((nl, HEAD_DIM), lambda i, h, t: (0, h)),
            pl.BlockSpec((1, HEAD_DIM), lambda i, h, t: (0, h)),
            pl.BlockSpec(sums.shape, lambda i, h, t: (0, 0)),
            pl.BlockSpec(masks.shape, lambda i, h, t: (0, 0, 0)),
        ],
        out_specs=pl.BlockSpec((1, tt, HEAD_DIM), lambda i, h, t: (i, t, h)),
        out_shape=jax.ShapeDtypeStruct((b, s, N_HEADS * HEAD_DIM), BF16),
        scratch_shapes=[pltpu.VMEM((HEAD_DIM, HEAD_DIM), F32)],
        compiler_params=_cparams("parallel", "parallel", "arbitrary"),
        name="hgrn_heads",
    )(proj, proj, proj, proj, lb_logits, norm_w.reshape(1, -1), sums, masks)


def _gdn_kernel(q_ref, k_ref, v_ref, gate_ref, gc_ref, gr_ref, cwq_ref, cwk_ref, cwv_ref,
                alog_ref, dtb_ref, nw_ref, y_ref,
                s_ref, xq_ref, xk_ref, xv_ref, qd_ref, kd_ref, qk_ref, a_ref, rhs_ref, egl_ref, wu_ref,
                *, tt):
    h = pl.program_id(1)
    c = CHUNK
    n_chunks = tt // c

    @pl.when(pl.program_id(2) == 0)
    def _():
        s_ref[...] = jnp.zeros_like(s_ref)
        for r in (xq_ref, xk_ref, xv_ref):
            r[pl.ds(0, SUBLANES), :] = jnp.zeros((SUBLANES, HEAD_DIM), F32)

    q = _silu(_causal_conv(xq_ref, q_ref[0], cwq_ref[...], tt))
    k = _silu(_causal_conv(xk_ref, k_ref[0], cwk_ref[...], tt))
    v = _silu(_causal_conv(xv_ref, v_ref[0], cwv_ref[...], tt))
    q = q * lax.rsqrt(jnp.sum(q * q, axis=-1, keepdims=True) + L2_EPS) * (HEAD_DIM ** -0.5)
    k = k * lax.rsqrt(jnp.sum(k * k, axis=-1, keepdims=True) + L2_EPS)

    a_neg = -jnp.exp(jnp.full((1, 1), alog_ref[h], F32))
    dtb = jnp.full((1, 1), dtb_ref[h], F32)
    gates = gc_ref[0]
    lane = lax.broadcasted_iota(jnp.int32, gates.shape, 1)
    beta_c = jax.nn.sigmoid(jnp.sum(jnp.where(lane == h, gates, 0.0), axis=-1, keepdims=True))
    g_c = a_neg * _softplus(jnp.sum(jnp.where(lane == N_HEADS + h, gates, 0.0), axis=-1, keepdims=True) + dtb)
    g_r = a_neg * _softplus(gr_ref[0, pl.ds(N_HEADS + h, 1), :] + dtb)

    ii = lax.broadcasted_iota(jnp.int32, (c, c), 0)
    jj = lax.broadcasted_iota(jnp.int32, (c, c), 1)
    lower = ii >= jj
    strict = ii > jj

    for ci in range(n_chunks):
        rows = slice(ci * c, (ci + 1) * c)
        qc, kc, vc = q[rows], k[rows], v[rows]
        bc = beta_c[rows]
        gcol = jnp.broadcast_to(g_c[rows], (c, c))
        grow = jnp.broadcast_to(g_r[:, rows], (c, c))
        gam_c = jnp.sum(jnp.where(lower, grow, 0.0), axis=1, keepdims=True)
        gam_r = jnp.sum(jnp.where(ii <= jj, gcol, 0.0), axis=0, keepdims=True)
        decay = jnp.where(lower, jnp.exp(jnp.minimum(gam_c - gam_r, 0.0)), 0.0)
        kb = kc * bc
        kcb = kc.astype(BF16)
        a_ref[ci] = jnp.where(strict, _dot_nt(kb.astype(BF16), kcb) * decay, 0.0)
        qk_ref[ci] = (_dot_nt(qc.astype(BF16), kcb) * decay).astype(BF16)
        eg = jnp.exp(gam_c)
        g_last = gam_c[c - 1:c, :]
        rhs_ref[ci] = jnp.concatenate([kb * eg, vc * bc], axis=1).astype(BF16)
        qd_ref[ci] = (qc * eg).astype(BF16)
        kd_ref[ci] = (kc * jnp.exp(g_last - gam_c)).astype(BF16)
        egl_ref[ci] = jnp.broadcast_to(jnp.exp(g_last), (SUBLANES, HEAD_DIM))

    x = -a_ref[...]
    eye = jnp.where(ii == jj, 1.0, 0.0)[None]
    t_mat = eye + x
    p = x
    for _ in range(5):
        p = _bmm3(p, p)
        t_mat = t_mat + _bmm3(t_mat, p)
    wu_ref[...] = jnp.einsum('cij,cjk->cik', t_mat.astype(BF16), rhs_ref[...], preferred_element_type=F32)

    nw = nw_ref[...]

    def chunk(ci, carry):
        rows = pl.ds(pl.multiple_of(ci * c, c), c)
        st = s_ref[...]
        stb = st.astype(BF16)
        wu = wu_ref[ci]
        v_new = wu[:, HEAD_DIM:] - _dot(wu[:, :HEAD_DIM].astype(BF16), stb)
        vnb = v_new.astype(BF16)
        o = _dot(qd_ref[ci], stb) + _dot(qk_ref[ci], vnb)
        s_ref[...] = st * egl_ref[ci][0:1, :] + _dot_tn(kd_ref[ci], vnb)
        y = o * lax.rsqrt(jnp.mean(o * o, axis=-1, keepdims=True) + NORM_EPS)
        y_ref[0, rows, :] = (y * nw * _silu(gate_ref[0, rows, :])).astype(y_ref.dtype)
        return carry

    lax.fori_loop(0, n_chunks, chunk, 0)


def gdn_heads(proj, gates_col, gates_row, conv_w, a_log, dt_bias, norm_w, col0, tt):
    b, s, _ = proj.shape
    n_chunks = tt // CHUNK
    blk = lambda o: pl.BlockSpec((1, tt, HEAD_DIM), lambda i, h, t: (i, t, col0 + o + h))
    cw = lambda o: pl.BlockSpec((CONV_K, HEAD_DIM), lambda i, h, t: (0, o + h))
    smem = pl.BlockSpec(memory_space=pltpu.SMEM)
    return pl.pallas_call(
        functools.partial(_gdn_kernel, tt=tt),
        grid=(b, N_HEADS, s // tt),
        in_specs=[
            blk(0), blk(N_HEADS), blk(2 * N_HEADS), blk(3 * N_HEADS),
            pl.BlockSpec((1, tt, LANES), lambda i, h, t: (i, t, 0)),
            pl.BlockSpec((1, 2 * N_HEADS, tt), lambda i, h, t: (i, 0, t)),
            cw(0), cw(N_HEADS), cw(2 * N_HEADS),
            smem, smem,
            pl.BlockSpec((1, HEAD_DIM), lambda i, h, t: (0, h)),
        ],
        out_specs=pl.BlockSpec((1, tt, HEAD_DIM), lambda i, h, t: (i, t, h)),
        out_shape=jax.ShapeDtypeStruct((b, s, N_HEADS * HEAD_DIM), BF16),
        scratch_shapes=[
            pltpu.VMEM((HEAD_DIM, HEAD_DIM), F32),
            pltpu.VMEM((tt + SUBLANES, HEAD_DIM), F32),
            pltpu.VMEM((tt + SUBLANES, HEAD_DIM), F32),
            pltpu.VMEM((tt + SUBLANES, HEAD_DIM), F32),
            pltpu.VMEM((n_chunks, CHUNK, HEAD_DIM), BF16),
            pltpu.VMEM((n_chunks, CHUNK, HEAD_DIM), BF16),
            pltpu.VMEM((n_chunks, CHUNK, CHUNK), BF16),
            pltpu.VMEM((n_chunks, CHUNK, CHUNK), F32),
            pltpu.VMEM((n_chunks, CHUNK, 2 * HEAD_DIM), BF16),
            pltpu.VMEM((n_chunks, SUBLANES, HEAD_DIM), F32),
            pltpu.VMEM((n_chunks, CHUNK, 2 * HEAD_DIM), F32),
        ],
        compiler_params=_cparams("parallel", "parallel", "arbitrary"),
        name="gdn_heads",
    )(proj, proj, proj, proj, gates_col, gates_row, conv_w, conv_w, conv_w,
      a_log, dt_bias, norm_w.reshape(1, -1))


def _ssd_tables():
    c = CHUNK
    r = np.arange(c)[:, None]
    t = np.arange(c)[None, :]
    col_sums = np.concatenate([(t <= r), (t > r)], axis=0).astype(np.float32)
    row_cum = (r <= t).astype(np.float32)
    expand = np.zeros((SSD_GROUPS, LANES, SSD_GROUP_WIDTH), np.float32)
    for g in range(SSD_GROUPS):
        for hh in range(SSD_HEADS_PER_GROUP):
            expand[g, g * SSD_HEADS_PER_GROUP + hh, hh * SSD_HEADDIM:(hh + 1) * SSD_HEADDIM] = 1.0
    return jnp.asarray(col_sums, BF16), jnp.asarray(row_cum, BF16), jnp.asarray(expand, BF16)


def _ssd_kernel(z_ref, x_ref, b_ref, c_ref, dtc_ref, dtr_ref, cwx_ref, cwb_ref, cwc_ref,
                cbx_ref, cbb_ref, cbc_ref, dtbe_ref, aloge_ref, de_ref, dtbc_ref, alogc_ref,
                exp_ref, csum_ref, rcum_ref, y_ref,
                ht_ref, xx_ref, xb_ref, xc_ref, *, tt):
    c = CHUNK
    n_chunks = tt // c
    hp = SSD_HEADDIM

    @pl.when(pl.program_id(2) == 0)
    def _():
        ht_ref[...] = jnp.zeros_like(ht_ref)
        for r in (xx_ref, xb_ref, xc_ref):
            r[pl.ds(0, SUBLANES), :] = jnp.zeros((SUBLANES, r.shape[1]), F32)

    xs = _silu(_causal_conv(xx_ref, x_ref[0], cwx_ref[...], tt) + cbx_ref[...])
    bm = _silu(_causal_conv(xb_ref, b_ref[0], cwb_ref[...], tt) + cbb_ref[...])
    cm = _silu(_causal_conv(xc_ref, c_ref[0], cwc_ref[...], tt) + cbc_ref[...])

    dt_e = _softplus(_dot_sel(dtc_ref[0], exp_ref[0]) + dtbe_ref[...])
    da_e = dt_e * -jnp.exp(aloge_ref[...])
    xdt = xs * dt_e
    da_r = _softplus(dtr_ref[0] + dtbc_ref[...]) * -jnp.exp(alogc_ref[...])

    ii = lax.broadcasted_iota(jnp.int32, (c, LANES), 0)
    jj = lax.broadcasted_iota(jnp.int32, (c, LANES), 1)
    lower2 = ii >= (jj % c)
    left = jj < hp
    csum = csum_ref[...]
    rcum = rcum_ref[...]
    d_skip = de_ref[...]

    for ci in range(n_chunks):
        rows = slice(ci * c, (ci + 1) * c)
        e = _sel_dot(csum, da_e[rows])
        acum = e[0:c]
        sfx = e[c:2 * c]
        acum_r = _dot_sel(da_r[:, rows], rcum)
        bmc = bm[rows].astype(BF16)
        cmc = cm[rows].astype(BF16)
        xdc = xdt[rows]
        cb2 = _dot_nt(cmc, jnp.concatenate([bmc, bmc], axis=0))
        ydiag = []
        for pr in range(SSD_HEADS_PER_GROUP // 2):
            lanes = slice(2 * pr * hp, (2 * pr + 2) * hp)
            col = acum[:, lanes]
            row = jnp.concatenate([acum_r[2 * pr:2 * pr + 1, :], acum_r[2 * pr + 1:2 * pr + 2, :]], axis=1)
            lmat = jnp.where(lower2, jnp.exp(jnp.minimum(col - row, 0.0)), 0.0)
            xp = xdc[:, lanes]
            rhs = jnp.concatenate([jnp.where(left, xp, 0.0), jnp.where(left, 0.0, xp)], axis=0)
            ydiag.append(_dot((cb2 * lmat).astype(BF16), rhs.astype(BF16)))
        y = jnp.concatenate(ydiag, axis=1)
        ht = ht_ref[...]
        y = y + _dot(cmc, ht.astype(BF16)) * jnp.exp(acum)
        ht_ref[...] = ht * jnp.exp(acum[c - 1:c, :]) + _dot_tn(bmc, (xdc * jnp.exp(sfx)).astype(BF16))
        y = y + d_skip * xs[rows]
        y_ref[0, rows, :] = y * _silu(z_ref[0, rows, :])


def ssd_groups(proj, dt_col, dt_row, conv_w, conv_b, dt_bias, a_log, d_skip, tt):
    b, s, _ = proj.shape
    g_w = SSD_GROUP_WIDTH
    n_xblk = SSD_GROUPS
    csum, rcum, expand = _ssd_tables()
    rep = lambda p: jnp.repeat(p.astype(F32), SSD_HEADDIM).reshape(1, -1)
    col = lambda p: p.astype(F32).reshape(-1, 1)
    row2 = lambda v: v.reshape(1, -1)
    wide = lambda o: pl.BlockSpec((1, tt, g_w), lambda i, g, t: (i, t, o + g))
    narrow = lambda o: pl.BlockSpec((1, tt, LANES), lambda i, g, t: (i, t, o + g))
    x_off = n_xblk
    b_off = 2 * n_xblk * (g_w // LANES)
    c_off = b_off + SSD_GROUPS
    return pl.pallas_call(
        functools.partial(_ssd_kernel, tt=tt),
        grid=(b, SSD_GROUPS, s // tt),
        in_specs=[
            wide(0), wide(x_off), narrow(b_off), narrow(c_off),
            pl.BlockSpec((1, tt, LANES), lambda i, g, t: (i, t, 0)),
            pl.BlockSpec((1, SSD_HEADS_PER_GROUP, tt), lambda i, g, t: (i, g, t)),
            pl.BlockSpec((CONV_K, g_w), lambda i, g, t: (0, g)),
            pl.BlockSpec((CONV_K, LANES), lambda i, g, t: (0, n_xblk * (g_w // LANES) + g)),
            pl.BlockSpec((CONV_K, LANES), lambda i, g, t: (0, n_xblk * (g_w // LANES) + SSD_GROUPS + g)),
            pl.BlockSpec((1, g_w), lambda i, g, t: (0, g)),
            pl.BlockSpec((1, LANES), lambda i, g, t: (0, n_xblk * (g_w // LANES) + g)),
            pl.BlockSpec((1, LANES), lambda i, g, t: (0, n_xblk * (g_w // LANES) + SSD_GROUPS + g)),
            pl.BlockSpec((1, g_w), lambda i, g, t: (0, g)),
            pl.BlockSpec((1, g_w), lambda i, g, t: (0, g)),
            pl.BlockSpec((1, g_w), lambda i, g, t: (0, g)),
            pl.BlockSpec((SSD_HEADS_PER_GROUP, 1), lambda i, g, t: (g, 0)),
            pl.BlockSpec((SSD_HEADS_PER_GROUP, 1), lambda i, g, t: (g, 0)),
            pl.BlockSpec((1, LANES, g_w), lambda i, g, t: (g, 0, 0)),
            pl.BlockSpec(csum.shape, lambda i, g, t: (0, 0)),
            pl.BlockSpec(rcum.shape, lambda i, g, t: (0, 0)),
        ],
        out_specs=pl.BlockSpec((1, tt, g_w), lambda i, g, t: (i, t, g)),
        out_shape=jax.ShapeDtypeStruct((b, s, SSD_GROUPS * g_w), F32),
        scratch_shapes=[
            pltpu.VMEM((SSD_DSTATE, g_w), F32),
            pltpu.VMEM((tt + SUBLANES, g_w), F32),
            pltpu.VMEM((tt + SUBLANES, LANES), F32),
            pltpu.VMEM((tt + SUBLANES, LANES), F32),
        ],
        compiler_params=_cparams("parallel", "parallel", "arbitrary"),
        name="ssd_groups",
    )(proj, proj, proj, proj, dt_col, dt_row, conv_w, conv_w, conv_w,
      row2(conv_b), row2(conv_b), row2(conv_b), rep(dt_bias), rep(a_log), rep(d_skip),
      col(dt_bias), col(a_log), expand, csum, rcum)


def _pad_tail(w_tail):
    return jnp.pad(w_tail, ((0, 0), (0, LANES - w_tail.shape[1]))).astype(BF16)


def _tile(n, pref):
    t = min(n, pref)
    while n % t:
        t //= 2
    return t


def kernel(x, mem, norm_mix, norm_xattn, norm_mem, norm_ffn, norm_final, hy_w_in, hgrn_lb_logits, hgrn_norm,
           gdn_conv_w, gdn_a_log, gdn_dt_bias, gdn_norm, hy_w_out, ssd_w_in, ssd_conv_w, ssd_conv_b,
           ssd_dt_bias, ssd_a_log, ssd_d, ssd_norm, ssd_w_out, xa_wq, xa_wk, xa_wv, xa_wo, ffn_w_gate,
           ffn_w_up, ffn_w_down):
    bsz, seq, d = x.shape
    tok = bsz * seq
    depth = norm_mix.shape[0]
    tm = _tile(tok, 1024)
    tt = _tile(seq, 512)
    bf = lambda w: w.astype(BF16)

    h = x.reshape(tok, d)
    for layer in range(depth):
        if layer % 2 == 0:
            e = layer // 2
            n_main = 8 * N_HEADS * HEAD_DIM
            w_in = hy_w_in[e]
            proj, tail = norm_matmul(h, norm_mix[layer], bf(w_in[:, :n_main]), _pad_tail(w_in[:, n_main:]),
                                     tm, _tile(n_main, 1024))
            proj = proj.reshape(bsz, seq, n_main)
            gates_col = tail.reshape(bsz, seq, LANES)
            gates_row = jnp.swapaxes(gates_col[:, :, :2 * N_HEADS], 1, 2)
            y_a = hgrn_heads(proj, hgrn_lb_logits.astype(F32), hgrn_norm[e], e, 0, tt)
            y_b = gdn_heads(proj, gates_col, gates_row, gdn_conv_w[e].astype(F32), gdn_a_log[e].astype(F32),
                            gdn_dt_bias[e].astype(F32), gdn_norm[e], 4 * N_HEADS, tt)
            y = jnp.concatenate([y_a, y_b], axis=-1).reshape(tok, -1)
            h = proj_residual(y, jnp.ones((y.shape[1],), F32), bf(hy_w_out[e]), h,
                              _tile(tok, 512), _tile(d, 1024), False)
        else:
            o = layer // 2
            n_heads = ssd_dt_bias.shape[1]
            w_in = ssd_w_in[o]
            n_main = w_in.shape[1] - n_heads
            proj, tail = norm_matmul(h, norm_mix[layer], bf(w_in[:, :n_main]), _pad_tail(w_in[:, n_main:]),
                                     tm, _tile(n_main, 1024))
            proj = proj.reshape(bsz, seq, n_main)
            dt_col = tail.reshape(bsz, seq, LANES)
            dt_row = jnp.swapaxes(dt_col[:, :, :n_heads], 1, 2)
            y = ssd_groups(proj, dt_col, dt_row, ssd_conv_w[o].astype(F32), ssd_conv_b[o].astype(F32),
                           ssd_dt_bias[o], ssd_a_log[o], ssd_d[o], tt)
            h = proj_residual(y.reshape(tok, -1), ssd_norm[o], bf(ssd_w_out[o]), h,
                              _tile(tok, 512), _tile(d, 1024), True)
        k_mem, v_mem = mem_kv(mem, norm_mem[layer], bf(xa_wk[layer]), bf(xa_wv[layer]))
        h = xattn_block(h.reshape(bsz, seq, d), norm_xattn[layer], bf(xa_wq[layer]), k_mem, v_mem,
                        bf(xa_wo[layer]), _tile(seq, 512)).reshape(tok, d)
        h = ffn_block(h, norm_ffn[layer], bf(ffn_w_gate[layer]), bf(ffn_w_up[layer]), bf(ffn_w_down[layer]),
                      norm_final, _tile(tok, 512), _tile(ffn_w_gate.shape[2], 512), layer == depth - 1)
    return h.reshape(bsz, seq, d)
```

```python
import functools

import numpy as np
import jax
import jax.numpy as jnp
from jax import lax
from jax.experimental import pallas as pl
from jax.experimental.pallas import tpu as pltpu

F32 = jnp.float32
BF16 = jnp.bfloat16

NORM_EPS = 1e-6
L2_EPS = 1e-6
CHUNK = 64
CONV_K = 4
LANES = 128
SUBLANES = 8
HEAD_DIM = 128
N_HEADS = 8
SSD_HEADDIM = 64
SSD_GROUPS = 8
SSD_HEADS_PER_GROUP = 8
SSD_GROUP_WIDTH = SSD_HEADDIM * SSD_HEADS_PER_GROUP
SSD_DSTATE = 128
XA_HEADS = 4
XA_HEAD_DIM = 128
VMEM_LIMIT_BYTES = 56 * 1024 * 1024
SEQ_TILE = 512
GDN_SEQ_TILE = 256


def _cparams(*sem):
    return pltpu.CompilerParams(dimension_semantics=sem, vmem_limit_bytes=VMEM_LIMIT_BYTES)


def _dot(a, b):
    return jnp.dot(a, b, preferred_element_type=F32)


def _dot_nt(a, b):
    return lax.dot_general(a, b, (((1,), (1,)), ((), ())), preferred_element_type=F32)


def _dot_tn(a, b):
    return lax.dot_general(a, b, (((0,), (0,)), ((), ())), preferred_element_type=F32)


def _split3(x):
    hi = x.astype(BF16)
    r1 = x - hi.astype(F32)
    mid = r1.astype(BF16)
    lo = (r1 - mid.astype(F32)).astype(BF16)
    return hi, mid, lo


def _sel_dot(m01, x):
    hi, mid, lo = _split3(x)
    return _dot(m01, hi) + _dot(m01, mid) + _dot(m01, lo)


def _dot_sel(x, m01):
    hi, mid, lo = _split3(x)
    return _dot(hi, m01) + _dot(mid, m01) + _dot(lo, m01)


def _bmm3(a, b):
    ah = a.astype(BF16)
    al = (a - ah.astype(F32)).astype(BF16)
    bh = b.astype(BF16)
    bl = (b - bh.astype(F32)).astype(BF16)
    mm = functools.partial(jnp.einsum, 'cij,cjk->cik', preferred_element_type=F32)
    return mm(ah, bh) + mm(ah, bl) + mm(al, bh)


def _silu(x):
    return x * jax.nn.sigmoid(x)


def _softplus(x):
    return jnp.maximum(x, 0.0) + jnp.log(1.0 + jnp.exp(-jnp.abs(x)))


def _rms_rows(x, gain):
    ms = jnp.mean(x * x, axis=-1, keepdims=True)
    return x * lax.rsqrt(ms + NORM_EPS) * gain


def _causal_conv(ext_ref, x, w, rows):
    ext_ref[pl.ds(SUBLANES, rows), :] = x
    off = SUBLANES - (CONV_K - 1)
    y = w[0:1, :] * ext_ref[pl.ds(off, rows), :]
    for k in range(1, CONV_K):
        y = y + w[k:k + 1, :] * ext_ref[pl.ds(off + k, rows), :]
    ext_ref[pl.ds(0, SUBLANES), :] = ext_ref[pl.ds(rows, SUBLANES), :]
    return y


def _norm_matmul_kernel(x_ref, g_ref, w_ref, wt_ref, o_ref, ot_ref, xn_ref):
    @pl.when(pl.program_id(1) == 0)
    def _():
        xn = _rms_rows(x_ref[...], g_ref[...]).astype(BF16)
        xn_ref[...] = xn
        ot_ref[...] = _dot(xn, wt_ref[...])

    o_ref[...] = _dot(xn_ref[...], w_ref[...])


def norm_matmul(x, gain, w_main, w_tail, tm, tn):
    m, k = x.shape
    n = w_main.shape[1]
    return pl.pallas_call(
        _norm_matmul_kernel,
        grid=(m // tm, n // tn),
        in_specs=[
            pl.BlockSpec((tm, k), lambda i, j: (i, 0)),
            pl.BlockSpec((1, k), lambda i, j: (0, 0)),
            pl.BlockSpec((k, tn), lambda i, j: (0, j)),
            pl.BlockSpec((k, LANES), lambda i, j: (0, 0)),
        ],
        out_specs=[
            pl.BlockSpec((tm, tn), lambda i, j: (i, j)),
            pl.BlockSpec((tm, LANES), lambda i, j: (i, 0)),
        ],
        out_shape=[jax.ShapeDtypeStruct((m, n), F32), jax.ShapeDtypeStruct((m, LANES), F32)],
        scratch_shapes=[pltpu.VMEM((tm, k), BF16)],
        compiler_params=_cparams("parallel", "arbitrary"),
        name="norm_matmul",
    )(x, gain.reshape(1, k), w_main, w_tail)


def _proj_residual_kernel(a_ref, g_ref, w_ref, r_ref, o_ref, an_ref, *, normalize):
    @pl.when(pl.program_id(1) == 0)
    def _():
        if normalize:
            an_ref[...] = _rms_rows(a_ref[...], g_ref[...]).astype(BF16)
        else:
            an_ref[...] = a_ref[...].astype(BF16)

    o_ref[...] = r_ref[...] + _dot(an_ref[...], w_ref[...])


def proj_residual(a, gain, w, resid, tm, tn, normalize):
    m, k = a.shape
    n = w.shape[1]
    return pl.pallas_call(
        functools.partial(_proj_residual_kernel, normalize=normalize),
        grid=(m // tm, n // tn),
        in_specs=[
            pl.BlockSpec((tm, k), lambda i, j: (i, 0)),
            pl.BlockSpec((1, k), lambda i, j: (0, 0)),
            pl.BlockSpec((k, tn), lambda i, j: (0, j)),
            pl.BlockSpec((tm, tn), lambda i, j: (i, j)),
        ],
        out_specs=pl.BlockSpec((tm, tn), lambda i, j: (i, j)),
        out_shape=jax.ShapeDtypeStruct((m, n), F32),
        scratch_shapes=[pltpu.VMEM((tm, k), BF16)],
        compiler_params=_cparams("parallel", "arbitrary"),
        name="proj_residual",
    )(a, gain.reshape(1, k), w, resid)


def _ffn_kernel(x_ref, g_ref, wg_ref, wu_ref, wd_ref, gf_ref, o_ref, xn_ref, *, final_norm):
    f = pl.program_id(1)

    @pl.when(f == 0)
    def _():
        x = x_ref[...]
        xn_ref[...] = _rms_rows(x, g_ref[...]).astype(BF16)
        o_ref[...] = x

    xn = xn_ref[...]
    act = (_silu(_dot(xn, wg_ref[...])) * _dot(xn, wu_ref[...])).astype(BF16)
    o_ref[...] += _dot(act, wd_ref[...])

    if final_norm:
        @pl.when(f == pl.num_programs(1) - 1)
        def _():
            o_ref[...] = _rms_rows(o_ref[...], gf_ref[...])


def ffn_block(x, gain, wg, wu, wd, gain_final, tm, tf, final_norm):
    m, d = x.shape
    dff = wg.shape[1]
    return pl.pallas_call(
        functools.partial(_ffn_kernel, final_norm=final_norm),
        grid=(m // tm, dff // tf),
        in_specs=[
            pl.BlockSpec((tm, d), lambda i, f: (i, 0)),
            pl.BlockSpec((1, d), lambda i, f: (0, 0)),
            pl.BlockSpec((d, tf), lambda i, f: (0, f)),
            pl.BlockSpec((d, tf), lambda i, f: (0, f)),
            pl.BlockSpec((tf, d), lambda i, f: (f, 0)),
            pl.BlockSpec((1, d), lambda i, f: (0, 0)),
        ],
        out_specs=pl.BlockSpec((tm, d), lambda i, f: (i, 0)),
        out_shape=jax.ShapeDtypeStruct((m, d), F32),
        scratch_shapes=[pltpu.VMEM((tm, d), BF16)],
        compiler_params=_cparams("parallel", "arbitrary"),
        name="ffn_block",
    )(x, gain.reshape(1, d), wg, wu, wd, gain_final.reshape(1, d))


def _mem_kv_kernel(m_ref, g_ref, wk_ref, wv_ref, k_ref, v_ref):
    mn = _rms_rows(m_ref[0], g_ref[...]).astype(BF16)
    k_ref[0] = _dot(mn, wk_ref[...]).astype(BF16)
    v_ref[0] = _dot(mn, wv_ref[...]).astype(BF16)


def mem_kv(mem, gain, wk, wv):
    b, ml, d = mem.shape
    xa = wk.shape[1]
    return pl.pallas_call(
        _mem_kv_kernel,
        grid=(b,),
        in_specs=[
            pl.BlockSpec((1, ml, d), lambda i: (i, 0, 0)),
            pl.BlockSpec((1, d), lambda i: (0, 0)),
            pl.BlockSpec((d, xa), lambda i: (0, 0)),
            pl.BlockSpec((d, xa), lambda i: (0, 0)),
        ],
        out_specs=[pl.BlockSpec((1, ml, xa), lambda i: (i, 0, 0))] * 2,
        out_shape=[jax.ShapeDtypeStruct((b, ml, xa), BF16)] * 2,
        compiler_params=_cparams("parallel"),
        name="mem_kv",
    )(mem, gain.reshape(1, d), wk, wv)


def _xattn_kernel(x_ref, g_ref, wq_ref, k_ref, v_ref, wo_ref, o_ref):
    x = x_ref[0]
    xn = _rms_rows(x, g_ref[...]).astype(BF16)
    q = (_dot(xn, wq_ref[...]) * (XA_HEAD_DIM ** -0.5)).astype(BF16)
    outs = []
    for h in range(XA_HEADS):
        sl = slice(h * XA_HEAD_DIM, (h + 1) * XA_HEAD_DIM)
        s = _dot_nt(q[:, sl], k_ref[0, :, sl])
        p = jnp.exp(s - jnp.max(s, axis=-1, keepdims=True))
        den = jnp.sum(p, axis=-1, keepdims=True)
        outs.append(_dot(p.astype(BF16), v_ref[0, :, sl]) / den)
    o = jnp.concatenate(outs, axis=-1).astype(BF16)
    o_ref[0] = x + _dot(o, wo_ref[...])


def xattn_block(x, gain, wq, k, v, wo, tm):
    b, s, d = x.shape
    ml, xa = k.shape[1], k.shape[2]
    return pl.pallas_call(
        _xattn_kernel,
        grid=(b, s // tm),
        in_specs=[
            pl.BlockSpec((1, tm, d), lambda i, t: (i, t, 0)),
            pl.BlockSpec((1, d), lambda i, t: (0, 0)),
            pl.BlockSpec((d, xa), lambda i, t: (0, 0)),
            pl.BlockSpec((1, ml, xa), lambda i, t: (i, 0, 0)),
            pl.BlockSpec((1, ml, xa), lambda i, t: (i, 0, 0)),
            pl.BlockSpec((xa, d), lambda i, t: (0, 0)),
        ],
        out_specs=pl.BlockSpec((1, tm, d), lambda i, t: (i, t, 0)),
        out_shape=jax.ShapeDtypeStruct((b, s, d), F32),
        compiler_params=_cparams("parallel", "parallel"),
        name="xattn_block",
    )(x, gain.reshape(1, d), wq, k, v, wo)


_HGRN_LEVELS = (32, 16, 8, 4, 2, 1)


def _hgrn_masks():
    c = CHUNK
    i = np.arange(c)[:, None]
    j = np.arange(c)[None, :]
    masks = [(i // (2 * s) == j // (2 * s)) & (i % (2 * s) >= s) & (j % (2 * s) < s) for s in _HGRN_LEVELS]
    masks.append(i == j)
    return jnp.asarray(np.stack(masks).astype(np.float32))


def _segment_row(x, seg, idx):
    rows, lanes = x.shape
    x3 = x.reshape(rows // seg, seg, lanes)
    return jnp.broadcast_to(x3[:, idx:idx + 1, :], x3.shape).reshape(rows, lanes)


def _hgrn_kernel(q_ref, f_ref, i_ref, gate_ref, lbl_ref, nw_ref, masks_ref, y_ref, st_ref, *, layer, n_chunks):
    c = CHUNK
    tt = n_chunks * c

    @pl.when(pl.program_id(2) == 0)
    def _():
        st_ref[...] = jnp.zeros_like(st_ref)

    logits = lbl_ref[...]
    ex = jnp.exp(logits - jnp.max(logits, axis=0, keepdims=True))
    sm = ex / jnp.sum(ex, axis=0, keepdims=True)
    lb = jnp.sum(sm[0:layer + 1, :], axis=0, keepdims=True)

    a_f = f_ref[0]
    q = _silu(q_ref[0])
    log_f = jnp.log(lb + (1.0 - lb) * jax.nn.sigmoid(a_f))
    k = (1.0 - lb) * jax.nn.sigmoid(-a_f)
    vb = i_ref[0].astype(BF16)

    row = lax.broadcasted_iota(jnp.int32, (tt, HEAD_DIM), 0)
    rc = row & (c - 1)
    g = log_f
    for s in (1, 2, 4, 8, 16, 32):
        g = g + jnp.where(rc >= s, pltpu.roll(g, s, axis=0), 0.0)
    g_last = _segment_row(g, c, c - 1)

    exps = []
    for s in _HGRN_LEVELS:
        if 2 * s >= SUBLANES:
            exps.append(-jnp.abs(g - _segment_row(g, 2 * s, s - 1)))
        elif s == 2:
            r4 = row & 3
            below = pltpu.roll(log_f, 1, axis=0)
            above = pltpu.roll(log_f, tt - 1, axis=0)
            exps.append(jnp.where(r4 == 0, above, jnp.where(r4 == 1, 0.0,
                                                            jnp.where(r4 == 2, log_f, log_f + below))))
        else:
            exps.append(jnp.where((row & 1) == 1, log_f, 0.0))

    def chunks(x):
        xb = x.astype(BF16)
        return [xb[ci * c:(ci + 1) * c] for ci in range(n_chunks)]

    qs, ks = chunks(q), chunks(k)
    n_lv = len(_HGRN_LEVELS)
    attn = [masks_ref[n_lv] * _dot_nt(qs[ci], ks[ci]) for ci in range(n_chunks)]
    for lv in range(n_lv):
        sc = jnp.exp(exps[lv])
        ql, kl = chunks(q * sc), chunks(k * sc)
        m = masks_ref[lv]
        attn = [attn[ci] + m * _dot_nt(ql[ci], kl[ci]) for ci in range(n_chunks)]

    vs = chunks(vb)
    qg = chunks(q * jnp.exp(g))
    kd = chunks(k * jnp.exp(g_last - g))
    upd = [_dot_tn(vs[ci], kd[ci]) for ci in range(n_chunks)]
    intra = [_dot(attn[ci].astype(BF16), vs[ci]) for ci in range(n_chunks)]
    dec = jnp.exp(g_last)
    st = st_ref[...]
    outs = []
    for ci in range(n_chunks):
        outs.append(intra[ci] + _dot_nt(qg[ci], st.astype(BF16)))
        st = st * dec[ci * c:ci * c + 1, :] + upd[ci]
    st_ref[...] = st
    o = jnp.concatenate(outs, axis=0)
    y = o * lax.rsqrt(jnp.mean(o * o, axis=-1, keepdims=True) + NORM_EPS)
    y_ref[0] = (y * nw_ref[...] * _silu(gate_ref[0])).astype(y_ref.dtype)


def hgrn_heads(proj, lb_logits, norm_w, layer, col0, tt):
    b, s, _ = proj.shape
    nl = lb_logits.shape[0]
    masks = _hgrn_masks()
    blk = lambda o: pl.BlockSpec((1, tt, HEAD_DIM), lambda i, h, t: (i, t, col0 + o + h))
    return pl.pallas_call(
        functools.partial(_hgrn_kernel, layer=layer, n_chunks=tt // CHUNK),
        grid=(b, N_HEADS, s // tt),
        in_specs=[
            blk(0), blk(N_HEADS), blk(2 * N_HEADS), blk(3 * N_HEADS),
            pl.BlockSpec((nl, HEAD_DIM), lambda i, h, t: (0, h)),
            pl.BlockSpec((1, HEAD_DIM), lambda i, h, t: (0, h)),
            pl.BlockSpec(masks.shape, lambda i, h, t: (0, 0, 0)),
        ],
        out_specs=pl.BlockSpec((1, tt, HEAD_DIM), lambda i, h, t: (i, t, h)),
        out_shape=jax.ShapeDtypeStruct((b, s, N_HEADS * HEAD_DIM), BF16),
        scratch_shapes=[pltpu.VMEM((HEAD_DIM, HEAD_DIM), F32)],
        compiler_params=_cparams("parallel", "parallel", "arbitrary"),
        name="hgrn_heads",
    )(proj, proj, proj, proj, lb_logits, norm_w.reshape(1, -1), masks)


def _gdn_kernel(q_ref, k_ref, v_ref, gate_ref, gc_ref, gr_ref, cw_ref, alane_ref, blane_ref, acol_ref, bcol_ref,
                nw_ref, y_ref, s_ref, xq_ref, xk_ref, xv_ref, *, tt):
    c = CHUNK
    hd = HEAD_DIM
    pair = 2 * c
    n_pairs = tt // pair
    n_chunks = tt // c
    width = N_HEADS * hd

    @pl.when(pl.program_id(1) == 0)
    def _():
        s_ref[...] = jnp.zeros_like(s_ref)
        for r in (xq_ref, xk_ref, xv_ref):
            r[pl.ds(0, SUBLANES), :] = jnp.zeros((SUBLANES, width), F32)

    q_all = _silu(_causal_conv(xq_ref, q_ref[0], cw_ref[:, 0:width], tt))
    k_all = _silu(_causal_conv(xk_ref, k_ref[0], cw_ref[:, width:2 * width], tt))
    v_all = _silu(_causal_conv(xv_ref, v_ref[0], cw_ref[:, 2 * width:3 * width], tt))

    gates = gc_ref[0]
    beta_all = jax.nn.sigmoid(gates)
    gam_all = -jnp.exp(alane_ref[...]) * _softplus(gates + blane_ref[...])
    row = lax.broadcasted_iota(jnp.int32, (tt, LANES), 0)
    rc = row & (c - 1)
    for s in (1, 2, 4, 8, 16, 32):
        gam_all = gam_all + jnp.where(rc >= s, pltpu.roll(gam_all, s, axis=0), 0.0)
    glast_all = _segment_row(gam_all, c, c - 1)
    gam_rows = -jnp.exp(acol_ref[...]) * _softplus(gr_ref[0] + bcol_ref[...])
    lc = lax.broadcasted_iota(jnp.int32, gam_rows.shape, 1) & (c - 1)
    for s in (1, 2, 4, 8, 16, 32):
        gam_rows = gam_rows + jnp.where(lc >= s, pltpu.roll(gam_rows, s, axis=1), 0.0)

    ii = lax.broadcasted_iota(jnp.int32, (c, LANES), 0)
    ll = lax.broadcasted_iota(jnp.int32, (c, LANES), 1)
    jj = ll & (c - 1)
    left = ll < c
    lower = ii >= jj
    strict = ii > jj
    eye2 = jnp.where(ii == jj, 1.0, 0.0)

    def block_diag(m):
        zero = jnp.zeros_like(m)
        return jnp.concatenate([jnp.where(left, m, zero), jnp.where(left, zero, m)], axis=0)

    def split2(x):
        hi = x.astype(BF16)
        return hi, (x - hi.astype(F32)).astype(BF16)

    def times_block_diag(lhs, p_hi, p_lo):
        l_hi, l_lo = split2(lhs)
        return _dot(jnp.concatenate([l_hi, l_lo, l_hi], axis=1), jnp.concatenate([p_hi, p_hi, p_lo], axis=0))

    units = [(h, p) for h in range(N_HEADS) for p in range(n_pairs)]
    qd, kd, rhs, khb, kbb, qhb, gam_h = [], [], [], [], [], [], []
    for h in range(N_HEADS):
        lanes = slice(h * hd, (h + 1) * hd)
        qh, kh, vh = q_all[:, lanes], k_all[:, lanes], v_all[:, lanes]
        qh = qh * lax.rsqrt(jnp.sum(qh * qh, axis=-1, keepdims=True) + L2_EPS) * (hd ** -0.5)
        kh = kh * lax.rsqrt(jnp.sum(kh * kh, axis=-1, keepdims=True) + L2_EPS)
        beta = beta_all[:, h:h + 1]
        gam = gam_all[:, N_HEADS + h:N_HEADS + h + 1]
        eg = jnp.exp(gam)
        kb = kh * beta
        gam_h.append(gam)
        rhs.append(jnp.concatenate([kb * eg, vh * beta], axis=1).astype(BF16))
        qd.append((qh * eg).astype(BF16))
        kd.append((kh * jnp.exp(glast_all[:, N_HEADS + h:N_HEADS + h + 1] - gam)).astype(BF16))
        khb.append(kh.astype(BF16))
        kbb.append(kb.astype(BF16))
        qhb.append(qh.astype(BF16))

    x_mats, qk_tiles = [], [[] for _ in range(N_HEADS)]
    for h, p in units:
        r0 = p * pair
        rows = slice(r0, r0 + pair)
        res = _dot_nt(jnp.concatenate([kbb[h][rows], qhb[h][rows]], axis=0), khb[h][rows])
        col = jnp.where(left, jnp.broadcast_to(gam_h[h][r0:r0 + c], (c, LANES)),
                        jnp.broadcast_to(gam_h[h][r0 + c:r0 + pair], (c, LANES)))
        decay = jnp.exp(jnp.minimum(col - gam_rows[N_HEADS + h:N_HEADS + h + 1, rows], 0.0))
        x_mats.append(jnp.where(strict, jnp.where(left, res[0:c], res[c:pair]) * -decay, 0.0))
        qk_tiles[h].append(jnp.where(lower & left, res[pair:pair + c] * decay, 0.0).astype(BF16))
        qk_tiles[h].append(jnp.where(lower & (~left), res[pair + c:2 * pair] * decay, 0.0).astype(BF16))

    t_mats = [eye2 + x for x in x_mats]
    powers = []
    for x in x_mats:
        p_hi, p_lo = split2(x)
        powers.append(times_block_diag(x, block_diag(p_hi), block_diag(p_lo)))
    for it in range(5):
        for u in range(len(units)):
            p_hi, p_lo = split2(powers[u])
            bd_hi, bd_lo = block_diag(p_hi), block_diag(p_lo)
            if it < 4:
                both = times_block_diag(jnp.concatenate([powers[u], t_mats[u]], axis=0), bd_hi, bd_lo)
                powers[u], t_mats[u] = both[0:c], t_mats[u] + both[c:pair]
            else:
                t_mats[u] = t_mats[u] + times_block_diag(t_mats[u], bd_hi, bd_lo)
    wu = [[] for _ in range(N_HEADS)]
    for u, (h, p) in enumerate(units):
        wu[h].append(_dot(block_diag(t_mats[u].astype(BF16)), rhs[h][p * pair:(p + 1) * pair]))

    outs = [[] for _ in range(N_HEADS)]
    zeros_half = jnp.zeros((c, hd), BF16)
    heads = range(N_HEADS)
    states = [s_ref[h] for h in heads]
    chunk_decay = jnp.exp(glast_all)
    for ci in range(n_chunks):
        rows = slice(ci * c, (ci + 1) * c)
        half = slice((ci % 2) * c, (ci % 2 + 1) * c)
        stb = [states[h].astype(BF16) for h in heads]
        wu_c = [wu[h][ci // 2][half] for h in heads]
        w_s = [_dot(wu_c[h][:, :hd].astype(BF16), stb[h]) for h in heads]
        vnb = [(wu_c[h][:, hd:] - w_s[h]).astype(BF16) for h in heads]
        for h in heads:
            v2 = jnp.concatenate([vnb[h], zeros_half] if ci % 2 == 0 else [zeros_half, vnb[h]], axis=0)
            outs[h].append(_dot(qd[h][rows], stb[h]) + _dot(qk_tiles[h][ci], v2))
        states = [states[h] * chunk_decay[ci * c:ci * c + 1, N_HEADS + h:N_HEADS + h + 1]
                  + _dot_tn(kd[h][rows], vnb[h]) for h in heads]
    for h in heads:
        s_ref[h] = states[h]

    for h in range(N_HEADS):
        lanes = slice(h * hd, (h + 1) * hd)
        o = jnp.concatenate(outs[h], axis=0)
        y = o * lax.rsqrt(jnp.mean(o * o, axis=-1, keepdims=True) + NORM_EPS)
        y_ref[0, :, lanes] = (y * nw_ref[:, lanes] * _silu(gate_ref[0, :, lanes])).astype(y_ref.dtype)


def gdn_heads(proj, gates_col, gates_row, conv_w, a_log, dt_bias, norm_w, col0, tt):
    b, s, _ = proj.shape
    width = N_HEADS * HEAD_DIM
    blk = lambda o: pl.BlockSpec((1, tt, width), lambda i, t: (i, t, col0 + o))
    whole = lambda shape: pl.BlockSpec(shape, lambda i, t: (0,) * len(shape))
    lane_vec = jnp.zeros((1, LANES), F32).at[0, N_HEADS:2 * N_HEADS]
    col_vec = jnp.zeros((2 * N_HEADS, 1), F32).at[N_HEADS:, 0]
    return pl.pallas_call(
        functools.partial(_gdn_kernel, tt=tt),
        grid=(b, s // tt),
        in_specs=[
            blk(0), blk(1), blk(2), blk(3),
            pl.BlockSpec((1, tt, LANES), lambda i, t: (i, t, 0)),
            pl.BlockSpec((1, 2 * N_HEADS, tt), lambda i, t: (i, 0, t)),
            whole(conv_w.shape),
            whole((1, LANES)), whole((1, LANES)), whole((2 * N_HEADS, 1)), whole((2 * N_HEADS, 1)),
            whole((1, width)),
        ],
        out_specs=pl.BlockSpec((1, tt, width), lambda i, t: (i, t, 0)),
        out_shape=jax.ShapeDtypeStruct((b, s, width), BF16),
        scratch_shapes=[
            pltpu.VMEM((N_HEADS, HEAD_DIM, HEAD_DIM), F32),
            pltpu.VMEM((tt + SUBLANES, width), F32),
            pltpu.VMEM((tt + SUBLANES, width), F32),
            pltpu.VMEM((tt + SUBLANES, width), F32),
        ],
        compiler_params=_cparams("parallel", "arbitrary"),
        name="gdn_heads",
    )(proj, proj, proj, proj, gates_col, gates_row, conv_w,
      lane_vec.set(a_log), lane_vec.set(dt_bias), col_vec.set(a_log), col_vec.set(dt_bias),
      norm_w.reshape(1, -1))


def _ssd_tables():
    c = CHUNK
    r = np.arange(c)[:, None]
    t = np.arange(c)[None, :]
    col_sums = np.concatenate([(t <= r), (t > r)], axis=0).astype(np.float32)
    row_cum = (r <= t).astype(np.float32)
    expand = np.zeros((SSD_GROUPS, LANES, SSD_GROUP_WIDTH), np.float32)
    for g in range(SSD_GROUPS):
        for hh in range(SSD_HEADS_PER_GROUP):
            expand[g, g * SSD_HEADS_PER_GROUP + hh, hh * SSD_HEADDIM:(hh + 1) * SSD_HEADDIM] = 1.0
    return jnp.asarray(col_sums, BF16), jnp.asarray(row_cum, BF16), jnp.asarray(expand, BF16)


def _ssd_kernel(z_ref, x_ref, b_ref, c_ref, dtc_ref, dtr_ref, cwx_ref, cwb_ref, cwc_ref,
                cbx_ref, cbb_ref, cbc_ref, dtbe_ref, aloge_ref, de_ref, dtbc_ref, alogc_ref,
                exp_ref, csum_ref, rcum_ref, y_ref,
                ht_ref, xx_ref, xb_ref, xc_ref, *, tt):
    c = CHUNK
    n_chunks = tt // c
    hp = SSD_HEADDIM

    @pl.when(pl.program_id(2) == 0)
    def _():
        ht_ref[...] = jnp.zeros_like(ht_ref)
        for r in (xx_ref, xb_ref, xc_ref):
            r[pl.ds(0, SUBLANES), :] = jnp.zeros((SUBLANES, r.shape[1]), F32)

    xs = _silu(_causal_conv(xx_ref, x_ref[0], cwx_ref[...], tt) + cbx_ref[...])
    bm = _silu(_causal_conv(xb_ref, b_ref[0], cwb_ref[...], tt) + cbb_ref[...])
    cm = _silu(_causal_conv(xc_ref, c_ref[0], cwc_ref[...], tt) + cbc_ref[...])

    dt_e = _softplus(_dot_sel(dtc_ref[0], exp_ref[0]) + dtbe_ref[...])
    da_e = dt_e * -jnp.exp(aloge_ref[...])
    xdt = xs * dt_e
    da_r = _softplus(dtr_ref[0] + dtbc_ref[...]) * -jnp.exp(alogc_ref[...])

    ii = lax.broadcasted_iota(jnp.int32, (c, LANES), 0)
    jj = lax.broadcasted_iota(jnp.int32, (c, LANES), 1)
    lower2 = ii >= (jj % c)
    left = jj < hp
    csum = csum_ref[...]
    rcum = rcum_ref[...]
    d_skip = de_ref[...]

    for ci in range(n_chunks):
        rows = slice(ci * c, (ci + 1) * c)
        e = _sel_dot(csum, da_e[rows])
        acum = e[0:c]
        sfx = e[c:2 * c]
        acum_r = _dot_sel(da_r[:, rows], rcum)
        bmc = bm[rows].astype(BF16)
        cmc = cm[rows].astype(BF16)
        xdc = xdt[rows]
        cb2 = _dot_nt(cmc, jnp.concatenate([bmc, bmc], axis=0))
        ydiag = []
        for pr in range(SSD_HEADS_PER_GROUP // 2):
            lanes = slice(2 * pr * hp, (2 * pr + 2) * hp)
            col = acum[:, lanes]
            row = jnp.concatenate([acum_r[2 * pr:2 * pr + 1, :], acum_r[2 * pr + 1:2 * pr + 2, :]], axis=1)
            lmat = jnp.where(lower2, jnp.exp(jnp.minimum(col - row, 0.0)), 0.0)
            xp = xdc[:, lanes]
            rhs = jnp.concatenate([jnp.where(left, xp, 0.0), jnp.where(left, 0.0, xp)], axis=0)
            ydiag.append(_dot((cb2 * lmat).astype(BF16), rhs.astype(BF16)))
        y = jnp.concatenate(ydiag, axis=1)
        ht = ht_ref[...]
        y = y + _dot(cmc, ht.astype(BF16)) * jnp.exp(acum)
        ht_ref[...] = ht * jnp.exp(acum[c - 1:c, :]) + _dot_tn(bmc, (xdc * jnp.exp(sfx)).astype(BF16))
        y = y + d_skip * xs[rows]
        y_ref[0, rows, :] = y * _silu(z_ref[0, rows, :])


def ssd_groups(proj, dt_col, dt_row, conv_w, conv_b, dt_bias, a_log, d_skip, tt):
    b, s, _ = proj.shape
    g_w = SSD_GROUP_WIDTH
    n_xblk = SSD_GROUPS
    csum, rcum, expand = _ssd_tables()
    rep = lambda p: jnp.repeat(p.astype(F32), SSD_HEADDIM).reshape(1, -1)
    col = lambda p: p.astype(F32).reshape(-1, 1)
    row2 = lambda v: v.reshape(1, -1)
    wide = lambda o: pl.BlockSpec((1, tt, g_w), lambda i, g, t: (i, t, o + g))
    narrow = lambda o: pl.BlockSpec((1, tt, LANES), lambda i, g, t: (i, t, o + g))
    x_off = n_xblk
    b_off = 2 * n_xblk * (g_w // LANES)
    c_off = b_off + SSD_GROUPS
    return pl.pallas_call(
        functools.partial(_ssd_kernel, tt=tt),
        grid=(b, SSD_GROUPS, s // tt),
        in_specs=[
            wide(0), wide(x_off), narrow(b_off), narrow(c_off),
            pl.BlockSpec((1, tt, LANES), lambda i, g, t: (i, t, 0)),
            pl.BlockSpec((1, SSD_HEADS_PER_GROUP, tt), lambda i, g, t: (i, g, t)),
            pl.BlockSpec((CONV_K, g_w), lambda i, g, t: (0, g)),
            pl.BlockSpec((CONV_K, LANES), lambda i, g, t: (0, n_xblk * (g_w // LANES) + g)),
            pl.BlockSpec((CONV_K, LANES), lambda i, g, t: (0, n_xblk * (g_w // LANES) + SSD_GROUPS + g)),
            pl.BlockSpec((1, g_w), lambda i, g, t: (0, g)),
            pl.BlockSpec((1, LANES), lambda i, g, t: (0, n_xblk * (g_w // LANES) + g)),
            pl.BlockSpec((1, LANES), lambda i, g, t: (0, n_xblk * (g_w // LANES) + SSD_GROUPS + g)),
            pl.BlockSpec((1, g_w), lambda i, g, t: (0, g)),
            pl.BlockSpec((1, g_w), lambda i, g, t: (0, g)),
            pl.BlockSpec((1, g_w), lambda i, g, t: (0, g)),
            pl.BlockSpec((SSD_HEADS_PER_GROUP, 1), lambda i, g, t: (g, 0)),
            pl.BlockSpec((SSD_HEADS_PER_GROUP, 1), lambda i, g, t: (g, 0)),
            pl.BlockSpec((1, LANES, g_w), lambda i, g, t: (g, 0, 0)),
            pl.BlockSpec(csum.shape, lambda i, g, t: (0, 0)),
            pl.BlockSpec(rcum.shape, lambda i, g, t: (0, 0)),
        ],
        out_specs=pl.BlockSpec((1, tt, g_w), lambda i, g, t: (i, t, g)),
        out_shape=jax.ShapeDtypeStruct((b, s, SSD_GROUPS * g_w), F32),
        scratch_shapes=[
            pltpu.VMEM((SSD_DSTATE, g_w), F32),
            pltpu.VMEM((tt + SUBLANES, g_w), F32),
            pltpu.VMEM((tt + SUBLANES, LANES), F32),
            pltpu.VMEM((tt + SUBLANES, LANES), F32),
        ],
        compiler_params=_cparams("parallel", "parallel", "arbitrary"),
        name="ssd_groups",
    )(proj, proj, proj, proj, dt_col, dt_row, conv_w, conv_w, conv_w,
      row2(conv_b), row2(conv_b), row2(conv_b), rep(dt_bias), rep(a_log), rep(d_skip),
      col(dt_bias), col(a_log), expand, csum, rcum)


def _pad_tail(w_tail):
    return jnp.pad(w_tail, ((0, 0), (0, LANES - w_tail.shape[1]))).astype(BF16)


def _tile(n, pref):
    t = min(n, pref)
    while n % t:
        t //= 2
    return t


def kernel(x, mem, norm_mix, norm_xattn, norm_mem, norm_ffn, norm_final, hy_w_in, hgrn_lb_logits, hgrn_norm,
           gdn_conv_w, gdn_a_log, gdn_dt_bias, gdn_norm, hy_w_out, ssd_w_in, ssd_conv_w, ssd_conv_b,
           ssd_dt_bias, ssd_a_log, ssd_d, ssd_norm, ssd_w_out, xa_wq, xa_wk, xa_wv, xa_wo, ffn_w_gate,
           ffn_w_up, ffn_w_down):
    bsz, seq, d = x.shape
    tok = bsz * seq
    depth = norm_mix.shape[0]
    tm = _tile(tok, 1024)
    tt = _tile(seq, SEQ_TILE)
    bf = lambda w: w.astype(BF16)

    h = x.reshape(tok, d)
    for layer in range(depth):
        if layer % 2 == 0:
            e = layer // 2
            n_main = 8 * N_HEADS * HEAD_DIM
            w_in = hy_w_in[e]
            proj, tail = norm_matmul(h, norm_mix[layer], bf(w_in[:, :n_main]), _pad_tail(w_in[:, n_main:]),
                                     tm, _tile(n_main, 1024))
            proj = proj.reshape(bsz, seq, n_main)
            gates_col = tail.reshape(bsz, seq, LANES)
            gates_row = jnp.swapaxes(gates_col[:, :, :2 * N_HEADS], 1, 2)
            y_a = hgrn_heads(proj, hgrn_lb_logits.astype(F32), hgrn_norm[e], e, 0, tt)
            y_b = gdn_heads(proj, gates_col, gates_row, gdn_conv_w[e].astype(F32), gdn_a_log[e].astype(F32),
                            gdn_dt_bias[e].astype(F32), gdn_norm[e], 4, _tile(seq, GDN_SEQ_TILE))
            y = jnp.concatenate([y_a, y_b], axis=-1).reshape(tok, -1)
            h = proj_residual(y, jnp.ones((y.shape[1],), F32), bf(hy_w_out[e]), h,
                              _tile(tok, 512), _tile(d, 1024), False)
        else:
            o = layer // 2
            n_heads = ssd_dt_bias.shape[1]
            w_in = ssd_w_in[o]
            n_main = w_in.shape[1] - n_heads
            proj, tail = norm_matmul(h, norm_mix[layer], bf(w_in[:, :n_main]), _pad_tail(w_in[:, n_main:]),
                                     tm, _tile(n_main, 1024))
            proj = proj.reshape(bsz, seq, n_main)
            dt_col = tail.reshape(bsz, seq, LANES)
            dt_row = jnp.swapaxes(dt_col[:, :, :n_heads], 1, 2)
            y = ssd_groups(proj, dt_col, dt_row, ssd_conv_w[o].astype(F32), ssd_conv_b[o].astype(F32),
                           ssd_dt_bias[o], ssd_a_log[o], ssd_d[o], tt)
            h = proj_residual(y.reshape(tok, -1), ssd_norm[o], bf(ssd_w_out[o]), h,
                              _tile(tok, 512), _tile(d, 1024), True)
        k_mem, v_mem = mem_kv(mem, norm_mem[layer], bf(xa_wk[layer]), bf(xa_wv[layer]))
        h = xattn_block(h.reshape(bsz, seq, d), norm_xattn[layer], bf(xa_wq[layer]), k_mem, v_mem,
                        bf(xa_wo[layer]), _tile(seq, 512)).reshape(tok, d)
        h = ffn_block(h, norm_ffn[layer], bf(ffn_w_gate[layer]), bf(ffn_w_up[layer]), bf(ffn_w_down[layer]),
                      norm_final, _tile(tok, 512), _tile(ffn_w_gate.shape[2], 512), layer == depth - 1)
    return h.reshape(bsz, seq, d)
```

```python
import functools

import numpy as np
import jax
import jax.numpy as jnp
from jax import lax
from jax.experimental import pallas as pl
from jax.experimental.pallas import tpu as pltpu

F32 = jnp.float32
BF16 = jnp.bfloat16

NORM_EPS = 1e-6
L2_EPS = 1e-6
CHUNK = 64
CONV_K = 4
LANES = 128
SUBLANES = 8
HEAD_DIM = 128
N_HEADS = 8
SSD_HEADDIM = 64
SSD_GROUPS = 8
SSD_HEADS_PER_GROUP = 8
SSD_GROUP_WIDTH = SSD_HEADDIM * SSD_HEADS_PER_GROUP
SSD_DSTATE = 128
XA_HEADS = 4
XA_HEAD_DIM = 128
VMEM_LIMIT_BYTES = 56 * 1024 * 1024
TOKEN_TILE = 1024
COLUMN_TILE = 1024
FFN_TILE = 256
SEQ_TILE = 512
GDN_SEQ_TILE = 256


def _cparams(*sem):
    return pltpu.CompilerParams(dimension_semantics=sem, vmem_limit_bytes=VMEM_LIMIT_BYTES)


def _dot(a, b):
    return jnp.dot(a, b, preferred_element_type=F32)


def _dot_nt(a, b):
    return lax.dot_general(a, b, (((1,), (1,)), ((), ())), preferred_element_type=F32)


def _dot_tn(a, b):
    return lax.dot_general(a, b, (((0,), (0,)), ((), ())), preferred_element_type=F32)


def _split3(x):
    hi = x.astype(BF16)
    r1 = x - hi.astype(F32)
    mid = r1.astype(BF16)
    lo = (r1 - mid.astype(F32)).astype(BF16)
    return hi, mid, lo


def _silu(x):
    return x * jax.nn.sigmoid(x)


def _softplus(x):
    return jnp.maximum(x, 0.0) + jnp.log(1.0 + jnp.exp(-jnp.abs(x)))


def _rms_rows(x, gain):
    ms = jnp.mean(x * x, axis=-1, keepdims=True)
    return x * lax.rsqrt(ms + NORM_EPS) * gain


def _causal_conv(carry_ref, x, w):
    rows = x.shape[0]
    ext = jnp.concatenate([carry_ref[...], x], axis=0)
    y = w[CONV_K - 1:CONV_K, :] * x
    for k in range(CONV_K - 1):
        y = y + w[k:k + 1, :] * pltpu.roll(ext, CONV_K - 1 - k, axis=0)[SUBLANES:]
    carry_ref[...] = x[rows - SUBLANES:]
    return y


def _norm_cast_kernel(x_ref, g_ref, o_ref):
    o_ref[...] = _rms_rows(x_ref[...], g_ref[...]).astype(BF16)


def norm_cast(x, gain, tm):
    m, k = x.shape
    return pl.pallas_call(
        _norm_cast_kernel,
        grid=(m // tm,),
        in_specs=[pl.BlockSpec((tm, k), lambda i: (i, 0)), pl.BlockSpec((1, k), lambda i: (0, 0))],
        out_specs=pl.BlockSpec((tm, k), lambda i: (i, 0)),
        out_shape=jax.ShapeDtypeStruct((m, k), BF16),
        compiler_params=_cparams("parallel"),
        name="norm_cast",
    )(x, gain.reshape(1, k))


def _ws_matmul_kernel(*refs, n_a, normalize, residual):
    a_refs = refs[:n_a]
    rest = list(refs[n_a:])
    g_ref = rest.pop(0) if normalize else None
    w_ref = rest.pop(0)
    r_ref = rest.pop(0) if residual else None
    o_ref, wb_ref = rest

    @pl.when(pl.program_id(1) == 0)
    def _():
        wb_ref[...] = w_ref[0].astype(BF16)

    acc = None
    k0 = 0
    for a_ref in a_refs:
        a = a_ref[...]
        if normalize:
            a = _rms_rows(a, g_ref[...]).astype(BF16)
        part = _dot(a, wb_ref[k0:k0 + a.shape[1], :])
        k0 += a.shape[1]
        acc = part if acc is None else acc + part
    o_ref[...] = r_ref[...] + acc if residual else acc


def ws_matmul(a_list, w, layer, n, tm, tn, gain=None, resid=None, single_buffer_w=False):
    m = a_list[0].shape[0]
    k = w.shape[1]
    normalize = gain is not None
    residual = resid is not None
    w_mode = dict(pipeline_mode=pl.Buffered(1)) if single_buffer_w else {}
    in_specs = [pl.BlockSpec((tm, a.shape[1]), lambda j, i: (i, 0)) for a in a_list]
    args = list(a_list)
    if normalize:
        in_specs.append(pl.BlockSpec((1, k), lambda j, i: (0, 0)))
        args.append(gain.reshape(1, k))
    in_specs.append(pl.BlockSpec((1, k, tn), lambda j, i: (layer, 0, j), **w_mode))
    args.append(w)
    if residual:
        in_specs.append(pl.BlockSpec((tm, tn), lambda j, i: (i, j)))
        args.append(resid)
    return pl.pallas_call(
        functools.partial(_ws_matmul_kernel, n_a=len(a_list), normalize=normalize, residual=residual),
        grid=(n // tn, m // tm),
        in_specs=in_specs,
        out_specs=pl.BlockSpec((tm, tn), lambda j, i: (i, j)),
        out_shape=jax.ShapeDtypeStruct((m, n), F32),
        scratch_shapes=[pltpu.VMEM((k, tn), BF16)],
        compiler_params=_cparams("arbitrary", "arbitrary"),
        name="ws_matmul",
    )(*args)


def _ffn_kernel(x_ref, g_ref, wg_ref, wu_ref, wd_ref, gf_ref, o_ref, xn_ref, *, final_norm):
    f = pl.program_id(1)

    @pl.when(f == 0)
    def _():
        x = x_ref[...]
        xn_ref[...] = _rms_rows(x, g_ref[...]).astype(BF16)
        o_ref[...] = x

    xn = xn_ref[...]
    act = (_silu(_dot(xn, wg_ref[0].astype(BF16))) * _dot(xn, wu_ref[0].astype(BF16))).astype(BF16)
    o_ref[...] += _dot(act, wd_ref[0].astype(BF16))

    if final_norm:
        @pl.when(f == pl.num_programs(1) - 1)
        def _():
            o_ref[...] = _rms_rows(o_ref[...], gf_ref[...])


def ffn_block(x, gain, wg, wu, wd, layer, gain_final, tm, tf, final_norm):
    m, d = x.shape
    dff = wg.shape[2]
    return pl.pallas_call(
        functools.partial(_ffn_kernel, final_norm=final_norm),
        grid=(m // tm, dff // tf),
        in_specs=[
            pl.BlockSpec((tm, d), lambda i, f: (i, 0), pipeline_mode=pl.Buffered(1)),
            pl.BlockSpec((1, d), lambda i, f: (0, 0)),
            pl.BlockSpec((1, d, tf), lambda i, f: (layer, 0, f)),
            pl.BlockSpec((1, d, tf), lambda i, f: (layer, 0, f)),
            pl.BlockSpec((1, tf, d), lambda i, f: (layer, f, 0)),
            pl.BlockSpec((1, d), lambda i, f: (0, 0)),
        ],
        out_specs=pl.BlockSpec((tm, d), lambda i, f: (i, 0)),
        out_shape=jax.ShapeDtypeStruct((m, d), F32),
        scratch_shapes=[pltpu.VMEM((tm, d), BF16)],
        compiler_params=_cparams("parallel", "arbitrary"),
        name="ffn_block",
    )(x, gain.reshape(1, d), wg, wu, wd, gain_final.reshape(1, d))


def _mem_kv_kernel(m_ref, g_ref, wk_ref, wv_ref, k_ref, v_ref):
    mn = _rms_rows(m_ref[0], g_ref[...]).astype(BF16)
    k_ref[0] = _dot(mn, wk_ref[0].astype(BF16)).astype(BF16)
    v_ref[0] = _dot(mn, wv_ref[0].astype(BF16)).astype(BF16)


def mem_kv(mem, gain, wk, wv, layer):
    b, ml, d = mem.shape
    xa = wk.shape[2]
    return pl.pallas_call(
        _mem_kv_kernel,
        grid=(b,),
        in_specs=[
            pl.BlockSpec((1, ml, d), lambda i: (i, 0, 0)),
            pl.BlockSpec((1, d), lambda i: (0, 0)),
            pl.BlockSpec((1, d, xa), lambda i: (layer, 0, 0)),
            pl.BlockSpec((1, d, xa), lambda i: (layer, 0, 0)),
        ],
        out_specs=[pl.BlockSpec((1, ml, xa), lambda i: (i, 0, 0))] * 2,
        out_shape=[jax.ShapeDtypeStruct((b, ml, xa), BF16)] * 2,
        compiler_params=_cparams("parallel"),
        name="mem_kv",
    )(mem, gain.reshape(1, d), wk, wv)


def _xattn_kernel(x_ref, g_ref, wq_ref, k_ref, v_ref, wo_ref, o_ref, wqb_ref, wob_ref):
    @pl.when((pl.program_id(0) == 0) & (pl.program_id(1) == 0))
    def _():
        wqb_ref[...] = wq_ref[0].astype(BF16)
        wob_ref[...] = wo_ref[0].astype(BF16)

    x = x_ref[0]
    xn = _rms_rows(x, g_ref[...]).astype(BF16)
    q = (_dot(xn, wqb_ref[...]) * (XA_HEAD_DIM ** -0.5)).astype(BF16)
    outs = []
    for h in range(XA_HEADS):
        sl = slice(h * XA_HEAD_DIM, (h + 1) * XA_HEAD_DIM)
        s = _dot_nt(q[:, sl], k_ref[0, :, sl])
        p = jnp.exp(s - jnp.max(s, axis=-1, keepdims=True))
        den = jnp.sum(p, axis=-1, keepdims=True)
        outs.append(_dot(p.astype(BF16), v_ref[0, :, sl]) / den)
    o = jnp.concatenate(outs, axis=-1).astype(BF16)
    o_ref[0] = x + _dot(o, wob_ref[...])


def xattn_block(x, gain, wq, k, v, wo, layer, tm):
    b, s, d = x.shape
    ml, xa = k.shape[1], k.shape[2]
    return pl.pallas_call(
        _xattn_kernel,
        grid=(b, s // tm),
        in_specs=[
            pl.BlockSpec((1, tm, d), lambda i, t: (i, t, 0)),
            pl.BlockSpec((1, d), lambda i, t: (0, 0)),
            pl.BlockSpec((1, d, xa), lambda i, t: (layer, 0, 0)),
            pl.BlockSpec((1, ml, xa), lambda i, t: (i, 0, 0)),
            pl.BlockSpec((1, ml, xa), lambda i, t: (i, 0, 0)),
            pl.BlockSpec((1, xa, d), lambda i, t: (layer, 0, 0)),
        ],
        out_specs=pl.BlockSpec((1, tm, d), lambda i, t: (i, t, 0)),
        out_shape=jax.ShapeDtypeStruct((b, s, d), F32),
        scratch_shapes=[pltpu.VMEM((d, xa), BF16), pltpu.VMEM((xa, d), BF16)],
        compiler_params=_cparams("arbitrary", "arbitrary"),
        name="xattn_block",
    )(x, gain.reshape(1, d), wq, k, v, wo)


_HGRN_LEVELS = (32, 16, 8, 4, 2, 1)


def _hgrn_masks():
    c = CHUNK
    i = np.arange(c)[:, None]
    j = np.arange(c)[None, :]
    masks = [(i // (2 * s) == j // (2 * s)) & (i % (2 * s) >= s) & (j % (2 * s) < s) for s in _HGRN_LEVELS]
    masks.append(i == j)
    return jnp.asarray(np.stack(masks).astype(np.float32))


def _segment_row(x, seg, idx):
    rows, lanes = x.shape
    x3 = x.reshape(rows // seg, seg, lanes)
    return jnp.broadcast_to(x3[:, idx:idx + 1, :], x3.shape).reshape(rows, lanes)


def _hgrn_kernel(q_ref, f_ref, i_ref, gate_ref, lbl_ref, nw_ref, masks_ref, y_ref, st_ref, *, layer, n_chunks):
    c = CHUNK
    tt = n_chunks * c

    @pl.when(pl.program_id(2) == 0)
    def _():
        st_ref[...] = jnp.zeros_like(st_ref)

    logits = lbl_ref[...]
    ex = jnp.exp(logits - jnp.max(logits, axis=0, keepdims=True))
    sm = ex / jnp.sum(ex, axis=0, keepdims=True)
    lb = jnp.sum(sm[0:layer + 1, :], axis=0, keepdims=True)

    a_f = f_ref[0]
    q = _silu(q_ref[0])
    log_f = jnp.log(lb + (1.0 - lb) * jax.nn.sigmoid(a_f))
    k = (1.0 - lb) * jax.nn.sigmoid(-a_f)
    vb = i_ref[0].astype(BF16)

    row = lax.broadcasted_iota(jnp.int32, (tt, HEAD_DIM), 0)
    rc = row & (c - 1)
    g = log_f
    for s in (1, 2, 4, 8, 16, 32):
        g = g + jnp.where(rc >= s, pltpu.roll(g, s, axis=0), 0.0)
    g_last = _segment_row(g, c, c - 1)

    exps = []
    for s in _HGRN_LEVELS:
        if 2 * s >= SUBLANES:
            exps.append(-jnp.abs(g - _segment_row(g, 2 * s, s - 1)))
        elif s == 2:
            r4 = row & 3
            below = pltpu.roll(log_f, 1, axis=0)
            above = pltpu.roll(log_f, tt - 1, axis=0)
            exps.append(jnp.where(r4 == 0, above, jnp.where(r4 == 1, 0.0,
                                                            jnp.where(r4 == 2, log_f, log_f + below))))
        else:
            exps.append(jnp.where((row & 1) == 1, log_f, 0.0))

    def chunks(x):
        xb = x.astype(BF16)
        return [xb[ci * c:(ci + 1) * c] for ci in range(n_chunks)]

    qs, ks = chunks(q), chunks(k)
    n_lv = len(_HGRN_LEVELS)
    attn = [masks_ref[n_lv] * _dot_nt(qs[ci], ks[ci]) for ci in range(n_chunks)]
    for lv in range(n_lv):
        sc = jnp.exp(exps[lv])
        ql, kl = chunks(q * sc), chunks(k * sc)
        m = masks_ref[lv]
        attn = [attn[ci] + m * _dot_nt(ql[ci], kl[ci]) for ci in range(n_chunks)]

    vs = chunks(vb)
    qg = chunks(q * jnp.exp(g))
    kd = chunks(k * jnp.exp(g_last - g))
    upd = [_dot_tn(vs[ci], kd[ci]) for ci in range(n_chunks)]
    intra = [_dot(attn[ci].astype(BF16), vs[ci]) for ci in range(n_chunks)]
    dec = jnp.exp(g_last)
    st = st_ref[...]
    outs = []
    for ci in range(n_chunks):
        outs.append(intra[ci] + _dot_nt(qg[ci], st.astype(BF16)))
        st = st * dec[ci * c:ci * c + 1, :] + upd[ci]
    st_ref[...] = st
    o = jnp.concatenate(outs, axis=0)
    y = o * lax.rsqrt(jnp.mean(o * o, axis=-1, keepdims=True) + NORM_EPS)
    y_ref[0] = (y * nw_ref[...] * _silu(gate_ref[0])).astype(y_ref.dtype)


def hgrn_heads(proj, lb_logits, norm_w, layer, col0, tt):
    b, s, _ = proj.shape
    nl = lb_logits.shape[0]
    masks = _hgrn_masks()
    blk = lambda o: pl.BlockSpec((1, tt, HEAD_DIM), lambda i, h, t: (i, t, col0 + o + h))
    return pl.pallas_call(
        functools.partial(_hgrn_kernel, layer=layer, n_chunks=tt // CHUNK),
        grid=(b, N_HEADS, s // tt),
        in_specs=[
            blk(0), blk(N_HEADS), blk(2 * N_HEADS), blk(3 * N_HEADS),
            pl.BlockSpec((nl, HEAD_DIM), lambda i, h, t: (0, h)),
            pl.BlockSpec((1, HEAD_DIM), lambda i, h, t: (0, h)),
            pl.BlockSpec(masks.shape, lambda i, h, t: (0, 0, 0)),
        ],
        out_specs=pl.BlockSpec((1, tt, HEAD_DIM), lambda i, h, t: (i, t, h)),
        out_shape=jax.ShapeDtypeStruct((b, s, N_HEADS * HEAD_DIM), BF16),
        scratch_shapes=[pltpu.VMEM((HEAD_DIM, HEAD_DIM), F32)],
        compiler_params=_cparams("parallel", "parallel", "arbitrary"),
        name="hgrn_heads",
    )(proj, proj, proj, proj, lb_logits, norm_w.reshape(1, -1), masks)


def _gdn_kernel(q_ref, k_ref, v_ref, gate_ref, gc_ref, gr_ref, cw_ref, alane_ref, blane_ref, acol_ref, bcol_ref,
                nw_ref, y_ref, s_ref, xq_ref, xk_ref, xv_ref, *, tt):
    c = CHUNK
    hd = HEAD_DIM
    pair = 2 * c
    n_pairs = tt // pair
    n_chunks = tt // c
    width = N_HEADS * hd

    @pl.when(pl.program_id(1) == 0)
    def _():
        s_ref[...] = jnp.zeros_like(s_ref)
        for r in (xq_ref, xk_ref, xv_ref):
            r[...] = jnp.zeros_like(r)

    q_all = _silu(_causal_conv(xq_ref, q_ref[0], cw_ref[:, 0:width]))
    k_all = _silu(_causal_conv(xk_ref, k_ref[0], cw_ref[:, width:2 * width]))
    v_all = _silu(_causal_conv(xv_ref, v_ref[0], cw_ref[:, 2 * width:3 * width]))

    gates = gc_ref[0]
    beta_all = jax.nn.sigmoid(gates)
    gam_all = -jnp.exp(alane_ref[...]) * _softplus(gates + blane_ref[...])
    row = lax.broadcasted_iota(jnp.int32, (tt, LANES), 0)
    rc = row & (c - 1)
    for s in (1, 2, 4, 8, 16, 32):
        gam_all = gam_all + jnp.where(rc >= s, pltpu.roll(gam_all, s, axis=0), 0.0)
    glast_all = _segment_row(gam_all, c, c - 1)
    gam_rows = -jnp.exp(acol_ref[...]) * _softplus(gr_ref[0] + bcol_ref[...])
    lc = lax.broadcasted_iota(jnp.int32, gam_rows.shape, 1) & (c - 1)
    for s in (1, 2, 4, 8, 16, 32):
        gam_rows = gam_rows + jnp.where(lc >= s, pltpu.roll(gam_rows, s, axis=1), 0.0)

    ii = lax.broadcasted_iota(jnp.int32, (c, LANES), 0)
    ll = lax.broadcasted_iota(jnp.int32, (c, LANES), 1)
    jj = ll & (c - 1)
    left = ll < c
    lower = ii >= jj
    strict = ii > jj
    eye2 = jnp.where(ii == jj, 1.0, 0.0)

    def block_diag(m):
        zero = jnp.zeros_like(m)
        return jnp.concatenate([jnp.where(left, m, zero), jnp.where(left, zero, m)], axis=0)

    def split2(x):
        hi = x.astype(BF16)
        return hi, (x - hi.astype(F32)).astype(BF16)

    def times_block_diag(lhs, p_hi, p_lo):
        l_hi, l_lo = split2(lhs)
        return _dot(jnp.concatenate([l_hi, l_lo, l_hi], axis=1), jnp.concatenate([p_hi, p_hi, p_lo], axis=0))

    units = [(h, p) for h in range(N_HEADS) for p in range(n_pairs)]
    qd, kd, rhs, khb, kbb, qhb, gam_h = [], [], [], [], [], [], []
    for h in range(N_HEADS):
        lanes = slice(h * hd, (h + 1) * hd)
        qh, kh, vh = q_all[:, lanes], k_all[:, lanes], v_all[:, lanes]
        qh = qh * lax.rsqrt(jnp.sum(qh * qh, axis=-1, keepdims=True) + L2_EPS) * (hd ** -0.5)
        kh = kh * lax.rsqrt(jnp.sum(kh * kh, axis=-1, keepdims=True) + L2_EPS)
        beta = beta_all[:, h:h + 1]
        gam = gam_all[:, N_HEADS + h:N_HEADS + h + 1]
        eg = jnp.exp(gam)
        kb = kh * beta
        gam_h.append(gam)
        rhs.append(jnp.concatenate([kb * eg, vh * beta], axis=1).astype(BF16))
        qd.append((qh * eg).astype(BF16))
        kd.append((kh * jnp.exp(glast_all[:, N_HEADS + h:N_HEADS + h + 1] - gam)).astype(BF16))
        khb.append(kh.astype(BF16))
        kbb.append(kb.astype(BF16))
        qhb.append(qh.astype(BF16))

    x_mats, qk_tiles = [], [[] for _ in range(N_HEADS)]
    for h, p in units:
        r0 = p * pair
        rows = slice(r0, r0 + pair)
        res = _dot_nt(jnp.concatenate([kbb[h][rows], qhb[h][rows]], axis=0), khb[h][rows])
        col = jnp.where(left, jnp.broadcast_to(gam_h[h][r0:r0 + c], (c, LANES)),
                        jnp.broadcast_to(gam_h[h][r0 + c:r0 + pair], (c, LANES)))
        decay = jnp.exp(jnp.minimum(col - gam_rows[N_HEADS + h:N_HEADS + h + 1, rows], 0.0))
        x_mats.append(jnp.where(strict, jnp.where(left, res[0:c], res[c:pair]) * -decay, 0.0))
        qk_tiles[h].append(jnp.where(lower & left, res[pair:pair + c] * decay, 0.0).astype(BF16))
        qk_tiles[h].append(jnp.where(lower & (~left), res[pair + c:2 * pair] * decay, 0.0).astype(BF16))

    t_mats = [eye2 + x for x in x_mats]
    powers = []
    for x in x_mats:
        p_hi, p_lo = split2(x)
        powers.append(times_block_diag(x, block_diag(p_hi), block_diag(p_lo)))
    for it in range(5):
        for u in range(len(units)):
            p_hi, p_lo = split2(powers[u])
            bd_hi, bd_lo = block_diag(p_hi), block_diag(p_lo)
            if it < 4:
                both = times_block_diag(jnp.concatenate([powers[u], t_mats[u]], axis=0), bd_hi, bd_lo)
                powers[u], t_mats[u] = both[0:c], t_mats[u] + both[c:pair]
            else:
                t_mats[u] = t_mats[u] + times_block_diag(t_mats[u], bd_hi, bd_lo)
    wu = [[] for _ in range(N_HEADS)]
    for u, (h, p) in enumerate(units):
        wu[h].append(_dot(block_diag(t_mats[u].astype(BF16)), rhs[h][p * pair:(p + 1) * pair]))

    outs = [[] for _ in range(N_HEADS)]
    zeros_half = jnp.zeros((c, hd), BF16)
    heads = range(N_HEADS)
    states = [s_ref[h] for h in heads]
    chunk_decay = jnp.exp(glast_all)
    for ci in range(n_chunks):
        rows = slice(ci * c, (ci + 1) * c)
        half = slice((ci % 2) * c, (ci % 2 + 1) * c)
        stb = [states[h].astype(BF16) for h in heads]
        wu_c = [wu[h][ci // 2][half] for h in heads]
        w_s = [_dot(wu_c[h][:, :hd].astype(BF16), stb[h]) for h in heads]
        vnb = [(wu_c[h][:, hd:] - w_s[h]).astype(BF16) for h in heads]
        for h in heads:
            v2 = jnp.concatenate([vnb[h], zeros_half] if ci % 2 == 0 else [zeros_half, vnb[h]], axis=0)
            outs[h].append(_dot(qd[h][rows], stb[h]) + _dot(qk_tiles[h][ci], v2))
        states = [states[h] * chunk_decay[ci * c:ci * c + 1, N_HEADS + h:N_HEADS + h + 1]
                  + _dot_tn(kd[h][rows], vnb[h]) for h in heads]
    for h in heads:
        s_ref[h] = states[h]

    for h in range(N_HEADS):
        lanes = slice(h * hd, (h + 1) * hd)
        o = jnp.concatenate(outs[h], axis=0)
        y = o * lax.rsqrt(jnp.mean(o * o, axis=-1, keepdims=True) + NORM_EPS)
        y_ref[0, :, lanes] = (y * nw_ref[:, lanes] * _silu(gate_ref[0, :, lanes])).astype(y_ref.dtype)


def gdn_heads(proj, gates_col, gates_row, conv_w, a_log, dt_bias, norm_w, col0, tt):
    b, s, _ = proj.shape
    width = N_HEADS * HEAD_DIM
    blk = lambda o: pl.BlockSpec((1, tt, width), lambda i, t: (i, t, col0 + o))
    whole = lambda shape: pl.BlockSpec(shape, lambda i, t: (0,) * len(shape))
    lane_vec = jnp.zeros((1, LANES), F32).at[0, N_HEADS:2 * N_HEADS]
    col_vec = jnp.zeros((2 * N_HEADS, 1), F32).at[N_HEADS:, 0]
    return pl.pallas_call(
        functools.partial(_gdn_kernel, tt=tt),
        grid=(b, s // tt),
        in_specs=[
            blk(0), blk(1), blk(2), blk(3),
            pl.BlockSpec((1, tt, LANES), lambda i, t: (i, t, 0)),
            pl.BlockSpec((1, 2 * N_HEADS, tt), lambda i, t: (i, 0, t)),
            whole(conv_w.shape),
            whole((1, LANES)), whole((1, LANES)), whole((2 * N_HEADS, 1)), whole((2 * N_HEADS, 1)),
            whole((1, width)),
        ],
        out_specs=pl.BlockSpec((1, tt, width), lambda i, t: (i, t, 0)),
        out_shape=jax.ShapeDtypeStruct((b, s, width), BF16),
        scratch_shapes=[
            pltpu.VMEM((N_HEADS, HEAD_DIM, HEAD_DIM), F32),
            pltpu.VMEM((SUBLANES, width), F32),
            pltpu.VMEM((SUBLANES, width), F32),
            pltpu.VMEM((SUBLANES, width), F32),
        ],
        compiler_params=_cparams("parallel", "arbitrary"),
        name="gdn_heads",
    )(proj, proj, proj, proj, gates_col, gates_row, conv_w,
      lane_vec.set(a_log), lane_vec.set(dt_bias), col_vec.set(a_log), col_vec.set(dt_bias),
      norm_w.reshape(1, -1))


SSD_PAIRS = SSD_HEADS_PER_GROUP // 2


def _ssd_expand_table():
    expand = np.zeros((SSD_GROUPS, LANES, SSD_GROUP_WIDTH), np.float32)
    for g in range(SSD_GROUPS):
        for hh in range(SSD_HEADS_PER_GROUP):
            expand[g, g * SSD_HEADS_PER_GROUP + hh, hh * SSD_HEADDIM:(hh + 1) * SSD_HEADDIM] = 1.0
    return jnp.asarray(np.concatenate([expand] * 3, axis=1), BF16)


def _ssd_kernel(z_ref, x_ref, b_ref, c_ref, dtc_ref, dtp_ref, cwx_ref, cwb_ref, cwc_ref,
                cbx_ref, cbb_ref, cbc_ref, blane_ref, alane_ref, bpair_ref, apair_ref, de_ref, exp_ref,
                y_ref, ht_ref, xx_ref, xb_ref, xc_ref, *, tt):
    c = CHUNK
    n_chunks = tt // c
    hp = SSD_HEADDIM

    @pl.when(pl.program_id(2) == 0)
    def _():
        for r in (ht_ref, xx_ref, xb_ref, xc_ref):
            r[...] = jnp.zeros_like(r)

    xs = _silu(_causal_conv(xx_ref, x_ref[0], cwx_ref[...]) + cbx_ref[...])
    bm = _silu(_causal_conv(xb_ref, b_ref[0], cwb_ref[...]) + cbb_ref[...]).astype(BF16)
    cm = _silu(_causal_conv(xc_ref, c_ref[0], cwc_ref[...]) + cbc_ref[...]).astype(BF16)

    dt_c = _softplus(dtc_ref[0] + blane_ref[...])
    acum_c = dt_c * -jnp.exp(alane_ref[...])
    rc = lax.broadcasted_iota(jnp.int32, (tt, LANES), 0) & (c - 1)
    for s in (1, 2, 4, 8, 16, 32):
        acum_c = acum_c + jnp.where(rc >= s, pltpu.roll(acum_c, s, axis=0), 0.0)
    s_c = dt_c * jnp.exp(_segment_row(acum_c, c, c - 1) - acum_c)
    a_hi, a_mid, a_lo = _split3(acum_c)
    acum_e = _dot(jnp.concatenate([a_hi, a_mid, a_lo], axis=1), exp_ref[0])
    s_hi, s_mid, _ = _split3(s_c)
    s_e = _dot(jnp.concatenate([s_hi, s_mid], axis=1), exp_ref[0, 0:2 * LANES, :])
    decay_e = jnp.exp(acum_e)
    xw = (xs * s_e).astype(BF16)
    xsb = xs.astype(BF16)

    dt_p = _softplus(dtp_ref[0, 0] + jnp.concatenate([bpair_ref[0]] * n_chunks, axis=1))
    acum_p = dt_p * -jnp.exp(jnp.concatenate([apair_ref[0]] * n_chunks, axis=1))
    lc = lax.broadcasted_iota(jnp.int32, acum_p.shape, 1) & (c - 1)
    for s in (1, 2, 4, 8, 16, 32):
        acum_p = acum_p + jnp.where(lc >= s, pltpu.roll(acum_p, s, axis=1), 0.0)

    ii = lax.broadcasted_iota(jnp.int32, (c, LANES), 0)
    ll = lax.broadcasted_iota(jnp.int32, (c, LANES), 1)
    lower2 = ii >= (ll & (c - 1))
    left = ll < hp

    chunk_rows = [slice(ci * c, (ci + 1) * c) for ci in range(n_chunks)]
    cb2 = [_dot_nt(cm[r], jnp.concatenate([bm[r], bm[r]], axis=0)) for r in chunk_rows]
    upd = [_dot_tn(bm[r], xw[r]) for r in chunk_rows]
    y_diag = []
    for ci, r in enumerate(chunk_rows):
        tiles = []
        for pr in range(SSD_PAIRS):
            lanes = slice(2 * pr * hp, (2 * pr + 2) * hp)
            tok = slice(ci * LANES, (ci + 1) * LANES)
            lmat = jnp.where(lower2, jnp.exp(jnp.minimum(acum_e[r, lanes] - acum_p[pr:pr + 1, tok], 0.0)), 0.0)
            xp = xsb[r, lanes]
            zero = jnp.zeros_like(xp)
            rhs = jnp.concatenate([jnp.where(left, xp, zero), jnp.where(left, zero, xp)], axis=0)
            tiles.append(_dot((cb2[ci] * lmat * dt_p[pr:pr + 1, tok]).astype(BF16), rhs))
        y_diag.append(jnp.concatenate(tiles, axis=1))
    ht = ht_ref[...]
    y_off = []
    for ci, r in enumerate(chunk_rows):
        y_off.append(_dot(cm[r], ht.astype(BF16)))
        ht = ht * decay_e[ci * c + c - 1:ci * c + c, :] + upd[ci]
    ht_ref[...] = ht
    y = jnp.concatenate(y_diag, axis=0) + jnp.concatenate(y_off, axis=0) * decay_e + de_ref[...] * xs
    y_ref[0] = y * _silu(z_ref[0])


def ssd_groups(proj, dt_col, dt_pairs, conv_w, conv_b, dt_bias, a_log, d_skip, tt):
    b, s, _ = proj.shape
    g_w = SSD_GROUP_WIDTH
    n_xblk = SSD_GROUPS
    expand = _ssd_expand_table()
    rep = lambda p: jnp.repeat(p.astype(F32), SSD_HEADDIM).reshape(1, -1)
    row2 = lambda v: v.reshape(1, -1)
    lane_vec = lambda p: jnp.pad(p.astype(F32), (0, LANES - p.shape[0])).reshape(1, LANES)

    def pair_tile(p):
        t = jnp.repeat(p.astype(F32).reshape(SSD_GROUPS, SSD_PAIRS, 2, 1), CHUNK, axis=-1)
        t = t.reshape(SSD_GROUPS, SSD_PAIRS, LANES)
        return jnp.pad(t, ((0, 0), (0, SUBLANES - SSD_PAIRS), (0, 0)))
    wide = lambda o: pl.BlockSpec((1, tt, g_w), lambda i, g, t: (i, t, o + g))
    narrow = lambda o: pl.BlockSpec((1, tt, LANES), lambda i, g, t: (i, t, o + g))
    x_off = n_xblk
    b_off = 2 * n_xblk * (g_w // LANES)
    c_off = b_off + SSD_GROUPS
    return pl.pallas_call(
        functools.partial(_ssd_kernel, tt=tt),
        grid=(b, SSD_GROUPS, s // tt),
        in_specs=[
            wide(0), wide(x_off), narrow(b_off), narrow(c_off),
            pl.BlockSpec((1, tt, LANES), lambda i, g, t: (i, t, 0)),
            pl.BlockSpec((1, 1, SUBLANES, 2 * tt), lambda i, g, t: (i, g, 0, t)),
            pl.BlockSpec((CONV_K, g_w), lambda i, g, t: (0, g)),
            pl.BlockSpec((CONV_K, LANES), lambda i, g, t: (0, n_xblk * (g_w // LANES) + g)),
            pl.BlockSpec((CONV_K, LANES), lambda i, g, t: (0, n_xblk * (g_w // LANES) + SSD_GROUPS + g)),
            pl.BlockSpec((1, g_w), lambda i, g, t: (0, g)),
            pl.BlockSpec((1, LANES), lambda i, g, t: (0, n_xblk * (g_w // LANES) + g)),
            pl.BlockSpec((1, LANES), lambda i, g, t: (0, n_xblk * (g_w // LANES) + SSD_GROUPS + g)),
            pl.BlockSpec((1, LANES), lambda i, g, t: (0, 0)),
            pl.BlockSpec((1, LANES), lambda i, g, t: (0, 0)),
            pl.BlockSpec((1, SUBLANES, LANES), lambda i, g, t: (g, 0, 0)),
            pl.BlockSpec((1, SUBLANES, LANES), lambda i, g, t: (g, 0, 0)),
            pl.BlockSpec((1, g_w), lambda i, g, t: (0, g)),
            pl.BlockSpec((1, 3 * LANES, g_w), lambda i, g, t: (g, 0, 0)),
        ],
        out_specs=pl.BlockSpec((1, tt, g_w), lambda i, g, t: (i, t, g)),
        out_shape=jax.ShapeDtypeStruct((b, s, SSD_GROUPS * g_w), F32),
        scratch_shapes=[
            pltpu.VMEM((SSD_DSTATE, g_w), F32),
            pltpu.VMEM((SUBLANES, g_w), F32),
            pltpu.VMEM((SUBLANES, LANES), F32),
            pltpu.VMEM((SUBLANES, LANES), F32),
        ],
        compiler_params=_cparams("parallel", "parallel", "arbitrary"),
        name="ssd_groups",
    )(proj, proj, proj, proj, dt_col, dt_pairs, conv_w, conv_w, conv_w,
      row2(conv_b), row2(conv_b), row2(conv_b), lane_vec(dt_bias), lane_vec(a_log),
      pair_tile(dt_bias), pair_tile(a_log), rep(d_skip), expand)


def _tail_weight(w, layer, n_main):
    tail = w[layer, :, n_main:]
    return jnp.pad(tail, ((0, 0), (0, LANES - tail.shape[1])))[None]


def _tile(n, pref):
    t = min(n, pref)
    while n % t:
        t //= 2
    return t


def kernel(x, mem, norm_mix, norm_xattn, norm_mem, norm_ffn, norm_final, hy_w_in, hgrn_lb_logits, hgrn_norm,
           gdn_conv_w, gdn_a_log, gdn_dt_bias, gdn_norm, hy_w_out, ssd_w_in, ssd_conv_w, ssd_conv_b,
           ssd_dt_bias, ssd_a_log, ssd_d, ssd_norm, ssd_w_out, xa_wq, xa_wk, xa_wv, xa_wo, ffn_w_gate,
           ffn_w_up, ffn_w_down):
    bsz, seq, d = x.shape
    tok = bsz * seq
    depth = norm_mix.shape[0]
    tm = _tile(tok, TOKEN_TILE)
    tt = _tile(seq, SEQ_TILE)

    h = x.reshape(tok, d)
    for layer in range(depth):
        xn = norm_cast(h, norm_mix[layer], tm)
        if layer % 2 == 0:
            e = layer // 2
            n_main = 8 * N_HEADS * HEAD_DIM
            proj = ws_matmul([xn], hy_w_in, e, n_main, tm, _tile(n_main, COLUMN_TILE)).reshape(bsz, seq, n_main)
            tail = ws_matmul([xn], _tail_weight(hy_w_in, e, n_main), 0, LANES, tm, LANES)
            gates_col = tail.reshape(bsz, seq, LANES)
            gates_row = jnp.swapaxes(gates_col[:, :, :2 * N_HEADS], 1, 2)
            y_a = hgrn_heads(proj, hgrn_lb_logits.astype(F32), hgrn_norm[e], e, 0, tt)
            y_b = gdn_heads(proj, gates_col, gates_row, gdn_conv_w[e].astype(F32), gdn_a_log[e].astype(F32),
                            gdn_dt_bias[e].astype(F32), gdn_norm[e], 4, _tile(seq, GDN_SEQ_TILE))
            h = ws_matmul([y_a.reshape(tok, -1), y_b.reshape(tok, -1)], hy_w_out, e, d, tm,
                          _tile(d, COLUMN_TILE), resid=h)
        else:
            o = layer // 2
            n_heads = ssd_dt_bias.shape[1]
            n_main = ssd_w_in.shape[2] - n_heads
            proj = ws_matmul([xn], ssd_w_in, o, n_main, tm, _tile(n_main, COLUMN_TILE)).reshape(bsz, seq, n_main)
            tail = ws_matmul([xn], _tail_weight(ssd_w_in, o, n_main), 0, LANES, tm, LANES)
            dt_col = tail.reshape(bsz, seq, LANES)
            dt_pairs = dt_col[:, :, :n_heads].reshape(bsz, seq // CHUNK, CHUNK, SSD_GROUPS, SSD_PAIRS, 2)
            dt_pairs = dt_pairs.transpose(0, 3, 4, 1, 5, 2).reshape(bsz, SSD_GROUPS, SSD_PAIRS, 2 * seq)
            dt_pairs = jnp.pad(dt_pairs, ((0, 0), (0, 0), (0, SUBLANES - SSD_PAIRS), (0, 0)))
            y = ssd_groups(proj, dt_col, dt_pairs, ssd_conv_w[o].astype(F32), ssd_conv_b[o].astype(F32),
                           ssd_dt_bias[o], ssd_a_log[o], ssd_d[o], tt)
            h = ws_matmul([y.reshape(tok, -1)], ssd_w_out, o, d, _tile(tok, TOKEN_TILE // 2),
                          _tile(d, COLUMN_TILE), gain=ssd_norm[o], resid=h, single_buffer_w=True)
        k_mem, v_mem = mem_kv(mem, norm_mem[layer], xa_wk, xa_wv, layer)
        h = xattn_block(h.reshape(bsz, seq, d), norm_xattn[layer], xa_wq, k_mem, v_mem, xa_wo, layer,
                        _tile(seq, 512)).reshape(tok, d)
        h = ffn_block(h, norm_ffn[layer], ffn_w_gate, ffn_w_up, ffn_w_down, layer, norm_final,
                      tm, _tile(ffn_w_gate.shape[2], FFN_TILE), layer == depth - 1)
    return h.reshape(bsz, seq, d)
```

```python
import functools

import numpy as np
import jax
import jax.numpy as jnp
from jax import lax
from jax.experimental import pallas as pl
from jax.experimental.pallas import tpu as pltpu

F32 = jnp.float32
BF16 = jnp.bfloat16

NORM_EPS = 1e-6
L2_EPS = 1e-6
CHUNK = 64
CONV_K = 4
LANES = 128
SUBLANES = 8
HEAD_DIM = 128
N_HEADS = 8
SSD_HEADDIM = 64
SSD_GROUPS = 8
SSD_HEADS_PER_GROUP = 8
SSD_GROUP_WIDTH = SSD_HEADDIM * SSD_HEADS_PER_GROUP
SSD_DSTATE = 128
XA_HEADS = 4
XA_HEAD_DIM = 128
VMEM_LIMIT_BYTES = 56 * 1024 * 1024
TOKEN_TILE = 1024
COLUMN_TILE = 1024
FFN_TILE = 256
FFN_TOKEN_TILE = 2048
SEQ_TILE = 512
GDN_SEQ_TILE = 256


def _cparams(*sem):
    return pltpu.CompilerParams(dimension_semantics=sem, vmem_limit_bytes=VMEM_LIMIT_BYTES)


def _dot(a, b):
    return jnp.dot(a, b, preferred_element_type=F32)


def _dot_nt(a, b):
    return lax.dot_general(a, b, (((1,), (1,)), ((), ())), preferred_element_type=F32)


def _dot_tn(a, b):
    return lax.dot_general(a, b, (((0,), (0,)), ((), ())), preferred_element_type=F32)


def _split3(x):
    hi = x.astype(BF16)
    r1 = x - hi.astype(F32)
    mid = r1.astype(BF16)
    lo = (r1 - mid.astype(F32)).astype(BF16)
    return hi, mid, lo


def _silu(x):
    return x * jax.nn.sigmoid(x)


def _softplus(x):
    return jnp.maximum(x, 0.0) + jnp.log(1.0 + jnp.exp(-jnp.abs(x)))


def _rms_rows(x, gain):
    ms = jnp.mean(x * x, axis=-1, keepdims=True)
    return x * lax.rsqrt(ms + NORM_EPS) * gain


def _causal_conv(carry_ref, x, w):
    rows = x.shape[0]
    ext = jnp.concatenate([carry_ref[...], x], axis=0)
    y = w[CONV_K - 1:CONV_K, :] * x
    for k in range(CONV_K - 1):
        y = y + w[k:k + 1, :] * pltpu.roll(ext, CONV_K - 1 - k, axis=0)[SUBLANES:]
    carry_ref[...] = x[rows - SUBLANES:]
    return y


def _norm_cast_kernel(x_ref, g_ref, o_ref):
    o_ref[...] = _rms_rows(x_ref[...], g_ref[...]).astype(BF16)


def norm_cast(x, gain, tm):
    m, k = x.shape
    return pl.pallas_call(
        _norm_cast_kernel,
        grid=(m // tm,),
        in_specs=[pl.BlockSpec((tm, k), lambda i: (i, 0)), pl.BlockSpec((1, k), lambda i: (0, 0))],
        out_specs=pl.BlockSpec((tm, k), lambda i: (i, 0)),
        out_shape=jax.ShapeDtypeStruct((m, k), BF16),
        compiler_params=_cparams("parallel"),
        name="norm_cast",
    )(x, gain.reshape(1, k))


def _ws_matmul_kernel(*refs, n_a, normalize, residual, w_transposed):
    a_refs = refs[:n_a]
    rest = list(refs[n_a:])
    g_ref = rest.pop(0) if normalize else None
    w_ref = rest.pop(0)
    r_ref = rest.pop(0) if residual else None
    o_ref, wb_ref = rest

    @pl.when(pl.program_id(1) == 0)
    def _():
        w = w_ref[0]
        wb_ref[...] = (w.T if w_transposed else w).astype(BF16)

    acc = None
    k0 = 0
    for a_ref in a_refs:
        a = a_ref[...]
        if normalize:
            a = _rms_rows(a, g_ref[...]).astype(BF16)
        part = _dot(a, wb_ref[k0:k0 + a.shape[1], :])
        k0 += a.shape[1]
        acc = part if acc is None else acc + part
    o_ref[...] = r_ref[...] + acc if residual else acc


def ws_matmul(a_list, w, layer, n, tm, tn, gain=None, resid=None, single_buffer_w=False, w_transposed=False):
    m = a_list[0].shape[0]
    k = w.shape[2] if w_transposed else w.shape[1]
    normalize = gain is not None
    residual = resid is not None
    w_mode = dict(pipeline_mode=pl.Buffered(1)) if single_buffer_w else {}
    in_specs = [pl.BlockSpec((tm, a.shape[1]), lambda j, i: (i, 0)) for a in a_list]
    args = list(a_list)
    if normalize:
        in_specs.append(pl.BlockSpec((1, k), lambda j, i: (0, 0)))
        args.append(gain.reshape(1, k))
    if w_transposed:
        in_specs.append(pl.BlockSpec((1, tn, k), lambda j, i: (layer, j, 0), **w_mode))
    else:
        in_specs.append(pl.BlockSpec((1, k, tn), lambda j, i: (layer, 0, j), **w_mode))
    args.append(w)
    if residual:
        in_specs.append(pl.BlockSpec((tm, tn), lambda j, i: (i, j)))
        args.append(resid)
    return pl.pallas_call(
        functools.partial(_ws_matmul_kernel, n_a=len(a_list), normalize=normalize, residual=residual,
                          w_transposed=w_transposed),
        grid=(n // tn, m // tm),
        in_specs=in_specs,
        out_specs=pl.BlockSpec((tm, tn), lambda j, i: (i, j)),
        out_shape=jax.ShapeDtypeStruct((m, n), F32),
        scratch_shapes=[pltpu.VMEM((k, tn), BF16)],
        compiler_params=_cparams("arbitrary", "arbitrary"),
        name="ws_matmul",
    )(*args)


def _ffn_kernel(x_hbm, g_ref, wg_ref, wu_ref, wd_ref, gf_ref, o_ref, xn_ref, sem, *, final_norm):
    f = pl.program_id(1)

    @pl.when(f == 0)
    def _():
        tm = o_ref.shape[0]
        rows = pl.ds(pl.multiple_of(pl.program_id(0) * tm, tm), tm)
        copy = pltpu.make_async_copy(x_hbm.at[rows, :], o_ref, sem)
        copy.start()
        copy.wait()
        xn_ref[...] = _rms_rows(o_ref[...], g_ref[...]).astype(BF16)

    xn = xn_ref[...]
    act = (_silu(_dot(xn, wg_ref[0].astype(BF16))) * _dot(xn, wu_ref[0].astype(BF16))).astype(BF16)
    o_ref[...] += _dot(act, wd_ref[0].astype(BF16))

    if final_norm:
        @pl.when(f == pl.num_programs(1) - 1)
        def _():
            o_ref[...] = _rms_rows(o_ref[...], gf_ref[...])


def ffn_block(x, gain, wg, wu, wd, layer, gain_final, tm, tf, final_norm):
    m, d = x.shape
    dff = wg.shape[2]
    return pl.pallas_call(
        functools.partial(_ffn_kernel, final_norm=final_norm),
        grid=(m // tm, dff // tf),
        in_specs=[
            pl.BlockSpec(memory_space=pl.ANY),
            pl.BlockSpec((1, d), lambda i, f: (0, 0)),
            pl.BlockSpec((1, d, tf), lambda i, f: (layer, 0, f)),
            pl.BlockSpec((1, d, tf), lambda i, f: (layer, 0, f)),
            pl.BlockSpec((1, tf, d), lambda i, f: (layer, f, 0)),
            pl.BlockSpec((1, d), lambda i, f: (0, 0)),
        ],
        out_specs=pl.BlockSpec((tm, d), lambda i, f: (i, 0), pipeline_mode=pl.Buffered(1)),
        out_shape=jax.ShapeDtypeStruct((m, d), F32),
        scratch_shapes=[pltpu.VMEM((tm, d), BF16), pltpu.SemaphoreType.DMA],
        compiler_params=_cparams("parallel", "arbitrary"),
        name="ffn_block",
    )(x, gain.reshape(1, d), wg, wu, wd, gain_final.reshape(1, d))


def _mem_kv_kernel(m_ref, g_ref, wk_ref, wv_ref, k_ref, v_ref):
    mn = _rms_rows(m_ref[0], g_ref[...]).astype(BF16)
    k_ref[0] = _dot(mn, wk_ref[0].astype(BF16)).astype(BF16)
    v_ref[0] = _dot(mn, wv_ref[0].astype(BF16)).astype(BF16)


def mem_kv(mem, gain, wk, wv, layer):
    b, ml, d = mem.shape
    xa = wk.shape[2]
    return pl.pallas_call(
        _mem_kv_kernel,
        grid=(b,),
        in_specs=[
            pl.BlockSpec((1, ml, d), lambda i: (i, 0, 0)),
            pl.BlockSpec((1, d), lambda i: (0, 0)),
            pl.BlockSpec((1, d, xa), lambda i: (layer, 0, 0)),
            pl.BlockSpec((1, d, xa), lambda i: (layer, 0, 0)),
        ],
        out_specs=[pl.BlockSpec((1, ml, xa), lambda i: (i, 0, 0))] * 2,
        out_shape=[jax.ShapeDtypeStruct((b, ml, xa), BF16)] * 2,
        compiler_params=_cparams("parallel"),
        name="mem_kv",
    )(mem, gain.reshape(1, d), wk, wv)


def _xattn_kernel(x_ref, g_ref, wq_ref, k_ref, v_ref, wo_ref, o_ref, wqb_ref, wob_ref):
    @pl.when((pl.program_id(0) == 0) & (pl.program_id(1) == 0))
    def _():
        wqb_ref[...] = wq_ref[0].astype(BF16)
        wob_ref[...] = wo_ref[0].astype(BF16)

    x = x_ref[0]
    xn = _rms_rows(x, g_ref[...]).astype(BF16)
    q = (_dot(xn, wqb_ref[...]) * (XA_HEAD_DIM ** -0.5)).astype(BF16)
    outs = []
    for h in range(XA_HEADS):
        sl = slice(h * XA_HEAD_DIM, (h + 1) * XA_HEAD_DIM)
        s = _dot_nt(q[:, sl], k_ref[0, :, sl])
        p = jnp.exp(s - jnp.max(s, axis=-1, keepdims=True))
        den = jnp.sum(p, axis=-1, keepdims=True)
        outs.append(_dot(p.astype(BF16), v_ref[0, :, sl]) / den)
    o = jnp.concatenate(outs, axis=-1).astype(BF16)
    o_ref[0] = x + _dot(o, wob_ref[...])


def xattn_block(x, gain, wq, k, v, wo, layer, tm):
    b, s, d = x.shape
    ml, xa = k.shape[1], k.shape[2]
    return pl.pallas_call(
        _xattn_kernel,
        grid=(b, s // tm),
        in_specs=[
            pl.BlockSpec((1, tm, d), lambda i, t: (i, t, 0)),
            pl.BlockSpec((1, d), lambda i, t: (0, 0)),
            pl.BlockSpec((1, d, xa), lambda i, t: (layer, 0, 0)),
            pl.BlockSpec((1, ml, xa), lambda i, t: (i, 0, 0)),
            pl.BlockSpec((1, ml, xa), lambda i, t: (i, 0, 0)),
            pl.BlockSpec((1, xa, d), lambda i, t: (layer, 0, 0)),
        ],
        out_specs=pl.BlockSpec((1, tm, d), lambda i, t: (i, t, 0)),
        out_shape=jax.ShapeDtypeStruct((b, s, d), F32),
        scratch_shapes=[pltpu.VMEM((d, xa), BF16), pltpu.VMEM((xa, d), BF16)],
        compiler_params=_cparams("arbitrary", "arbitrary"),
        name="xattn_block",
    )(x, gain.reshape(1, d), wq, k, v, wo)


_HGRN_LEVELS = (32, 16, 8, 4, 2, 1)


def _hgrn_masks():
    c = CHUNK
    i = np.arange(c)[:, None]
    j = np.arange(c)[None, :]
    masks = [(i // (2 * s) == j // (2 * s)) & (i % (2 * s) >= s) & (j % (2 * s) < s) for s in _HGRN_LEVELS]
    masks.append(i == j)
    return jnp.asarray(np.stack(masks).astype(np.float32))


def _segment_row(x, seg, idx):
    rows, lanes = x.shape
    x3 = x.reshape(rows // seg, seg, lanes)
    return jnp.broadcast_to(x3[:, idx:idx + 1, :], x3.shape).reshape(rows, lanes)


def _hgrn_kernel(q_ref, f_ref, i_ref, gate_ref, lbl_ref, nw_ref, masks_ref, y_ref, st_ref, *, layer, n_chunks):
    c = CHUNK
    tt = n_chunks * c

    @pl.when(pl.program_id(2) == 0)
    def _():
        st_ref[...] = jnp.zeros_like(st_ref)

    logits = lbl_ref[...]
    ex = jnp.exp(logits - jnp.max(logits, axis=0, keepdims=True))
    sm = ex / jnp.sum(ex, axis=0, keepdims=True)
    lb = jnp.sum(sm[0:layer + 1, :], axis=0, keepdims=True)

    a_f = f_ref[0]
    q = _silu(q_ref[0])
    log_f = jnp.log(lb + (1.0 - lb) * jax.nn.sigmoid(a_f))
    k = (1.0 - lb) * jax.nn.sigmoid(-a_f)
    vb = i_ref[0].astype(BF16)

    row = lax.broadcasted_iota(jnp.int32, (tt, HEAD_DIM), 0)
    rc = row & (c - 1)
    g = log_f
    for s in (1, 2, 4, 8, 16, 32):
        g = g + jnp.where(rc >= s, pltpu.roll(g, s, axis=0), 0.0)
    g_last = _segment_row(g, c, c - 1)

    exps = []
    for s in _HGRN_LEVELS:
        if 2 * s >= SUBLANES:
            exps.append(-jnp.abs(g - _segment_row(g, 2 * s, s - 1)))
        elif s == 2:
            r4 = row & 3
            below = pltpu.roll(log_f, 1, axis=0)
            above = pltpu.roll(log_f, tt - 1, axis=0)
            exps.append(jnp.where(r4 == 0, above, jnp.where(r4 == 1, 0.0,
                                                            jnp.where(r4 == 2, log_f, log_f + below))))
        else:
            exps.append(jnp.where((row & 1) == 1, log_f, 0.0))

    def chunks(x):
        xb = x.astype(BF16)
        return [xb[ci * c:(ci + 1) * c] for ci in range(n_chunks)]

    qs, ks = chunks(q), chunks(k)
    n_lv = len(_HGRN_LEVELS)
    attn = [masks_ref[n_lv] * _dot_nt(qs[ci], ks[ci]) for ci in range(n_chunks)]
    for lv in range(n_lv):
        sc = jnp.exp(exps[lv])
        ql, kl = chunks(q * sc), chunks(k * sc)
        m = masks_ref[lv]
        attn = [attn[ci] + m * _dot_nt(ql[ci], kl[ci]) for ci in range(n_chunks)]

    vs = chunks(vb)
    qg = chunks(q * jnp.exp(g))
    kd = chunks(k * jnp.exp(g_last - g))
    upd = [_dot_tn(vs[ci], kd[ci]) for ci in range(n_chunks)]
    intra = [_dot(attn[ci].astype(BF16), vs[ci]) for ci in range(n_chunks)]
    dec = jnp.exp(g_last)
    st = st_ref[...]
    outs = []
    for ci in range(n_chunks):
        outs.append(intra[ci] + _dot_nt(qg[ci], st.astype(BF16)))
        st = st * dec[ci * c:ci * c + 1, :] + upd[ci]
    st_ref[...] = st
    o = jnp.concatenate(outs, axis=0)
    y = o * lax.rsqrt(jnp.mean(o * o, axis=-1, keepdims=True) + NORM_EPS)
    y_ref[0] = (y * nw_ref[...] * _silu(gate_ref[0])).astype(y_ref.dtype)


def hgrn_heads(proj, lb_logits, norm_w, layer, col0, tt):
    b, s, _ = proj.shape
    nl = lb_logits.shape[0]
    masks = _hgrn_masks()
    blk = lambda o: pl.BlockSpec((1, tt, HEAD_DIM), lambda i, h, t: (i, t, col0 + o + h))
    return pl.pallas_call(
        functools.partial(_hgrn_kernel, layer=layer, n_chunks=tt // CHUNK),
        grid=(b, N_HEADS, s // tt),
        in_specs=[
            blk(0), blk(N_HEADS), blk(2 * N_HEADS), blk(3 * N_HEADS),
            pl.BlockSpec((nl, HEAD_DIM), lambda i, h, t: (0, h)),
            pl.BlockSpec((1, HEAD_DIM), lambda i, h, t: (0, h)),
            pl.BlockSpec(masks.shape, lambda i, h, t: (0, 0, 0)),
        ],
        out_specs=pl.BlockSpec((1, tt, HEAD_DIM), lambda i, h, t: (i, t, h)),
        out_shape=jax.ShapeDtypeStruct((b, s, N_HEADS * HEAD_DIM), BF16),
        scratch_shapes=[pltpu.VMEM((HEAD_DIM, HEAD_DIM), F32)],
        compiler_params=_cparams("parallel", "parallel", "arbitrary"),
        name="hgrn_heads",
    )(proj, proj, proj, proj, lb_logits, norm_w.reshape(1, -1), masks)


def _gdn_kernel(q_ref, k_ref, v_ref, gate_ref, gc_ref, gr_ref, cw_ref, alane_ref, blane_ref, acol_ref, bcol_ref,
                nw_ref, y_ref, s_ref, xq_ref, xk_ref, xv_ref, *, tt):
    c = CHUNK
    hd = HEAD_DIM
    pair = 2 * c
    n_pairs = tt // pair
    n_chunks = tt // c
    width = N_HEADS * hd

    @pl.when(pl.program_id(1) == 0)
    def _():
        s_ref[...] = jnp.zeros_like(s_ref)
        for r in (xq_ref, xk_ref, xv_ref):
            r[...] = jnp.zeros_like(r)

    q_all = _silu(_causal_conv(xq_ref, q_ref[0], cw_ref[:, 0:width]))
    k_all = _silu(_causal_conv(xk_ref, k_ref[0], cw_ref[:, width:2 * width]))
    v_all = _silu(_causal_conv(xv_ref, v_ref[0], cw_ref[:, 2 * width:3 * width]))

    gates = gc_ref[0]
    beta_all = jax.nn.sigmoid(gates)
    gam_all = -jnp.exp(alane_ref[...]) * _softplus(gates + blane_ref[...])
    row = lax.broadcasted_iota(jnp.int32, (tt, LANES), 0)
    rc = row & (c - 1)
    for s in (1, 2, 4, 8, 16, 32):
        gam_all = gam_all + jnp.where(rc >= s, pltpu.roll(gam_all, s, axis=0), 0.0)
    glast_all = _segment_row(gam_all, c, c - 1)
    gam_rows = -jnp.exp(acol_ref[...]) * _softplus(gr_ref[0] + bcol_ref[...])
    lc = lax.broadcasted_iota(jnp.int32, gam_rows.shape, 1) & (c - 1)
    for s in (1, 2, 4, 8, 16, 32):
        gam_rows = gam_rows + jnp.where(lc >= s, pltpu.roll(gam_rows, s, axis=1), 0.0)

    ii = lax.broadcasted_iota(jnp.int32, (c, LANES), 0)
    ll = lax.broadcasted_iota(jnp.int32, (c, LANES), 1)
    jj = ll & (c - 1)
    left = ll < c
    lower = ii >= jj
    strict = ii > jj
    eye2 = jnp.where(ii == jj, 1.0, 0.0)

    def block_diag(m):
        zero = jnp.zeros_like(m)
        return jnp.concatenate([jnp.where(left, m, zero), jnp.where(left, zero, m)], axis=0)

    def split2(x):
        hi = x.astype(BF16)
        return hi, (x - hi.astype(F32)).astype(BF16)

    def times_block_diag(lhs, p_hi, p_lo):
        l_hi, l_lo = split2(lhs)
        return _dot(jnp.concatenate([l_hi, l_lo, l_hi], axis=1), jnp.concatenate([p_hi, p_hi, p_lo], axis=0))

    units = [(h, p) for h in range(N_HEADS) for p in range(n_pairs)]
    qd, kd, rhs, khb, kbb, qhb, gam_h = [], [], [], [], [], [], []
    for h in range(N_HEADS):
        lanes = slice(h * hd, (h + 1) * hd)
        qh, kh, vh = q_all[:, lanes], k_all[:, lanes], v_all[:, lanes]
        qh = qh * lax.rsqrt(jnp.sum(qh * qh, axis=-1, keepdims=True) + L2_EPS) * (hd ** -0.5)
        kh = kh * lax.rsqrt(jnp.sum(kh * kh, axis=-1, keepdims=True) + L2_EPS)
        beta = beta_all[:, h:h + 1]
        gam = gam_all[:, N_HEADS + h:N_HEADS + h + 1]
        eg = jnp.exp(gam)
        kb = kh * beta
        gam_h.append(gam)
        rhs.append(jnp.concatenate([kb * eg, vh * beta], axis=1).astype(BF16))
        qd.append((qh * eg).astype(BF16))
        kd.append((kh * jnp.exp(glast_all[:, N_HEADS + h:N_HEADS + h + 1] - gam)).astype(BF16))
        khb.append(kh.astype(BF16))
        kbb.append(kb.astype(BF16))
        qhb.append(qh.astype(BF16))

    x_mats, qk_tiles = [], [[] for _ in range(N_HEADS)]
    for h, p in units:
        r0 = p * pair
        rows = slice(r0, r0 + pair)
        res = _dot_nt(jnp.concatenate([kbb[h][rows], qhb[h][rows]], axis=0), khb[h][rows])
        col = jnp.where(left, jnp.broadcast_to(gam_h[h][r0:r0 + c], (c, LANES)),
                        jnp.broadcast_to(gam_h[h][r0 + c:r0 + pair], (c, LANES)))
        decay = jnp.exp(jnp.minimum(col - gam_rows[N_HEADS + h:N_HEADS + h + 1, rows], 0.0))
        x_mats.append(jnp.where(strict, jnp.where(left, res[0:c], res[c:pair]) * -decay, 0.0))
        qk_tiles[h].append(jnp.where(lower & left, res[pair:pair + c] * decay, 0.0).astype(BF16))
        qk_tiles[h].append(jnp.where(lower & (~left), res[pair + c:2 * pair] * decay, 0.0).astype(BF16))

    t_mats = [eye2 + x for x in x_mats]
    powers = []
    for x in x_mats:
        p_hi, p_lo = split2(x)
        powers.append(times_block_diag(x, block_diag(p_hi), block_diag(p_lo)))
    for it in range(5):
        for u in range(len(units)):
            p_hi, p_lo = split2(powers[u])
            bd_hi, bd_lo = block_diag(p_hi), block_diag(p_lo)
            if it < 4:
                both = times_block_diag(jnp.concatenate([powers[u], t_mats[u]], axis=0), bd_hi, bd_lo)
                powers[u], t_mats[u] = both[0:c], t_mats[u] + both[c:pair]
            else:
                t_mats[u] = t_mats[u] + times_block_diag(t_mats[u], bd_hi, bd_lo)
    wu = [[] for _ in range(N_HEADS)]
    for u, (h, p) in enumerate(units):
        wu[h].append(_dot(block_diag(t_mats[u].astype(BF16)), rhs[h][p * pair:(p + 1) * pair]))

    outs = [[] for _ in range(N_HEADS)]
    zeros_half = jnp.zeros((c, hd), BF16)
    heads = range(N_HEADS)
    states = [s_ref[h] for h in heads]
    chunk_decay = jnp.exp(glast_all)
    for ci in range(n_chunks):
        rows = slice(ci * c, (ci + 1) * c)
        half = slice((ci % 2) * c, (ci % 2 + 1) * c)
        stb = [states[h].astype(BF16) for h in heads]
        wu_c = [wu[h][ci // 2][half] for h in heads]
        w_s = [_dot(wu_c[h][:, :hd].astype(BF16), stb[h]) for h in heads]
        vnb = [(wu_c[h][:, hd:] - w_s[h]).astype(BF16) for h in heads]
        for h in heads:
            v2 = jnp.concatenate([vnb[h], zeros_half] if ci % 2 == 0 else [zeros_half, vnb[h]], axis=0)
            outs[h].append(_dot(qd[h][rows], stb[h]) + _dot(qk_tiles[h][ci], v2))
        states = [states[h] * chunk_decay[ci * c:ci * c + 1, N_HEADS + h:N_HEADS + h + 1]
                  + _dot_tn(kd[h][rows], vnb[h]) for h in heads]
    for h in heads:
        s_ref[h] = states[h]

    for h in range(N_HEADS):
        lanes = slice(h * hd, (h + 1) * hd)
        o = jnp.concatenate(outs[h], axis=0)
        y = o * lax.rsqrt(jnp.mean(o * o, axis=-1, keepdims=True) + NORM_EPS)
        y_ref[0, :, lanes] = (y * nw_ref[:, lanes] * _silu(gate_ref[0, :, lanes])).astype(y_ref.dtype)


def gdn_heads(proj, gates_col, gates_row, conv_w, a_log, dt_bias, norm_w, col0, tt):
    b, s, _ = proj.shape
    width = N_HEADS * HEAD_DIM
    blk = lambda o: pl.BlockSpec((1, tt, width), lambda i, t: (i, t, col0 + o))
    whole = lambda shape: pl.BlockSpec(shape, lambda i, t: (0,) * len(shape))
    lane_vec = jnp.zeros((1, LANES), F32).at[0, N_HEADS:2 * N_HEADS]
    col_vec = jnp.zeros((2 * N_HEADS, 1), F32).at[N_HEADS:, 0]
    return pl.pallas_call(
        functools.partial(_gdn_kernel, tt=tt),
        grid=(b, s // tt),
        in_specs=[
            blk(0), blk(1), blk(2), blk(3),
            pl.BlockSpec((1, tt, LANES), lambda i, t: (i, t, 0)),
            pl.BlockSpec((1, 2 * N_HEADS, tt), lambda i, t: (i, 0, t)),
            whole(conv_w.shape),
            whole((1, LANES)), whole((1, LANES)), whole((2 * N_HEADS, 1)), whole((2 * N_HEADS, 1)),
            whole((1, width)),
        ],
        out_specs=pl.BlockSpec((1, tt, width), lambda i, t: (i, t, 0)),
        out_shape=jax.ShapeDtypeStruct((b, s, width), BF16),
        scratch_shapes=[
            pltpu.VMEM((N_HEADS, HEAD_DIM, HEAD_DIM), F32),
            pltpu.VMEM((SUBLANES, width), F32),
            pltpu.VMEM((SUBLANES, width), F32),
            pltpu.VMEM((SUBLANES, width), F32),
        ],
        compiler_params=_cparams("parallel", "arbitrary"),
        name="gdn_heads",
    )(proj, proj, proj, proj, gates_col, gates_row, conv_w,
      lane_vec.set(a_log), lane_vec.set(dt_bias), col_vec.set(a_log), col_vec.set(dt_bias),
      norm_w.reshape(1, -1))


SSD_PAIRS = SSD_HEADS_PER_GROUP // 2


def _ssd_expand_table():
    expand = np.zeros((SSD_GROUPS, LANES, SSD_GROUP_WIDTH), np.float32)
    for g in range(SSD_GROUPS):
        for hh in range(SSD_HEADS_PER_GROUP):
            expand[g, g * SSD_HEADS_PER_GROUP + hh, hh * SSD_HEADDIM:(hh + 1) * SSD_HEADDIM] = 1.0
    return jnp.asarray(np.concatenate([expand] * 3, axis=1), BF16)


def _ssd_kernel(z_ref, x_ref, b_ref, c_ref, dtc_ref, dtp_ref, cwx_ref, cwb_ref, cwc_ref,
                cbx_ref, cbb_ref, cbc_ref, blane_ref, alane_ref, bpair_ref, apair_ref, de_ref, exp_ref,
                y_ref, ht_ref, xx_ref, xb_ref, xc_ref, *, tt):
    c = CHUNK
    n_chunks = tt // c
    hp = SSD_HEADDIM

    @pl.when(pl.program_id(2) == 0)
    def _():
        for r in (ht_ref, xx_ref, xb_ref, xc_ref):
            r[...] = jnp.zeros_like(r)

    xs = _silu(_causal_conv(xx_ref, x_ref[0], cwx_ref[...]) + cbx_ref[...])
    bm = _silu(_causal_conv(xb_ref, b_ref[0], cwb_ref[...]) + cbb_ref[...]).astype(BF16)
    cm = _silu(_causal_conv(xc_ref, c_ref[0], cwc_ref[...]) + cbc_ref[...]).astype(BF16)

    dt_c = _softplus(dtc_ref[0] + blane_ref[...])
    acum_c = dt_c * -jnp.exp(alane_ref[...])
    rc = lax.broadcasted_iota(jnp.int32, (tt, LANES), 0) & (c - 1)
    for s in (1, 2, 4, 8, 16, 32):
        acum_c = acum_c + jnp.where(rc >= s, pltpu.roll(acum_c, s, axis=0), 0.0)
    s_c = dt_c * jnp.exp(_segment_row(acum_c, c, c - 1) - acum_c)
    a_hi, a_mid, a_lo = _split3(acum_c)
    acum_e = _dot(jnp.concatenate([a_hi, a_mid, a_lo], axis=1), exp_ref[0])
    s_hi, s_mid, _ = _split3(s_c)
    s_e = _dot(jnp.concatenate([s_hi, s_mid], axis=1), exp_ref[0, 0:2 * LANES, :])
    decay_e = jnp.exp(acum_e)
    xw = (xs * s_e).astype(BF16)
    xsb = xs.astype(BF16)

    dt_p = _softplus(dtp_ref[0, 0] + jnp.concatenate([bpair_ref[0]] * n_chunks, axis=1))
    acum_p = dt_p * -jnp.exp(jnp.concatenate([apair_ref[0]] * n_chunks, axis=1))
    lc = lax.broadcasted_iota(jnp.int32, acum_p.shape, 1) & (c - 1)
    for s in (1, 2, 4, 8, 16, 32):
        acum_p = acum_p + jnp.where(lc >= s, pltpu.roll(acum_p, s, axis=1), 0.0)

    ii = lax.broadcasted_iota(jnp.int32, (c, LANES), 0)
    ll = lax.broadcasted_iota(jnp.int32, (c, LANES), 1)
    lower2 = ii >= (ll & (c - 1))
    left = ll < hp

    chunk_rows = [slice(ci * c, (ci + 1) * c) for ci in range(n_chunks)]
    cb2 = [_dot_nt(cm[r], jnp.concatenate([bm[r], bm[r]], axis=0)) for r in chunk_rows]
    upd = [_dot_tn(bm[r], xw[r]) for r in chunk_rows]
    y_diag = []
    for ci, r in enumerate(chunk_rows):
        tiles = []
        for pr in range(SSD_PAIRS):
            lanes = slice(2 * pr * hp, (2 * pr + 2) * hp)
            tok = slice(ci * LANES, (ci + 1) * LANES)
            lmat = jnp.where(lower2, jnp.exp(jnp.minimum(acum_e[r, lanes] - acum_p[pr:pr + 1, tok], 0.0)), 0.0)
            xp = xsb[r, lanes]
            zero = jnp.zeros_like(xp)
            rhs = jnp.concatenate([jnp.where(left, xp, zero), jnp.where(left, zero, xp)], axis=0)
            tiles.append(_dot((cb2[ci] * lmat * dt_p[pr:pr + 1, tok]).astype(BF16), rhs))
        y_diag.append(jnp.concatenate(tiles, axis=1))
    ht = ht_ref[...]
    y_off = []
    for ci, r in enumerate(chunk_rows):
        y_off.append(_dot(cm[r], ht.astype(BF16)))
        ht = ht * decay_e[ci * c + c - 1:ci * c + c, :] + upd[ci]
    ht_ref[...] = ht
    y = jnp.concatenate(y_diag, axis=0) + jnp.concatenate(y_off, axis=0) * decay_e + de_ref[...] * xs
    y_ref[0] = y * _silu(z_ref[0])


def ssd_groups(proj, dt_col, dt_pairs, conv_w, conv_b, dt_bias, a_log, d_skip, tt):
    b, s, _ = proj.shape
    g_w = SSD_GROUP_WIDTH
    n_xblk = SSD_GROUPS
    expand = _ssd_expand_table()
    rep = lambda p: jnp.repeat(p.astype(F32), SSD_HEADDIM).reshape(1, -1)
    row2 = lambda v: v.reshape(1, -1)
    lane_vec = lambda p: jnp.pad(p.astype(F32), (0, LANES - p.shape[0])).reshape(1, LANES)

    def pair_tile(p):
        t = jnp.repeat(p.astype(F32).reshape(SSD_GROUPS, SSD_PAIRS, 2, 1), CHUNK, axis=-1)
        t = t.reshape(SSD_GROUPS, SSD_PAIRS, LANES)
        return jnp.pad(t, ((0, 0), (0, SUBLANES - SSD_PAIRS), (0, 0)))
    wide = lambda o: pl.BlockSpec((1, tt, g_w), lambda i, g, t: (i, t, o + g))
    narrow = lambda o: pl.BlockSpec((1, tt, LANES), lambda i, g, t: (i, t, o + g))
    x_off = n_xblk
    b_off = 2 * n_xblk * (g_w // LANES)
    c_off = b_off + SSD_GROUPS
    return pl.pallas_call(
        functools.partial(_ssd_kernel, tt=tt),
        grid=(b, SSD_GROUPS, s // tt),
        in_specs=[
            wide(0), wide(x_off), narrow(b_off), narrow(c_off),
            pl.BlockSpec((1, tt, LANES), lambda i, g, t: (i, t, 0)),
            pl.BlockSpec((1, 1, SUBLANES, 2 * tt), lambda i, g, t: (i, g, 0, t)),
            pl.BlockSpec((CONV_K, g_w), lambda i, g, t: (0, g)),
            pl.BlockSpec((CONV_K, LANES), lambda i, g, t: (0, n_xblk * (g_w // LANES) + g)),
            pl.BlockSpec((CONV_K, LANES), lambda i, g, t: (0, n_xblk * (g_w // LANES) + SSD_GROUPS + g)),
            pl.BlockSpec((1, g_w), lambda i, g, t: (0, g)),
            pl.BlockSpec((1, LANES), lambda i, g, t: (0, n_xblk * (g_w // LANES) + g)),
            pl.BlockSpec((1, LANES), lambda i, g, t: (0, n_xblk * (g_w // LANES) + SSD_GROUPS + g)),
            pl.BlockSpec((1, LANES), lambda i, g, t: (0, 0)),
            pl.BlockSpec((1, LANES), lambda i, g, t: (0, 0)),
            pl.BlockSpec((1, SUBLANES, LANES), lambda i, g, t: (g, 0, 0)),
            pl.BlockSpec((1, SUBLANES, LANES), lambda i, g, t: (g, 0, 0)),
            pl.BlockSpec((1, g_w), lambda i, g, t: (0, g)),
            pl.BlockSpec((1, 3 * LANES, g_w), lambda i, g, t: (g, 0, 0)),
        ],
        out_specs=pl.BlockSpec((1, tt, g_w), lambda i, g, t: (i, t, g)),
        out_shape=jax.ShapeDtypeStruct((b, s, SSD_GROUPS * g_w), F32),
        scratch_shapes=[
            pltpu.VMEM((SSD_DSTATE, g_w), F32),
            pltpu.VMEM((SUBLANES, g_w), F32),
            pltpu.VMEM((SUBLANES, LANES), F32),
            pltpu.VMEM((SUBLANES, LANES), F32),
        ],
        compiler_params=_cparams("parallel", "parallel", "arbitrary"),
        name="ssd_groups",
    )(proj, proj, proj, proj, dt_col, dt_pairs, conv_w, conv_w, conv_w,
      row2(conv_b), row2(conv_b), row2(conv_b), lane_vec(dt_bias), lane_vec(a_log),
      pair_tile(dt_bias), pair_tile(a_log), rep(d_skip), expand)


def _tail_weight(w_t, layer, n_main):
    tail = w_t[layer, n_main:, :]
    return jnp.pad(tail, ((0, LANES - tail.shape[0]), (0, 0)))[None]


def _tile(n, pref):
    t = min(n, pref)
    while n % t:
        t //= 2
    return t


def kernel(x, mem, norm_mix, norm_xattn, norm_mem, norm_ffn, norm_final, hy_w_in, hgrn_lb_logits, hgrn_norm,
           gdn_conv_w, gdn_a_log, gdn_dt_bias, gdn_norm, hy_w_out, ssd_w_in, ssd_conv_w, ssd_conv_b,
           ssd_dt_bias, ssd_a_log, ssd_d, ssd_norm, ssd_w_out, xa_wq, xa_wk, xa_wv, xa_wo, ffn_w_gate,
           ffn_w_up, ffn_w_down):
    bsz, seq, d = x.shape
    tok = bsz * seq
    depth = norm_mix.shape[0]
    tm = _tile(tok, TOKEN_TILE)
    tt = _tile(seq, SEQ_TILE)
    hy_w_in_t = jnp.swapaxes(hy_w_in, 1, 2)
    ssd_w_in_t = jnp.swapaxes(ssd_w_in, 1, 2)

    h = x.reshape(tok, d)
    for layer in range(depth):
        xn = norm_cast(h, norm_mix[layer], tm)
        if layer % 2 == 0:
            e = layer // 2
            n_main = 8 * N_HEADS * HEAD_DIM
            proj = ws_matmul([xn], hy_w_in_t, e, n_main, tm, _tile(n_main, COLUMN_TILE),
                             w_transposed=True).reshape(bsz, seq, n_main)
            tail = ws_matmul([xn], _tail_weight(hy_w_in_t, e, n_main), 0, LANES, tm, LANES, w_transposed=True)
            gates_col = tail.reshape(bsz, seq, LANES)
            gates_row = jnp.swapaxes(gates_col[:, :, :2 * N_HEADS], 1, 2)
            y_a = hgrn_heads(proj, hgrn_lb_logits.astype(F32), hgrn_norm[e], e, 0, tt)
            y_b = gdn_heads(proj, gates_col, gates_row, gdn_conv_w[e].astype(F32), gdn_a_log[e].astype(F32),
                            gdn_dt_bias[e].astype(F32), gdn_norm[e], 4, _tile(seq, GDN_SEQ_TILE))
            h = ws_matmul([y_a.reshape(tok, -1), y_b.reshape(tok, -1)], hy_w_out, e, d, tm,
                          _tile(d, COLUMN_TILE), resid=h)
        else:
            o = layer // 2
            n_heads = ssd_dt_bias.shape[1]
            n_main = ssd_w_in.shape[2] - n_heads
            proj = ws_matmul([xn], ssd_w_in_t, o, n_main, tm, _tile(n_main, COLUMN_TILE),
                             w_transposed=True).reshape(bsz, seq, n_main)
            tail = ws_matmul([xn], _tail_weight(ssd_w_in_t, o, n_main), 0, LANES, tm, LANES, w_transposed=True)
            dt_col = tail.reshape(bsz, seq, LANES)
            dt_pairs = dt_col[:, :, :n_heads].reshape(bsz, seq // CHUNK, CHUNK, SSD_GROUPS, SSD_PAIRS, 2)
            dt_pairs = dt_pairs.transpose(0, 3, 4, 1, 5, 2).reshape(bsz, SSD_GROUPS, SSD_PAIRS, 2 * seq)
            dt_pairs = jnp.pad(dt_pairs, ((0, 0), (0, 0), (0, SUBLANES - SSD_PAIRS), (0, 0)))
            y = ssd_groups(proj, dt_col, dt_pairs, ssd_conv_w[o].astype(F32), ssd_conv_b[o].astype(F32),
                           ssd_dt_bias[o], ssd_a_log[o], ssd_d[o], tt)
            h = ws_matmul([y.reshape(tok, -1)], ssd_w_out, o, d, _tile(tok, TOKEN_TILE // 2),
                          _tile(d, COLUMN_TILE), gain=ssd_norm[o], resid=h, single_buffer_w=True)
        k_mem, v_mem = mem_kv(mem, norm_mem[layer], xa_wk, xa_wv, layer)
        h = xattn_block(h.reshape(bsz, seq, d), norm_xattn[layer], xa_wq, k_mem, v_mem, xa_wo, layer,
                        _tile(seq, 512)).reshape(tok, d)
        h = ffn_block(h, norm_ffn[layer], ffn_w_gate, ffn_w_up, ffn_w_down, layer, norm_final,
                      _tile(tok, FFN_TOKEN_TILE), _tile(ffn_w_gate.shape[2], FFN_TILE), layer == depth - 1)
    return h.reshape(bsz, seq, d)
```

```python
import functools

import numpy as np
import jax
import jax.numpy as jnp
from jax import lax
from jax.experimental import pallas as pl
from jax.experimental.pallas import tpu as pltpu

F32 = jnp.float32
BF16 = jnp.bfloat16

NORM_EPS = 1e-6
L2_EPS = 1e-6
CHUNK = 64
CONV_K = 4
LANES = 128
SUBLANES = 8
HEAD_DIM = 128
N_HEADS = 8
SSD_HEADDIM = 64
SSD_GROUPS = 8
SSD_HEADS_PER_GROUP = 8
SSD_GROUP_WIDTH = SSD_HEADDIM * SSD_HEADS_PER_GROUP
SSD_DSTATE = 128
XA_HEADS = 4
XA_HEAD_DIM = 128
VMEM_LIMIT_BYTES = 56 * 1024 * 1024
TOKEN_TILE = 1024
COLUMN_TILE = 1024
FFN_TILE = 256
FFN_TOKEN_TILE = 2048
SEQ_TILE = 512
GDN_SEQ_TILE = 256


def _cparams(*sem):
    return pltpu.CompilerParams(dimension_semantics=sem, vmem_limit_bytes=VMEM_LIMIT_BYTES)


def _dot(a, b):
    return jnp.dot(a, b, preferred_element_type=F32)


def _dot_nt(a, b):
    return lax.dot_general(a, b, (((1,), (1,)), ((), ())), preferred_element_type=F32)


def _dot_tn(a, b):
    return lax.dot_general(a, b, (((0,), (0,)), ((), ())), preferred_element_type=F32)


def _split3(x):
    hi = x.astype(BF16)
    r1 = x - hi.astype(F32)
    mid = r1.astype(BF16)
    lo = (r1 - mid.astype(F32)).astype(BF16)
    return hi, mid, lo


def _silu(x):
    return x * jax.nn.sigmoid(x)


def _softplus(x):
    return jnp.maximum(x, 0.0) + jnp.log(1.0 + jnp.exp(-jnp.abs(x)))


def _rms_rows(x, gain):
    ms = jnp.mean(x * x, axis=-1, keepdims=True)
    return x * lax.rsqrt(ms + NORM_EPS) * gain


def _causal_conv(carry_ref, x, w):
    rows = x.shape[0]
    ext = jnp.concatenate([carry_ref[...], x], axis=0)
    y = w[CONV_K - 1:CONV_K, :] * x
    for k in range(CONV_K - 1):
        y = y + w[k:k + 1, :] * pltpu.roll(ext, CONV_K - 1 - k, axis=0)[SUBLANES:]
    carry_ref[...] = x[rows - SUBLANES:]
    return y


def _norm_cast_kernel(x_ref, g_ref, o_ref):
    o_ref[...] = _rms_rows(x_ref[...], g_ref[...]).astype(BF16)


def norm_cast(x, gain, tm):
    m, k = x.shape
    return pl.pallas_call(
        _norm_cast_kernel,
        grid=(m // tm,),
        in_specs=[pl.BlockSpec((tm, k), lambda i: (i, 0)), pl.BlockSpec((1, k), lambda i: (0, 0))],
        out_specs=pl.BlockSpec((tm, k), lambda i: (i, 0)),
        out_shape=jax.ShapeDtypeStruct((m, k), BF16),
        compiler_params=_cparams("parallel"),
        name="norm_cast",
    )(x, gain.reshape(1, k))


def _ws_matmul_kernel(*refs, n_a, normalize, residual, w_transposed):
    a_refs = refs[:n_a]
    rest = list(refs[n_a:])
    g_ref = rest.pop(0) if normalize else None
    w_ref = rest.pop(0)
    r_ref = rest.pop(0) if residual else None
    o_ref, wb_ref = rest

    @pl.when(pl.program_id(1) == 0)
    def _():
        w = w_ref[0]
        wb_ref[...] = (w.T if w_transposed else w).astype(BF16)

    acc = None
    k0 = 0
    for a_ref in a_refs:
        a = a_ref[...]
        if normalize:
            a = _rms_rows(a, g_ref[...]).astype(BF16)
        part = _dot(a, wb_ref[k0:k0 + a.shape[1], :])
        k0 += a.shape[1]
        acc = part if acc is None else acc + part
    o_ref[...] = r_ref[...] + acc if residual else acc


def ws_matmul(a_list, w, layer, n, tm, tn, gain=None, resid=None, single_buffer_w=False, w_transposed=False):
    m = a_list[0].shape[0]
    k = w.shape[2] if w_transposed else w.shape[1]
    normalize = gain is not None
    residual = resid is not None
    w_mode = dict(pipeline_mode=pl.Buffered(1)) if single_buffer_w else {}
    in_specs = [pl.BlockSpec((tm, a.shape[1]), lambda j, i: (i, 0)) for a in a_list]
    args = list(a_list)
    if normalize:
        in_specs.append(pl.BlockSpec((1, k), lambda j, i: (0, 0)))
        args.append(gain.reshape(1, k))
    if w_transposed:
        in_specs.append(pl.BlockSpec((1, tn, k), lambda j, i: (layer, j, 0), **w_mode))
    else:
        in_specs.append(pl.BlockSpec((1, k, tn), lambda j, i: (layer, 0, j), **w_mode))
    args.append(w)
    if residual:
        in_specs.append(pl.BlockSpec((tm, tn), lambda j, i: (i, j)))
        args.append(resid)
    return pl.pallas_call(
        functools.partial(_ws_matmul_kernel, n_a=len(a_list), normalize=normalize, residual=residual,
                          w_transposed=w_transposed),
        grid=(n // tn, m // tm),
        in_specs=in_specs,
        out_specs=pl.BlockSpec((tm, tn), lambda j, i: (i, j)),
        out_shape=jax.ShapeDtypeStruct((m, n), F32),
        scratch_shapes=[pltpu.VMEM((k, tn), BF16)],
        compiler_params=_cparams("arbitrary", "arbitrary"),
        name="ws_matmul",
    )(*args)


def _ffn_kernel(x_hbm, g_ref, wg_ref, wu_ref, wd_ref, gf_ref, o_ref, xn_ref, sem, *, final_norm):
    f = pl.program_id(1)

    @pl.when(f == 0)
    def _():
        tm = o_ref.shape[0]
        rows = pl.ds(pl.multiple_of(pl.program_id(0) * tm, tm), tm)
        copy = pltpu.make_async_copy(x_hbm.at[rows, :], o_ref, sem)
        copy.start()
        copy.wait()
        xn_ref[...] = _rms_rows(o_ref[...], g_ref[...]).astype(BF16)

    xn = xn_ref[...]
    act = (_silu(_dot(xn, wg_ref[0].astype(BF16))) * _dot(xn, wu_ref[0].astype(BF16))).astype(BF16)
    o_ref[...] += _dot(act, wd_ref[0].astype(BF16))

    if final_norm:
        @pl.when(f == pl.num_programs(1) - 1)
        def _():
            o_ref[...] = _rms_rows(o_ref[...], gf_ref[...])


def ffn_block(x, gain, wg, wu, wd, layer, gain_final, tm, tf, final_norm):
    m, d = x.shape
    dff = wg.shape[2]
    return pl.pallas_call(
        functools.partial(_ffn_kernel, final_norm=final_norm),
        grid=(m // tm, dff // tf),
        in_specs=[
            pl.BlockSpec(memory_space=pl.ANY),
            pl.BlockSpec((1, d), lambda i, f: (0, 0)),
            pl.BlockSpec((1, d, tf), lambda i, f: (layer, 0, f)),
            pl.BlockSpec((1, d, tf), lambda i, f: (layer, 0, f)),
            pl.BlockSpec((1, tf, d), lambda i, f: (layer, f, 0)),
            pl.BlockSpec((1, d), lambda i, f: (0, 0)),
        ],
        out_specs=pl.BlockSpec((tm, d), lambda i, f: (i, 0), pipeline_mode=pl.Buffered(1)),
        out_shape=jax.ShapeDtypeStruct((m, d), F32),
        scratch_shapes=[pltpu.VMEM((tm, d), BF16), pltpu.SemaphoreType.DMA],
        compiler_params=_cparams("parallel", "arbitrary"),
        name="ffn_block",
    )(x, gain.reshape(1, d), wg, wu, wd, gain_final.reshape(1, d))


def _mem_kv_kernel(m_ref, g_ref, wk_ref, wv_ref, k_ref, v_ref):
    mn = _rms_rows(m_ref[0], g_ref[...]).astype(BF16)
    k_ref[0] = _dot(mn, wk_ref[0].astype(BF16)).astype(BF16)
    v_ref[0] = _dot(mn, wv_ref[0].astype(BF16)).astype(BF16)


def mem_kv(mem, gain, wk, wv, layer):
    b, ml, d = mem.shape
    xa = wk.shape[2]
    return pl.pallas_call(
        _mem_kv_kernel,
        grid=(b,),
        in_specs=[
            pl.BlockSpec((1, ml, d), lambda i: (i, 0, 0)),
            pl.BlockSpec((1, d), lambda i: (0, 0)),
            pl.BlockSpec((1, d, xa), lambda i: (layer, 0, 0)),
            pl.BlockSpec((1, d, xa), lambda i: (layer, 0, 0)),
        ],
        out_specs=[pl.BlockSpec((1, ml, xa), lambda i: (i, 0, 0))] * 2,
        out_shape=[jax.ShapeDtypeStruct((b, ml, xa), BF16)] * 2,
        compiler_params=_cparams("parallel"),
        name="mem_kv",
    )(mem, gain.reshape(1, d), wk, wv)


def _xattn_kernel(x_ref, g_ref, wq_ref, k_ref, v_ref, wo_ref, o_ref, wqb_ref, wob_ref):
    @pl.when((pl.program_id(0) == 0) & (pl.program_id(1) == 0))
    def _():
        wqb_ref[...] = wq_ref[0].astype(BF16)
        wob_ref[...] = wo_ref[0].astype(BF16)

    x = x_ref[0]
    xn = _rms_rows(x, g_ref[...]).astype(BF16)
    q = (_dot(xn, wqb_ref[...]) * (XA_HEAD_DIM ** -0.5)).astype(BF16)
    outs = []
    for h in range(XA_HEADS):
        sl = slice(h * XA_HEAD_DIM, (h + 1) * XA_HEAD_DIM)
        s = _dot_nt(q[:, sl], k_ref[0, :, sl])
        p = jnp.exp(s - jnp.max(s, axis=-1, keepdims=True))
        den = jnp.sum(p, axis=-1, keepdims=True)
        outs.append(_dot(p.astype(BF16), v_ref[0, :, sl]) / den)
    o = jnp.concatenate(outs, axis=-1).astype(BF16)
    o_ref[0] = x + _dot(o, wob_ref[...])


def xattn_block(x, gain, wq, k, v, wo, layer, tm):
    b, s, d = x.shape
    ml, xa = k.shape[1], k.shape[2]
    return pl.pallas_call(
        _xattn_kernel,
        grid=(b, s // tm),
        in_specs=[
            pl.BlockSpec((1, tm, d), lambda i, t: (i, t, 0)),
            pl.BlockSpec((1, d), lambda i, t: (0, 0)),
            pl.BlockSpec((1, d, xa), lambda i, t: (layer, 0, 0)),
            pl.BlockSpec((1, ml, xa), lambda i, t: (i, 0, 0)),
            pl.BlockSpec((1, ml, xa), lambda i, t: (i, 0, 0)),
            pl.BlockSpec((1, xa, d), lambda i, t: (layer, 0, 0)),
        ],
        out_specs=pl.BlockSpec((1, tm, d), lambda i, t: (i, t, 0)),
        out_shape=jax.ShapeDtypeStruct((b, s, d), F32),
        scratch_shapes=[pltpu.VMEM((d, xa), BF16), pltpu.VMEM((xa, d), BF16)],
        compiler_params=_cparams("arbitrary", "arbitrary"),
        name="xattn_block",
    )(x, gain.reshape(1, d), wq, k, v, wo)


_HGRN_LEVELS = (32, 16, 8, 4, 2, 1)


def _hgrn_masks():
    c = CHUNK
    i = np.arange(c)[:, None]
    j = np.arange(c)[None, :]
    masks = [(i // (2 * s) == j // (2 * s)) & (i % (2 * s) >= s) & (j % (2 * s) < s) for s in _HGRN_LEVELS]
    masks.append(i == j)
    return jnp.asarray(np.stack(masks).astype(np.float32))


def _segment_row(x, seg, idx):
    rows, lanes = x.shape
    x3 = x.reshape(rows // seg, seg, lanes)
    return jnp.broadcast_to(x3[:, idx:idx + 1, :], x3.shape).reshape(rows, lanes)


def _hgrn_kernel(q_ref, f_ref, i_ref, gate_ref, lbl_ref, nw_ref, masks_ref, y_ref, st_ref, *, layer, n_chunks):
    c = CHUNK
    tt = n_chunks * c

    @pl.when(pl.program_id(2) == 0)
    def _():
        st_ref[...] = jnp.zeros_like(st_ref)

    logits = lbl_ref[...]
    ex = jnp.exp(logits - jnp.max(logits, axis=0, keepdims=True))
    sm = ex / jnp.sum(ex, axis=0, keepdims=True)
    lb = jnp.sum(sm[0:layer + 1, :], axis=0, keepdims=True)

    a_f = f_ref[0]
    q = _silu(q_ref[0])
    log_f = jnp.log(lb + (1.0 - lb) * jax.nn.sigmoid(a_f))
    k = (1.0 - lb) * jax.nn.sigmoid(-a_f)
    vb = i_ref[0].astype(BF16)

    row = lax.broadcasted_iota(jnp.int32, (tt, HEAD_DIM), 0)
    rc = row & (c - 1)
    g = log_f
    for s in (1, 2, 4, 8, 16, 32):
        g = g + jnp.where(rc >= s, pltpu.roll(g, s, axis=0), 0.0)
    g_last = _segment_row(g, c, c - 1)

    exps = []
    for s in _HGRN_LEVELS:
        if 2 * s >= SUBLANES:
            exps.append(-jnp.abs(g - _segment_row(g, 2 * s, s - 1)))
        elif s == 2:
            r4 = row & 3
            below = pltpu.roll(log_f, 1, axis=0)
            above = pltpu.roll(log_f, tt - 1, axis=0)
            exps.append(jnp.where(r4 == 0, above, jnp.where(r4 == 1, 0.0,
                                                            jnp.where(r4 == 2, log_f, log_f + below))))
        else:
            exps.append(jnp.where((row & 1) == 1, log_f, 0.0))

    def chunks(x):
        xb = x.astype(BF16)
        return [xb[ci * c:(ci + 1) * c] for ci in range(n_chunks)]

    qs, ks = chunks(q), chunks(k)
    n_lv = len(_HGRN_LEVELS)
    attn = [masks_ref[n_lv] * _dot_nt(qs[ci], ks[ci]) for ci in range(n_chunks)]
    for lv in range(n_lv):
        sc = jnp.exp(exps[lv])
        ql, kl = chunks(q * sc), chunks(k * sc)
        m = masks_ref[lv]
        attn = [attn[ci] + m * _dot_nt(ql[ci], kl[ci]) for ci in range(n_chunks)]

    vs = chunks(vb)
    qg = chunks(q * jnp.exp(g))
    kd = chunks(k * jnp.exp(g_last - g))
    upd = [_dot_tn(vs[ci], kd[ci]) for ci in range(n_chunks)]
    intra = [_dot(attn[ci].astype(BF16), vs[ci]) for ci in range(n_chunks)]
    dec = jnp.exp(g_last)
    st = st_ref[...]
    outs = []
    for ci in range(n_chunks):
        outs.append(intra[ci] + _dot_nt(qg[ci], st.astype(BF16)))
        st = st * dec[ci * c:ci * c + 1, :] + upd[ci]
    st_ref[...] = st
    o = jnp.concatenate(outs, axis=0)
    y = o * lax.rsqrt(jnp.mean(o * o, axis=-1, keepdims=True) + NORM_EPS)
    y_ref[0] = (y * nw_ref[...] * _silu(gate_ref[0])).astype(y_ref.dtype)


def hgrn_heads(proj, lb_logits, norm_w, layer, col0, tt):
    b, s, _ = proj.shape
    nl = lb_logits.shape[0]
    masks = _hgrn_masks()
    blk = lambda o: pl.BlockSpec((1, tt, HEAD_DIM), lambda i, h, t: (i, t, col0 + o + h))
    return pl.pallas_call(
        functools.partial(_hgrn_kernel, layer=layer, n_chunks=tt // CHUNK),
        grid=(b, N_HEADS, s // tt),
        in_specs=[
            blk(0), blk(N_HEADS), blk(2 * N_HEADS), blk(3 * N_HEADS),
            pl.BlockSpec((nl, HEAD_DIM), lambda i, h, t: (0, h)),
            pl.BlockSpec((1, HEAD_DIM), lambda i, h, t: (0, h)),
            pl.BlockSpec(masks.shape, lambda i, h, t: (0, 0, 0)),
        ],
        out_specs=pl.BlockSpec((1, tt, HEAD_DIM), lambda i, h, t: (i, t, h)),
        out_shape=jax.ShapeDtypeStruct((b, s, N_HEADS * HEAD_DIM), BF16),
        scratch_shapes=[pltpu.VMEM((HEAD_DIM, HEAD_DIM), F32)],
        compiler_params=_cparams("parallel", "parallel", "arbitrary"),
        name="hgrn_heads",
    )(proj, proj, proj, proj, lb_logits, norm_w.reshape(1, -1), masks)


def _gdn_kernel(q_ref, k_ref, v_ref, gate_ref, gc_ref, gr_ref, cw_ref, alane_ref, blane_ref, acol_ref, bcol_ref,
                nw_ref, y_ref, s_ref, xq_ref, xk_ref, xv_ref, *, tt):
    c = CHUNK
    hd = HEAD_DIM
    pair = 2 * c
    n_pairs = tt // pair
    n_chunks = tt // c
    width = N_HEADS * hd

    @pl.when(pl.program_id(1) == 0)
    def _():
        s_ref[...] = jnp.zeros_like(s_ref)
        for r in (xq_ref, xk_ref, xv_ref):
            r[...] = jnp.zeros_like(r)

    q_all = _silu(_causal_conv(xq_ref, q_ref[0], cw_ref[:, 0:width]))
    k_all = _silu(_causal_conv(xk_ref, k_ref[0], cw_ref[:, width:2 * width]))
    v_all = _silu(_causal_conv(xv_ref, v_ref[0], cw_ref[:, 2 * width:3 * width]))

    gates = gc_ref[0]
    beta_all = jax.nn.sigmoid(gates)
    gam_all = -jnp.exp(alane_ref[...]) * _softplus(gates + blane_ref[...])
    row = lax.broadcasted_iota(jnp.int32, (tt, LANES), 0)
    rc = row & (c - 1)
    for s in (1, 2, 4, 8, 16, 32):
        gam_all = gam_all + jnp.where(rc >= s, pltpu.roll(gam_all, s, axis=0), 0.0)
    glast_all = _segment_row(gam_all, c, c - 1)
    gam_rows = -jnp.exp(acol_ref[...]) * _softplus(gr_ref[0] + bcol_ref[...])
    lc = lax.broadcasted_iota(jnp.int32, gam_rows.shape, 1) & (c - 1)
    for s in (1, 2, 4, 8, 16, 32):
        gam_rows = gam_rows + jnp.where(lc >= s, pltpu.roll(gam_rows, s, axis=1), 0.0)

    ii = lax.broadcasted_iota(jnp.int32, (c, LANES), 0)
    ll = lax.broadcasted_iota(jnp.int32, (c, LANES), 1)
    jj = ll & (c - 1)
    left = ll < c
    lower = ii >= jj
    strict = ii > jj
    eye2 = jnp.where(ii == jj, 1.0, 0.0)

    def block_diag(m):
        zero = jnp.zeros_like(m)
        return jnp.concatenate([jnp.where(left, m, zero), jnp.where(left, zero, m)], axis=0)

    def split2(x):
        hi = x.astype(BF16)
        return hi, (x - hi.astype(F32)).astype(BF16)

    def times_block_diag(lhs, p_hi, p_lo):
        l_hi, l_lo = split2(lhs)
        return _dot(jnp.concatenate([l_hi, l_lo, l_hi], axis=1), jnp.concatenate([p_hi, p_hi, p_lo], axis=0))

    units = [(h, p) for h in range(N_HEADS) for p in range(n_pairs)]
    qd, kd, rhs, khb, kbb, qhb, gam_h = [], [], [], [], [], [], []
    for h in range(N_HEADS):
        lanes = slice(h * hd, (h + 1) * hd)
        qh, kh, vh = q_all[:, lanes], k_all[:, lanes], v_all[:, lanes]
        qh = qh * lax.rsqrt(jnp.sum(qh * qh, axis=-1, keepdims=True) + L2_EPS) * (hd ** -0.5)
        kh = kh * lax.rsqrt(jnp.sum(kh * kh, axis=-1, keepdims=True) + L2_EPS)
        beta = beta_all[:, h:h + 1]
        gam = gam_all[:, N_HEADS + h:N_HEADS + h + 1]
        eg = jnp.exp(gam)
        kb = kh * beta
        gam_h.append(gam)
        rhs.append(jnp.concatenate([kb * eg, vh * beta], axis=1).astype(BF16))
        qd.append((qh * eg).astype(BF16))
        kd.append((kh * jnp.exp(glast_all[:, N_HEADS + h:N_HEADS + h + 1] - gam)).astype(BF16))
        khb.append(kh.astype(BF16))
        kbb.append(kb.astype(BF16))
        qhb.append(qh.astype(BF16))

    x_mats, qk_tiles = [], [[] for _ in range(N_HEADS)]
    for h, p in units:
        r0 = p * pair
        rows = slice(r0, r0 + pair)
        res = _dot_nt(jnp.concatenate([kbb[h][rows], qhb[h][rows]], axis=0), khb[h][rows])
        col = jnp.where(left, jnp.broadcast_to(gam_h[h][r0:r0 + c], (c, LANES)),
                        jnp.broadcast_to(gam_h[h][r0 + c:r0 + pair], (c, LANES)))
        decay = jnp.exp(jnp.minimum(col - gam_rows[N_HEADS + h:N_HEADS + h + 1, rows], 0.0))
        x_mats.append(jnp.where(strict, jnp.where(left, res[0:c], res[c:pair]) * -decay, 0.0))
        qk_tiles[h].append(jnp.where(lower & left, res[pair:pair + c] * decay, 0.0).astype(BF16))
        qk_tiles[h].append(jnp.where(lower & (~left), res[pair + c:2 * pair] * decay, 0.0).astype(BF16))

    t_mats = [eye2 + x for x in x_mats]
    powers = []
    for x in x_mats:
        p_hi, p_lo = split2(x)
        powers.append(times_block_diag(x, block_diag(p_hi), block_diag(p_lo)))
    for it in range(5):
        for u in range(len(units)):
            p_hi, p_lo = split2(powers[u])
            bd_hi, bd_lo = block_diag(p_hi), block_diag(p_lo)
            if it < 4:
                both = times_block_diag(jnp.concatenate([powers[u], t_mats[u]], axis=0), bd_hi, bd_lo)
                powers[u], t_mats[u] = both[0:c], t_mats[u] + both[c:pair]
            else:
                t_mats[u] = t_mats[u] + times_block_diag(t_mats[u], bd_hi, bd_lo)
    wu = [[] for _ in range(N_HEADS)]
    for u, (h, p) in enumerate(units):
        wu[h].append(_dot(block_diag(t_mats[u].astype(BF16)), rhs[h][p * pair:(p + 1) * pair]))

    outs = [[] for _ in range(N_HEADS)]
    zeros_half = jnp.zeros((c, hd), BF16)
    heads = range(N_HEADS)
    states = [s_ref[h] for h in heads]
    chunk_decay = jnp.exp(glast_all)
    for ci in range(n_chunks):
        rows = slice(ci * c, (ci + 1) * c)
        half = slice((ci % 2) * c, (ci % 2 + 1) * c)
        stb = [states[h].astype(BF16) for h in heads]
        wu_c = [wu[h][ci // 2][half] for h in heads]
        w_s = [_dot(wu_c[h][:, :hd].astype(BF16), stb[h]) for h in heads]
        vnb = [(wu_c[h][:, hd:] - w_s[h]).astype(BF16) for h in heads]
        for h in heads:
            v2 = jnp.concatenate([vnb[h], zeros_half] if ci % 2 == 0 else [zeros_half, vnb[h]], axis=0)
            outs[h].append(_dot(qd[h][rows], stb[h]) + _dot(qk_tiles[h][ci], v2))
        states = [states[h] * chunk_decay[ci * c:ci * c + 1, N_HEADS + h:N_HEADS + h + 1]
                  + _dot_tn(kd[h][rows], vnb[h]) for h in heads]
    for h in heads:
        s_ref[h] = states[h]

    for h in range(N_HEADS):
        lanes = slice(h * hd, (h + 1) * hd)
        o = jnp.concatenate(outs[h], axis=0)
        y = o * lax.rsqrt(jnp.mean(o * o, axis=-1, keepdims=True) + NORM_EPS)
        y_ref[0, :, lanes] = (y * nw_ref[:, lanes] * _silu(gate_ref[0, :, lanes])).astype(y_ref.dtype)


def gdn_heads(proj, gates_col, gates_row, conv_w, a_log, dt_bias, norm_w, col0, tt):
    b, s, _ = proj.shape
    width = N_HEADS * HEAD_DIM
    blk = lambda o: pl.BlockSpec((1, tt, width), lambda i, t: (i, t, col0 + o))
    whole = lambda shape: pl.BlockSpec(shape, lambda i, t: (0,) * len(shape))
    lane_vec = jnp.zeros((1, LANES), F32).at[0, N_HEADS:2 * N_HEADS]
    col_vec = jnp.zeros((2 * N_HEADS, 1), F32).at[N_HEADS:, 0]
    return pl.pallas_call(
        functools.partial(_gdn_kernel, tt=tt),
        grid=(b, s // tt),
        in_specs=[
            blk(0), blk(1), blk(2), blk(3),
            pl.BlockSpec((1, tt, LANES), lambda i, t: (i, t, 0)),
            pl.BlockSpec((1, 2 * N_HEADS, tt), lambda i, t: (i, 0, t)),
            whole(conv_w.shape),
            whole((1, LANES)), whole((1, LANES)), whole((2 * N_HEADS, 1)), whole((2 * N_HEADS, 1)),
            whole((1, width)),
        ],
        out_specs=pl.BlockSpec((1, tt, width), lambda i, t: (i, t, 0)),
        out_shape=jax.ShapeDtypeStruct((b, s, width), BF16),
        scratch_shapes=[
            pltpu.VMEM((N_HEADS, HEAD_DIM, HEAD_DIM), F32),
            pltpu.VMEM((SUBLANES, width), F32),
            pltpu.VMEM((SUBLANES, width), F32),
            pltpu.VMEM((SUBLANES, width), F32),
        ],
        compiler_params=_cparams("parallel", "arbitrary"),
        name="gdn_heads",
    )(proj, proj, proj, proj, gates_col, gates_row, conv_w,
      lane_vec.set(a_log), lane_vec.set(dt_bias), col_vec.set(a_log), col_vec.set(dt_bias),
      norm_w.reshape(1, -1))


SSD_PAIRS = SSD_HEADS_PER_GROUP // 2


def _ssd_expand_table():
    expand = np.zeros((SSD_GROUPS, LANES, SSD_GROUP_WIDTH), np.float32)
    for g in range(SSD_GROUPS):
        for hh in range(SSD_HEADS_PER_GROUP):
            expand[g, g * SSD_HEADS_PER_GROUP + hh, hh * SSD_HEADDIM:(hh + 1) * SSD_HEADDIM] = 1.0
    return jnp.asarray(np.concatenate([expand] * 3, axis=1), BF16)


def _ssd_kernel(xfirst_ref, xnext_ref, wz_ref, wx_ref, wb_ref, wc_ref, dtc_ref, dtp_ref, cwx_ref, cwb_ref, cwc_ref,
                cbx_ref, cbb_ref, cbc_ref, blane_ref, alane_ref, bpair_ref, apair_ref, de_ref, exp_ref,
                y_ref, ht_ref, xx_ref, xb_ref, xc_ref, wbf_ref, proj_ref, pnext_ref, *, tt):
    c = CHUNK
    n_chunks = tt // c
    hp = SSD_HEADDIM
    gw = SSD_GROUP_WIDTH

    @pl.when((pl.program_id(1) == 0) & (pl.program_id(2) == 0))
    def _():
        wbf_ref[:, 0:gw] = wz_ref[0].T.astype(BF16)
        wbf_ref[:, gw:2 * gw] = wx_ref[0].T.astype(BF16)
        wbf_ref[:, 2 * gw:2 * gw + LANES] = wb_ref[0].T.astype(BF16)
        wbf_ref[:, 2 * gw + LANES:2 * gw + 2 * LANES] = wc_ref[0].T.astype(BF16)
        proj_ref[...] = _dot(xfirst_ref[...], wbf_ref[...])

    @pl.when(pl.program_id(2) == 0)
    def _():
        for r in (ht_ref, xx_ref, xb_ref, xc_ref):
            r[...] = jnp.zeros_like(r)

    piece_w = 2 * LANES
    pieces = [slice(p0, p0 + piece_w) for p0 in range(0, wbf_ref.shape[1], piece_w)]

    def project_piece():
        cols = pieces.pop(0)
        pnext_ref[:, cols] = _dot(xnext_ref[...], wbf_ref[:, cols])

    project_piece()

    xs = _silu(_causal_conv(xx_ref, proj_ref[:, gw:2 * gw], cwx_ref[...]) + cbx_ref[...])
    bm = _silu(_causal_conv(xb_ref, proj_ref[:, 2 * gw:2 * gw + LANES], cwb_ref[...])
               + cbb_ref[...]).astype(BF16)
    cm = _silu(_causal_conv(xc_ref, proj_ref[:, 2 * gw + LANES:2 * gw + 2 * LANES], cwc_ref[...])
               + cbc_ref[...]).astype(BF16)

    dt_c = _softplus(dtc_ref[0] + blane_ref[...])
    acum_c = dt_c * -jnp.exp(alane_ref[...])
    rc = lax.broadcasted_iota(jnp.int32, (tt, LANES), 0) & (c - 1)
    for s in (1, 2, 4, 8, 16, 32):
        acum_c = acum_c + jnp.where(rc >= s, pltpu.roll(acum_c, s, axis=0), 0.0)
    s_c = dt_c * jnp.exp(_segment_row(acum_c, c, c - 1) - acum_c)
    a_hi, a_mid, a_lo = _split3(acum_c)
    acum_e = _dot(jnp.concatenate([a_hi, a_mid, a_lo], axis=1), exp_ref[0])
    s_hi, s_mid, _ = _split3(s_c)
    s_e = _dot(jnp.concatenate([s_hi, s_mid], axis=1), exp_ref[0, 0:2 * LANES, :])
    decay_e = jnp.exp(acum_e)
    xw = (xs * s_e).astype(BF16)
    xsb = xs.astype(BF16)

    project_piece()
    dt_p = _softplus(dtp_ref[0, 0] + jnp.concatenate([bpair_ref[0]] * n_chunks, axis=1))
    acum_p = dt_p * -jnp.exp(jnp.concatenate([apair_ref[0]] * n_chunks, axis=1))
    lc = lax.broadcasted_iota(jnp.int32, acum_p.shape, 1) & (c - 1)
    for s in (1, 2, 4, 8, 16, 32):
        acum_p = acum_p + jnp.where(lc >= s, pltpu.roll(acum_p, s, axis=1), 0.0)

    ii = lax.broadcasted_iota(jnp.int32, (c, LANES), 0)
    ll = lax.broadcasted_iota(jnp.int32, (c, LANES), 1)
    lower2 = ii >= (ll & (c - 1))
    left = ll < hp

    chunk_rows = [slice(ci * c, (ci + 1) * c) for ci in range(n_chunks)]
    cb2 = [_dot_nt(cm[r], jnp.concatenate([bm[r], bm[r]], axis=0)) for r in chunk_rows]
    upd = [_dot_tn(bm[r], xw[r]) for r in chunk_rows]
    y_diag = []
    for ci, r in enumerate(chunk_rows):
        tiles = []
        for pr in range(SSD_PAIRS):
            lanes = slice(2 * pr * hp, (2 * pr + 2) * hp)
            tok = slice(ci * LANES, (ci + 1) * LANES)
            lmat = jnp.where(lower2, jnp.exp(jnp.minimum(acum_e[r, lanes] - acum_p[pr:pr + 1, tok], 0.0)), 0.0)
            xp = xsb[r, lanes]
            zero = jnp.zeros_like(xp)
            rhs = jnp.concatenate([jnp.where(left, xp, zero), jnp.where(left, zero, xp)], axis=0)
            tiles.append(_dot((cb2[ci] * lmat * dt_p[pr:pr + 1, tok]).astype(BF16), rhs))
        y_diag.append(jnp.concatenate(tiles, axis=1))
        if pieces and ci % 2 == 1:
            project_piece()
    while pieces:
        project_piece()
    ht = ht_ref[...]
    y_off = []
    for ci, r in enumerate(chunk_rows):
        y_off.append(_dot(cm[r], ht.astype(BF16)))
        ht = ht * decay_e[ci * c + c - 1:ci * c + c, :] + upd[ci]
    ht_ref[...] = ht
    y = jnp.concatenate(y_diag, axis=0) + jnp.concatenate(y_off, axis=0) * decay_e + de_ref[...] * xs
    y_ref[0] = y * _silu(proj_ref[:, 0:gw])
    proj_ref[...] = pnext_ref[...]


def ssd_groups(xn, w_t, layer, dt_col, dt_pairs, conv_w, conv_b, dt_bias, a_log, d_skip, tt):
    b, s, _ = dt_col.shape
    k = xn.shape[1]
    n_t = s // tt
    g_w = SSD_GROUP_WIDTH
    n_xblk = SSD_GROUPS
    expand = _ssd_expand_table()
    rep = lambda p: jnp.repeat(p.astype(F32), SSD_HEADDIM).reshape(1, -1)
    row2 = lambda v: v.reshape(1, -1)
    lane_vec = lambda p: jnp.pad(p.astype(F32), (0, LANES - p.shape[0])).reshape(1, LANES)

    def pair_tile(p):
        t = jnp.repeat(p.astype(F32).reshape(SSD_GROUPS, SSD_PAIRS, 2, 1), CHUNK, axis=-1)
        t = t.reshape(SSD_GROUPS, SSD_PAIRS, LANES)
        return jnp.pad(t, ((0, 0), (0, SUBLANES - SSD_PAIRS), (0, 0)))
    x_off = n_xblk
    b_off = 2 * n_xblk * (g_w // LANES)
    c_off = b_off + SSD_GROUPS
    p_w = 2 * g_w + 2 * LANES
    w_wide = lambda o: pl.BlockSpec((1, g_w, k), lambda g, i, t: (layer, o + g, 0))
    w_narrow = lambda o: pl.BlockSpec((1, LANES, k), lambda g, i, t: (layer, o + g, 0))
    return pl.pallas_call(
        functools.partial(_ssd_kernel, tt=tt),
        grid=(SSD_GROUPS, b, n_t),
        in_specs=[
            pl.BlockSpec((tt, k), lambda g, i, t: (0, 0)),
            pl.BlockSpec((tt, k), lambda g, i, t: ((i * n_t + t + 1) % (b * n_t), 0)),
            w_wide(0), w_wide(x_off), w_narrow(b_off), w_narrow(c_off),
            pl.BlockSpec((1, tt, LANES), lambda g, i, t: (i, t, 0)),
            pl.BlockSpec((1, 1, SUBLANES, 2 * tt), lambda g, i, t: (i, g, 0, t)),
            pl.BlockSpec((CONV_K, g_w), lambda g, i, t: (0, g)),
            pl.BlockSpec((CONV_K, LANES), lambda g, i, t: (0, n_xblk * (g_w // LANES) + g)),
            pl.BlockSpec((CONV_K, LANES), lambda g, i, t: (0, n_xblk * (g_w // LANES) + SSD_GROUPS + g)),
            pl.BlockSpec((1, g_w), lambda g, i, t: (0, g)),
            pl.BlockSpec((1, LANES), lambda g, i, t: (0, n_xblk * (g_w // LANES) + g)),
            pl.BlockSpec((1, LANES), lambda g, i, t: (0, n_xblk * (g_w // LANES) + SSD_GROUPS + g)),
            pl.BlockSpec((1, LANES), lambda g, i, t: (0, 0)),
            pl.BlockSpec((1, LANES), lambda g, i, t: (0, 0)),
            pl.BlockSpec((1, SUBLANES, LANES), lambda g, i, t: (g, 0, 0)),
            pl.BlockSpec((1, SUBLANES, LANES), lambda g, i, t: (g, 0, 0)),
            pl.BlockSpec((1, g_w), lambda g, i, t: (0, g)),
            pl.BlockSpec((1, 3 * LANES, g_w), lambda g, i, t: (g, 0, 0)),
        ],
        out_specs=pl.BlockSpec((1, tt, g_w), lambda g, i, t: (i, t, g)),
        out_shape=jax.ShapeDtypeStruct((b, s, SSD_GROUPS * g_w), F32),
        scratch_shapes=[
            pltpu.VMEM((SSD_DSTATE, g_w), F32),
            pltpu.VMEM((SUBLANES, g_w), F32),
            pltpu.VMEM((SUBLANES, LANES), F32),
            pltpu.VMEM((SUBLANES, LANES), F32),
            pltpu.VMEM((k, p_w), BF16),
            pltpu.VMEM((tt, p_w), F32),
            pltpu.VMEM((tt, p_w), F32),
        ],
        compiler_params=_cparams("arbitrary", "arbitrary", "arbitrary"),
        name="ssd_groups",
    )(xn, xn, w_t, w_t, w_t, w_t, dt_col, dt_pairs, conv_w, conv_w, conv_w,
      row2(conv_b), row2(conv_b), row2(conv_b), lane_vec(dt_bias), lane_vec(a_log),
      pair_tile(dt_bias), pair_tile(a_log), rep(d_skip), expand)


def _tail_weight(w_t, layer, n_main):
    tail = w_t[layer, n_main:, :]
    return jnp.pad(tail, ((0, LANES - tail.shape[0]), (0, 0)))[None]


def _tile(n, pref):
    t = min(n, pref)
    while n % t:
        t //= 2
    return t


def kernel(x, mem, norm_mix, norm_xattn, norm_mem, norm_ffn, norm_final, hy_w_in, hgrn_lb_logits, hgrn_norm,
           gdn_conv_w, gdn_a_log, gdn_dt_bias, gdn_norm, hy_w_out, ssd_w_in, ssd_conv_w, ssd_conv_b,
           ssd_dt_bias, ssd_a_log, ssd_d, ssd_norm, ssd_w_out, xa_wq, xa_wk, xa_wv, xa_wo, ffn_w_gate,
           ffn_w_up, ffn_w_down):
    bsz, seq, d = x.shape
    tok = bsz * seq
    depth = norm_mix.shape[0]
    tm = _tile(tok, TOKEN_TILE)
    tt = _tile(seq, SEQ_TILE)
    hy_w_in_t = jnp.swapaxes(hy_w_in, 1, 2)
    ssd_w_in_t = jnp.swapaxes(ssd_w_in, 1, 2)

    h = x.reshape(tok, d)
    for layer in range(depth):
        xn = norm_cast(h, norm_mix[layer], tm)
        if layer % 2 == 0:
            e = layer // 2
            n_main = 8 * N_HEADS * HEAD_DIM
            proj = ws_matmul([xn], hy_w_in_t, e, n_main, tm, _tile(n_main, COLUMN_TILE),
                             w_transposed=True).reshape(bsz, seq, n_main)
            tail = ws_matmul([xn], _tail_weight(hy_w_in_t, e, n_main), 0, LANES, tm, LANES, w_transposed=True)
            gates_col = tail.reshape(bsz, seq, LANES)
            gates_row = jnp.swapaxes(gates_col[:, :, :2 * N_HEADS], 1, 2)
            y_a = hgrn_heads(proj, hgrn_lb_logits.astype(F32), hgrn_norm[e], e, 0, tt)
            y_b = gdn_heads(proj, gates_col, gates_row, gdn_conv_w[e].astype(F32), gdn_a_log[e].astype(F32),
                            gdn_dt_bias[e].astype(F32), gdn_norm[e], 4, _tile(seq, GDN_SEQ_TILE))
            h = ws_matmul([y_a.reshape(tok, -1), y_b.reshape(tok, -1)], hy_w_out, e, d, tm,
                          _tile(d, COLUMN_TILE), resid=h)
        else:
            o = layer // 2
            n_heads = ssd_dt_bias.shape[1]
            n_main = ssd_w_in.shape[2] - n_heads
            tail = ws_matmul([xn], _tail_weight(ssd_w_in_t, o, n_main), 0, LANES, tm, LANES, w_transposed=True)
            dt_col = tail.reshape(bsz, seq, LANES)
            dt_pairs = dt_col[:, :, :n_heads].reshape(bsz, seq // CHUNK, CHUNK, SSD_GROUPS, SSD_PAIRS, 2)
            dt_pairs = dt_pairs.transpose(0, 3, 4, 1, 5, 2).reshape(bsz, SSD_GROUPS, SSD_PAIRS, 2 * seq)
            dt_pairs = jnp.pad(dt_pairs, ((0, 0), (0, 0), (0, SUBLANES - SSD_PAIRS), (0, 0)))
            y = ssd_groups(xn, ssd_w_in_t, o, dt_col, dt_pairs, ssd_conv_w[o].astype(F32), ssd_conv_b[o].astype(F32),
                           ssd_dt_bias[o], ssd_a_log[o], ssd_d[o], tt)
            h = ws_matmul([y.reshape(tok, -1)], ssd_w_out, o, d, _tile(tok, TOKEN_TILE // 2),
                          _tile(d, COLUMN_TILE), gain=ssd_norm[o], resid=h, single_buffer_w=True)
        k_mem, v_mem = mem_kv(mem, norm_mem[layer], xa_wk, xa_wv, layer)
        h = xattn_block(h.reshape(bsz, seq, d), norm_xattn[layer], xa_wq, k_mem, v_mem, xa_wo, layer,
                        _tile(seq, 512)).reshape(tok, d)
        h = ffn_block(h, norm_ffn[layer], ffn_w_gate, ffn_w_up, ffn_w_down, layer, norm_final,
                      _tile(tok, FFN_TOKEN_TILE), _tile(ffn_w_gate.shape[2], FFN_TILE), layer == depth - 1)
    return h.reshape(bsz, seq, d)
```

```python
import functools

import numpy as np
import jax
import jax.numpy as jnp
from jax import lax
from jax.experimental import pallas as pl
from jax.experimental.pallas import tpu as pltpu

F32 = jnp.float32
BF16 = jnp.bfloat16

NORM_EPS = 1e-6
L2_EPS = 1e-6
CHUNK = 64
CONV_K = 4
LANES = 128
SUBLANES = 8
HEAD_DIM = 128
N_HEADS = 8
SSD_HEADDIM = 64
SSD_GROUPS = 8
SSD_HEADS_PER_GROUP = 8
SSD_GROUP_WIDTH = SSD_HEADDIM * SSD_HEADS_PER_GROUP
SSD_DSTATE = 128
XA_HEADS = 4
XA_HEAD_DIM = 128
VMEM_LIMIT_BYTES = 56 * 1024 * 1024
TOKEN_TILE = 1024
COLUMN_TILE = 1024
PROJ_PIECE = 256
FFN_TILE = 256
FFN_TOKEN_TILE = 2048
SEQ_TILE = 512
GDN_SEQ_TILE = 256


def _cparams(*sem):
    return pltpu.CompilerParams(dimension_semantics=sem, vmem_limit_bytes=VMEM_LIMIT_BYTES)


def _dot(a, b):
    return jnp.dot(a, b, preferred_element_type=F32)


def _dot_nt(a, b):
    return lax.dot_general(a, b, (((1,), (1,)), ((), ())), preferred_element_type=F32)


def _dot_tn(a, b):
    return lax.dot_general(a, b, (((0,), (0,)), ((), ())), preferred_element_type=F32)


def _split3(x):
    hi = x.astype(BF16)
    r1 = x - hi.astype(F32)
    mid = r1.astype(BF16)
    lo = (r1 - mid.astype(F32)).astype(BF16)
    return hi, mid, lo


def _silu(x):
    return x * jax.nn.sigmoid(x)


def _softplus(x):
    return jnp.maximum(x, 0.0) + jnp.log(1.0 + jnp.exp(-jnp.abs(x)))


def _rms_rows(x, gain):
    ms = jnp.mean(x * x, axis=-1, keepdims=True)
    return x * lax.rsqrt(ms + NORM_EPS) * gain


def _causal_conv(carry_ref, x, w):
    rows = x.shape[0]
    ext = jnp.concatenate([carry_ref[...], x], axis=0)
    y = w[CONV_K - 1:CONV_K, :] * x
    for k in range(CONV_K - 1):
        y = y + w[k:k + 1, :] * pltpu.roll(ext, CONV_K - 1 - k, axis=0)[SUBLANES:]
    carry_ref[...] = x[rows - SUBLANES:]
    return y


def _norm_cast_kernel(x_ref, g_ref, wt_ref, o_ref, tail_ref):
    xn = _rms_rows(x_ref[...], g_ref[...]).astype(BF16)
    o_ref[...] = xn
    tail_ref[...] = _dot_nt(xn, wt_ref[0].astype(BF16))


def norm_cast(x, gain, w_tail_t, tm):
    m, k = x.shape
    return pl.pallas_call(
        _norm_cast_kernel,
        grid=(m // tm,),
        in_specs=[pl.BlockSpec((tm, k), lambda i: (i, 0)), pl.BlockSpec((1, k), lambda i: (0, 0)),
                  pl.BlockSpec((1, LANES, k), lambda i: (0, 0, 0))],
        out_specs=[pl.BlockSpec((tm, k), lambda i: (i, 0)), pl.BlockSpec((tm, LANES), lambda i: (i, 0))],
        out_shape=[jax.ShapeDtypeStruct((m, k), BF16), jax.ShapeDtypeStruct((m, LANES), F32)],
        compiler_params=_cparams("parallel"),
        name="norm_cast",
    )(x, gain.reshape(1, k), w_tail_t)


def _ws_matmul_kernel(*refs, n_a, normalize, residual, w_transposed):
    a_refs = refs[:n_a]
    rest = list(refs[n_a:])
    g_ref = rest.pop(0) if normalize else None
    w_ref = rest.pop(0)
    r_ref = rest.pop(0) if residual else None
    o_ref, wb_ref = rest

    @pl.when(pl.program_id(1) == 0)
    def _():
        w = w_ref[0]
        wb_ref[...] = (w.T if w_transposed else w).astype(BF16)

    acc = None
    k0 = 0
    for a_ref in a_refs:
        a = a_ref[...]
        if normalize:
            a = _rms_rows(a, g_ref[...]).astype(BF16)
        part = _dot(a, wb_ref[k0:k0 + a.shape[1], :])
        k0 += a.shape[1]
        acc = part if acc is None else acc + part
    o_ref[...] = r_ref[...] + acc if residual else acc


def ws_matmul(a_list, w, layer, n, tm, tn, gain=None, resid=None, single_buffer_w=False, w_transposed=False):
    m = a_list[0].shape[0]
    k = w.shape[2] if w_transposed else w.shape[1]
    normalize = gain is not None
    residual = resid is not None
    w_mode = dict(pipeline_mode=pl.Buffered(1)) if single_buffer_w else {}
    in_specs = [pl.BlockSpec((tm, a.shape[1]), lambda j, i: (i, 0)) for a in a_list]
    args = list(a_list)
    if normalize:
        in_specs.append(pl.BlockSpec((1, k), lambda j, i: (0, 0)))
        args.append(gain.reshape(1, k))
    if w_transposed:
        in_specs.append(pl.BlockSpec((1, tn, k), lambda j, i: (layer, j, 0), **w_mode))
    else:
        in_specs.append(pl.BlockSpec((1, k, tn), lambda j, i: (layer, 0, j), **w_mode))
    args.append(w)
    if residual:
        in_specs.append(pl.BlockSpec((tm, tn), lambda j, i: (i, j)))
        args.append(resid)
    return pl.pallas_call(
        functools.partial(_ws_matmul_kernel, n_a=len(a_list), normalize=normalize, residual=residual,
                          w_transposed=w_transposed),
        grid=(n // tn, m // tm),
        in_specs=in_specs,
        out_specs=pl.BlockSpec((tm, tn), lambda j, i: (i, j)),
        out_shape=jax.ShapeDtypeStruct((m, n), F32),
        scratch_shapes=[pltpu.VMEM((k, tn), BF16)],
        compiler_params=_cparams("arbitrary", "arbitrary"),
        name="ws_matmul",
    )(*args)


def _ffn_kernel(x_hbm, g_ref, wg_ref, wu_ref, wd_ref, gf_ref, o_ref, xn_ref, sem, *, final_norm):
    f = pl.program_id(1)

    @pl.when(f == 0)
    def _():
        tm = o_ref.shape[0]
        rows = pl.ds(pl.multiple_of(pl.program_id(0) * tm, tm), tm)
        copy = pltpu.make_async_copy(x_hbm.at[rows, :], o_ref, sem)
        copy.start()
        copy.wait()
        xn_ref[...] = _rms_rows(o_ref[...], g_ref[...]).astype(BF16)

    xn = xn_ref[...]
    act = (_silu(_dot(xn, wg_ref[0].astype(BF16))) * _dot(xn, wu_ref[0].astype(BF16))).astype(BF16)
    o_ref[...] += _dot(act, wd_ref[0].astype(BF16))

    if final_norm:
        @pl.when(f == pl.num_programs(1) - 1)
        def _():
            o_ref[...] = _rms_rows(o_ref[...], gf_ref[...])


def ffn_block(x, gain, wg, wu, wd, layer, gain_final, tm, tf, final_norm):
    m, d = x.shape
    dff = wg.shape[2]
    return pl.pallas_call(
        functools.partial(_ffn_kernel, final_norm=final_norm),
        grid=(m // tm, dff // tf),
        in_specs=[
            pl.BlockSpec(memory_space=pl.ANY),
            pl.BlockSpec((1, d), lambda i, f: (0, 0)),
            pl.BlockSpec((1, d, tf), lambda i, f: (layer, 0, f)),
            pl.BlockSpec((1, d, tf), lambda i, f: (layer, 0, f)),
            pl.BlockSpec((1, tf, d), lambda i, f: (layer, f, 0)),
            pl.BlockSpec((1, d), lambda i, f: (0, 0)),
        ],
        out_specs=pl.BlockSpec((tm, d), lambda i, f: (i, 0), pipeline_mode=pl.Buffered(1)),
        out_shape=jax.ShapeDtypeStruct((m, d), F32),
        scratch_shapes=[pltpu.VMEM((tm, d), BF16), pltpu.SemaphoreType.DMA],
        compiler_params=_cparams("parallel", "arbitrary"),
        name="ffn_block",
    )(x, gain.reshape(1, d), wg, wu, wd, gain_final.reshape(1, d))


def _mem_kv_kernel(m_ref, g_ref, wk_ref, wv_ref, k_ref, v_ref):
    mn = _rms_rows(m_ref[0], g_ref[...]).astype(BF16)
    k_ref[0] = _dot(mn, wk_ref[0].astype(BF16)).astype(BF16)
    v_ref[0] = _dot(mn, wv_ref[0].astype(BF16)).astype(BF16)


def mem_kv(mem, gain, wk, wv, layer):
    b, ml, d = mem.shape
    xa = wk.shape[2]
    return pl.pallas_call(
        _mem_kv_kernel,
        grid=(b,),
        in_specs=[
            pl.BlockSpec((1, ml, d), lambda i: (i, 0, 0)),
            pl.BlockSpec((1, d), lambda i: (0, 0)),
            pl.BlockSpec((1, d, xa), lambda i: (layer, 0, 0)),
            pl.BlockSpec((1, d, xa), lambda i: (layer, 0, 0)),
        ],
        out_specs=[pl.BlockSpec((1, ml, xa), lambda i: (i, 0, 0))] * 2,
        out_shape=[jax.ShapeDtypeStruct((b, ml, xa), BF16)] * 2,
        compiler_params=_cparams("parallel"),
        name="mem_kv",
    )(mem, gain.reshape(1, d), wk, wv)


def _xattn_kernel(x_ref, g_ref, wq_ref, k_ref, v_ref, wo_ref, o_ref, wqb_ref, wob_ref):
    @pl.when((pl.program_id(0) == 0) & (pl.program_id(1) == 0))
    def _():
        wqb_ref[...] = wq_ref[0].astype(BF16)
        wob_ref[...] = wo_ref[0].astype(BF16)

    x = x_ref[0]
    xn = _rms_rows(x, g_ref[...]).astype(BF16)
    q = (_dot(xn, wqb_ref[...]) * (XA_HEAD_DIM ** -0.5)).astype(BF16)
    outs = []
    for h in range(XA_HEADS):
        sl = slice(h * XA_HEAD_DIM, (h + 1) * XA_HEAD_DIM)
        s = _dot_nt(q[:, sl], k_ref[0, :, sl])
        p = jnp.exp(s - jnp.max(s, axis=-1, keepdims=True))
        den = jnp.sum(p, axis=-1, keepdims=True)
        outs.append(_dot(p.astype(BF16), v_ref[0, :, sl]) / den)
    o = jnp.concatenate(outs, axis=-1).astype(BF16)
    o_ref[0] = x + _dot(o, wob_ref[...])


def xattn_block(x, gain, wq, k, v, wo, layer, tm):
    b, s, d = x.shape
    ml, xa = k.shape[1], k.shape[2]
    return pl.pallas_call(
        _xattn_kernel,
        grid=(b, s // tm),
        in_specs=[
            pl.BlockSpec((1, tm, d), lambda i, t: (i, t, 0)),
            pl.BlockSpec((1, d), lambda i, t: (0, 0)),
            pl.BlockSpec((1, d, xa), lambda i, t: (layer, 0, 0)),
            pl.BlockSpec((1, ml, xa), lambda i, t: (i, 0, 0)),
            pl.BlockSpec((1, ml, xa), lambda i, t: (i, 0, 0)),
            pl.BlockSpec((1, xa, d), lambda i, t: (layer, 0, 0)),
        ],
        out_specs=pl.BlockSpec((1, tm, d), lambda i, t: (i, t, 0)),
        out_shape=jax.ShapeDtypeStruct((b, s, d), F32),
        scratch_shapes=[pltpu.VMEM((d, xa), BF16), pltpu.VMEM((xa, d), BF16)],
        compiler_params=_cparams("arbitrary", "arbitrary"),
        name="xattn_block",
    )(x, gain.reshape(1, d), wq, k, v, wo)


_HGRN_LEVELS = (32, 16, 8, 4, 2, 1)


def _hgrn_masks():
    c = CHUNK
    i = np.arange(c)[:, None]
    j = np.arange(c)[None, :]
    masks = [(i // (2 * s) == j // (2 * s)) & (i % (2 * s) >= s) & (j % (2 * s) < s) for s in _HGRN_LEVELS]
    masks.append(i == j)
    return jnp.asarray(np.stack(masks).astype(np.float32))


def _segment_row(x, seg, idx):
    rows, lanes = x.shape
    x3 = x.reshape(rows // seg, seg, lanes)
    return jnp.broadcast_to(x3[:, idx:idx + 1, :], x3.shape).reshape(rows, lanes)


def _hgrn_kernel(q_ref, f_ref, i_ref, gate_ref, lbl_ref, nw_ref, masks_ref, y_ref, st_ref, *, layer, n_chunks):
    c = CHUNK
    tt = n_chunks * c

    @pl.when(pl.program_id(2) == 0)
    def _():
        st_ref[...] = jnp.zeros_like(st_ref)

    logits = lbl_ref[...]
    ex = jnp.exp(logits - jnp.max(logits, axis=0, keepdims=True))
    sm = ex / jnp.sum(ex, axis=0, keepdims=True)
    lb = jnp.sum(sm[0:layer + 1, :], axis=0, keepdims=True)

    a_f = f_ref[0]
    q = _silu(q_ref[0])
    log_f = jnp.log(lb + (1.0 - lb) * jax.nn.sigmoid(a_f))
    k = (1.0 - lb) * jax.nn.sigmoid(-a_f)
    vb = i_ref[0].astype(BF16)

    row = lax.broadcasted_iota(jnp.int32, (tt, HEAD_DIM), 0)
    rc = row & (c - 1)
    g = log_f
    for s in (1, 2, 4, 8, 16, 32):
        g = g + jnp.where(rc >= s, pltpu.roll(g, s, axis=0), 0.0)
    g_last = _segment_row(g, c, c - 1)

    exps = []
    for s in _HGRN_LEVELS:
        if 2 * s >= SUBLANES:
            exps.append(-jnp.abs(g - _segment_row(g, 2 * s, s - 1)))
        elif s == 2:
            r4 = row & 3
            below = pltpu.roll(log_f, 1, axis=0)
            above = pltpu.roll(log_f, tt - 1, axis=0)
            exps.append(jnp.where(r4 == 0, above, jnp.where(r4 == 1, 0.0,
                                                            jnp.where(r4 == 2, log_f, log_f + below))))
        else:
            exps.append(jnp.where((row & 1) == 1, log_f, 0.0))

    def chunks(x):
        xb = x.astype(BF16)
        return [xb[ci * c:(ci + 1) * c] for ci in range(n_chunks)]

    qs, ks = chunks(q), chunks(k)
    n_lv = len(_HGRN_LEVELS)
    attn = [masks_ref[n_lv] * _dot_nt(qs[ci], ks[ci]) for ci in range(n_chunks)]
    for lv in range(n_lv):
        sc = jnp.exp(exps[lv])
        ql, kl = chunks(q * sc), chunks(k * sc)
        m = masks_ref[lv]
        attn = [attn[ci] + m * _dot_nt(ql[ci], kl[ci]) for ci in range(n_chunks)]

    vs = chunks(vb)
    qg = chunks(q * jnp.exp(g))
    kd = chunks(k * jnp.exp(g_last - g))
    upd = [_dot_tn(vs[ci], kd[ci]) for ci in range(n_chunks)]
    intra = [_dot(attn[ci].astype(BF16), vs[ci]) for ci in range(n_chunks)]
    dec = jnp.exp(g_last)
    st = st_ref[...]
    outs = []
    for ci in range(n_chunks):
        outs.append(intra[ci] + _dot_nt(qg[ci], st.astype(BF16)))
        st = st * dec[ci * c:ci * c + 1, :] + upd[ci]
    st_ref[...] = st
    o = jnp.concatenate(outs, axis=0)
    y = o * lax.rsqrt(jnp.mean(o * o, axis=-1, keepdims=True) + NORM_EPS)
    y_ref[0] = (y * nw_ref[...] * _silu(gate_ref[0])).astype(y_ref.dtype)


def hgrn_heads(proj, lb_logits, norm_w, layer, col0, tt):
    b, s, _ = proj.shape
    nl = lb_logits.shape[0]
    masks = _hgrn_masks()
    blk = lambda o: pl.BlockSpec((1, tt, HEAD_DIM), lambda i, h, t: (i, t, col0 + o + h))
    return pl.pallas_call(
        functools.partial(_hgrn_kernel, layer=layer, n_chunks=tt // CHUNK),
        grid=(b, N_HEADS, s // tt),
        in_specs=[
            blk(0), blk(N_HEADS), blk(2 * N_HEADS), blk(3 * N_HEADS),
            pl.BlockSpec((nl, HEAD_DIM), lambda i, h, t: (0, h)),
            pl.BlockSpec((1, HEAD_DIM), lambda i, h, t: (0, h)),
            pl.BlockSpec(masks.shape, lambda i, h, t: (0, 0, 0)),
        ],
        out_specs=pl.BlockSpec((1, tt, HEAD_DIM), lambda i, h, t: (i, t, h)),
        out_shape=jax.ShapeDtypeStruct((b, s, N_HEADS * HEAD_DIM), BF16),
        scratch_shapes=[pltpu.VMEM((HEAD_DIM, HEAD_DIM), F32)],
        compiler_params=_cparams("parallel", "parallel", "arbitrary"),
        name="hgrn_heads",
    )(proj, proj, proj, proj, lb_logits, norm_w.reshape(1, -1), masks)


def _gdn_kernel(q_ref, k_ref, v_ref, gate_ref, gc_ref, gr_ref, cw_ref, alane_ref, blane_ref, acol_ref, bcol_ref,
                nw_ref, y_ref, s_ref, xq_ref, xk_ref, xv_ref, *, tt):
    c = CHUNK
    hd = HEAD_DIM
    pair = 2 * c
    n_pairs = tt // pair
    n_chunks = tt // c
    width = N_HEADS * hd

    @pl.when(pl.program_id(1) == 0)
    def _():
        s_ref[...] = jnp.zeros_like(s_ref)
        for r in (xq_ref, xk_ref, xv_ref):
            r[...] = jnp.zeros_like(r)

    q_all = _silu(_causal_conv(xq_ref, q_ref[0], cw_ref[:, 0:width]))
    k_all = _silu(_causal_conv(xk_ref, k_ref[0], cw_ref[:, width:2 * width]))
    v_all = _silu(_causal_conv(xv_ref, v_ref[0], cw_ref[:, 2 * width:3 * width]))

    gates = gc_ref[0]
    beta_all = jax.nn.sigmoid(gates)
    gam_all = -jnp.exp(alane_ref[...]) * _softplus(gates + blane_ref[...])
    row = lax.broadcasted_iota(jnp.int32, (tt, LANES), 0)
    rc = row & (c - 1)
    for s in (1, 2, 4, 8, 16, 32):
        gam_all = gam_all + jnp.where(rc >= s, pltpu.roll(gam_all, s, axis=0), 0.0)
    glast_all = _segment_row(gam_all, c, c - 1)
    gam_rows = -jnp.exp(acol_ref[...]) * _softplus(gr_ref[0] + bcol_ref[...])
    lc = lax.broadcasted_iota(jnp.int32, gam_rows.shape, 1) & (c - 1)
    for s in (1, 2, 4, 8, 16, 32):
        gam_rows = gam_rows + jnp.where(lc >= s, pltpu.roll(gam_rows, s, axis=1), 0.0)

    ii = lax.broadcasted_iota(jnp.int32, (c, LANES), 0)
    ll = lax.broadcasted_iota(jnp.int32, (c, LANES), 1)
    jj = ll & (c - 1)
    left = ll < c
    lower = ii >= jj
    strict = ii > jj
    eye2 = jnp.where(ii == jj, 1.0, 0.0)

    def block_diag(m):
        zero = jnp.zeros_like(m)
        return jnp.concatenate([jnp.where(left, m, zero), jnp.where(left, zero, m)], axis=0)

    def split2(x):
        hi = x.astype(BF16)
        return hi, (x - hi.astype(F32)).astype(BF16)

    def times_block_diag(lhs, p_hi, p_lo):
        l_hi, l_lo = split2(lhs)
        return _dot(jnp.concatenate([l_hi, l_lo, l_hi], axis=1), jnp.concatenate([p_hi, p_hi, p_lo], axis=0))

    units = [(h, p) for h in range(N_HEADS) for p in range(n_pairs)]
    qd, kd, rhs, khb, kbb, qhb, gam_h = [], [], [], [], [], [], []
    for h in range(N_HEADS):
        lanes = slice(h * hd, (h + 1) * hd)
        qh, kh, vh = q_all[:, lanes], k_all[:, lanes], v_all[:, lanes]
        qh = qh * lax.rsqrt(jnp.sum(qh * qh, axis=-1, keepdims=True) + L2_EPS) * (hd ** -0.5)
        kh = kh * lax.rsqrt(jnp.sum(kh * kh, axis=-1, keepdims=True) + L2_EPS)
        beta = beta_all[:, h:h + 1]
        gam = gam_all[:, N_HEADS + h:N_HEADS + h + 1]
        eg = jnp.exp(gam)
        kb = kh * beta
        gam_h.append(gam)
        rhs.append(jnp.concatenate([kb * eg, vh * beta], axis=1).astype(BF16))
        qd.append((qh * eg).astype(BF16))
        kd.append((kh * jnp.exp(glast_all[:, N_HEADS + h:N_HEADS + h + 1] - gam)).astype(BF16))
        khb.append(kh.astype(BF16))
        kbb.append(kb.astype(BF16))
        qhb.append(qh.astype(BF16))

    x_mats, qk_tiles = [], [[] for _ in range(N_HEADS)]
    for h, p in units:
        r0 = p * pair
        rows = slice(r0, r0 + pair)
        res = _dot_nt(jnp.concatenate([kbb[h][rows], qhb[h][rows]], axis=0), khb[h][rows])
        col = jnp.where(left, jnp.broadcast_to(gam_h[h][r0:r0 + c], (c, LANES)),
                        jnp.broadcast_to(gam_h[h][r0 + c:r0 + pair], (c, LANES)))
        decay = jnp.exp(jnp.minimum(col - gam_rows[N_HEADS + h:N_HEADS + h + 1, rows], 0.0))
        x_mats.append(jnp.where(strict, jnp.where(left, res[0:c], res[c:pair]) * -decay, 0.0))
        qk_tiles[h].append(jnp.where(lower & left, res[pair:pair + c] * decay, 0.0).astype(BF16))
        qk_tiles[h].append(jnp.where(lower & (~left), res[pair + c:2 * pair] * decay, 0.0).astype(BF16))

    t_mats = [eye2 + x for x in x_mats]
    powers = []
    for x in x_mats:
        p_hi, p_lo = split2(x)
        powers.append(times_block_diag(x, block_diag(p_hi), block_diag(p_lo)))
    for it in range(5):
        for u in range(len(units)):
            p_hi, p_lo = split2(powers[u])
            bd_hi, bd_lo = block_diag(p_hi), block_diag(p_lo)
            if it < 4:
                both = times_block_diag(jnp.concatenate([powers[u], t_mats[u]], axis=0), bd_hi, bd_lo)
                powers[u], t_mats[u] = both[0:c], t_mats[u] + both[c:pair]
            else:
                t_mats[u] = t_mats[u] + times_block_diag(t_mats[u], bd_hi, bd_lo)
    wu = [[] for _ in range(N_HEADS)]
    for u, (h, p) in enumerate(units):
        wu[h].append(_dot(block_diag(t_mats[u].astype(BF16)), rhs[h][p * pair:(p + 1) * pair]))

    outs = [[] for _ in range(N_HEADS)]
    zeros_half = jnp.zeros((c, hd), BF16)
    heads = range(N_HEADS)
    states = [s_ref[h] for h in heads]
    chunk_decay = jnp.exp(glast_all)
    for ci in range(n_chunks):
        rows = slice(ci * c, (ci + 1) * c)
        half = slice((ci % 2) * c, (ci % 2 + 1) * c)
        stb = [states[h].astype(BF16) for h in heads]
        wu_c = [wu[h][ci // 2][half] for h in heads]
        w_s = [_dot(wu_c[h][:, :hd].astype(BF16), stb[h]) for h in heads]
        vnb = [(wu_c[h][:, hd:] - w_s[h]).astype(BF16) for h in heads]
        for h in heads:
            v2 = jnp.concatenate([vnb[h], zeros_half] if ci % 2 == 0 else [zeros_half, vnb[h]], axis=0)
            outs[h].append(_dot(qd[h][rows], stb[h]) + _dot(qk_tiles[h][ci], v2))
        states = [states[h] * chunk_decay[ci * c:ci * c + 1, N_HEADS + h:N_HEADS + h + 1]
                  + _dot_tn(kd[h][rows], vnb[h]) for h in heads]
    for h in heads:
        s_ref[h] = states[h]

    for h in range(N_HEADS):
        lanes = slice(h * hd, (h + 1) * hd)
        o = jnp.concatenate(outs[h], axis=0)
        y = o * lax.rsqrt(jnp.mean(o * o, axis=-1, keepdims=True) + NORM_EPS)
        y_ref[0, :, lanes] = (y * nw_ref[:, lanes] * _silu(gate_ref[0, :, lanes])).astype(y_ref.dtype)


def gdn_heads(proj, gates_col, gates_row, conv_w, a_log, dt_bias, norm_w, col0, tt):
    b, s, _ = proj.shape
    width = N_HEADS * HEAD_DIM
    blk = lambda o: pl.BlockSpec((1, tt, width), lambda i, t: (i, t, col0 + o))
    whole = lambda shape: pl.BlockSpec(shape, lambda i, t: (0,) * len(shape))
    lane_vec = jnp.zeros((1, LANES), F32).at[0, N_HEADS:2 * N_HEADS]
    col_vec = jnp.zeros((2 * N_HEADS, 1), F32).at[N_HEADS:, 0]
    return pl.pallas_call(
        functools.partial(_gdn_kernel, tt=tt),
        grid=(b, s // tt),
        in_specs=[
            blk(0), blk(1), blk(2), blk(3),
            pl.BlockSpec((1, tt, LANES), lambda i, t: (i, t, 0)),
            pl.BlockSpec((1, 2 * N_HEADS, tt), lambda i, t: (i, 0, t)),
            whole(conv_w.shape),
            whole((1, LANES)), whole((1, LANES)), whole((2 * N_HEADS, 1)), whole((2 * N_HEADS, 1)),
            whole((1, width)),
        ],
        out_specs=pl.BlockSpec((1, tt, width), lambda i, t: (i, t, 0)),
        out_shape=jax.ShapeDtypeStruct((b, s, width), BF16),
        scratch_shapes=[
            pltpu.VMEM((N_HEADS, HEAD_DIM, HEAD_DIM), F32),
            pltpu.VMEM((SUBLANES, width), F32),
            pltpu.VMEM((SUBLANES, width), F32),
            pltpu.VMEM((SUBLANES, width), F32),
        ],
        compiler_params=_cparams("parallel", "arbitrary"),
        name="gdn_heads",
    )(proj, proj, proj, proj, gates_col, gates_row, conv_w,
      lane_vec.set(a_log), lane_vec.set(dt_bias), col_vec.set(a_log), col_vec.set(dt_bias),
      norm_w.reshape(1, -1))


SSD_PAIRS = SSD_HEADS_PER_GROUP // 2


def _ssd_expand_table():
    expand = np.zeros((SSD_GROUPS, LANES, SSD_GROUP_WIDTH), np.float32)
    for g in range(SSD_GROUPS):
        for hh in range(SSD_HEADS_PER_GROUP):
            expand[g, g * SSD_HEADS_PER_GROUP + hh, hh * SSD_HEADDIM:(hh + 1) * SSD_HEADDIM] = 1.0
    return jnp.asarray(np.concatenate([expand] * 3, axis=1), BF16)


def _ssd_kernel(xfirst_ref, xnext_ref, wz_ref, wx_ref, wb_ref, wc_ref, dtc_ref, dtp_ref, cwx_ref, cwb_ref, cwc_ref,
                cbx_ref, cbb_ref, cbc_ref, blane_ref, alane_ref, bpair_ref, apair_ref, de_ref, exp_ref,
                y_ref, ht_ref, xx_ref, xb_ref, xc_ref, wbf_ref, proj_ref, pnext_ref, *, tt):
    c = CHUNK
    n_chunks = tt // c
    hp = SSD_HEADDIM
    gw = SSD_GROUP_WIDTH

    @pl.when((pl.program_id(1) == 0) & (pl.program_id(2) == 0))
    def _():
        wbf_ref[:, 0:gw] = wz_ref[0].T.astype(BF16)
        wbf_ref[:, gw:2 * gw] = wx_ref[0].T.astype(BF16)
        wbf_ref[:, 2 * gw:2 * gw + LANES] = wb_ref[0].T.astype(BF16)
        wbf_ref[:, 2 * gw + LANES:2 * gw + 2 * LANES] = wc_ref[0].T.astype(BF16)
        proj_ref[...] = _dot(xfirst_ref[...], wbf_ref[...])

    @pl.when(pl.program_id(2) == 0)
    def _():
        for r in (ht_ref, xx_ref, xb_ref, xc_ref):
            r[...] = jnp.zeros_like(r)

    piece_w = PROJ_PIECE
    pieces = [slice(p0, p0 + piece_w) for p0 in range(0, wbf_ref.shape[1], piece_w)]

    def project_piece():
        if pieces:
            cols = pieces.pop(0)
            pnext_ref[:, cols] = _dot(xnext_ref[...], wbf_ref[:, cols])

    project_piece()
    xs_tiles = []
    for lo in range(0, gw, PROJ_PIECE):
        lanes = slice(lo, lo + PROJ_PIECE)
        xs_tiles.append(_silu(_causal_conv(xx_ref.at[:, lanes], proj_ref[:, gw + lo:gw + lo + PROJ_PIECE],
                                           cwx_ref[:, lanes]) + cbx_ref[:, lanes]))
        project_piece()
    xs = jnp.concatenate(xs_tiles, axis=1)
    bm = _silu(_causal_conv(xb_ref, proj_ref[:, 2 * gw:2 * gw + LANES], cwb_ref[...])
               + cbb_ref[...]).astype(BF16)
    cm = _silu(_causal_conv(xc_ref, proj_ref[:, 2 * gw + LANES:2 * gw + 2 * LANES], cwc_ref[...])
               + cbc_ref[...]).astype(BF16)

    dt_c = _softplus(dtc_ref[0] + blane_ref[...])
    acum_c = dt_c * -jnp.exp(alane_ref[...])
    rc = lax.broadcasted_iota(jnp.int32, (tt, LANES), 0) & (c - 1)
    for s in (1, 2, 4, 8, 16, 32):
        acum_c = acum_c + jnp.where(rc >= s, pltpu.roll(acum_c, s, axis=0), 0.0)
    s_c = dt_c * jnp.exp(_segment_row(acum_c, c, c - 1) - acum_c)
    a_hi, a_mid, a_lo = _split3(acum_c)
    acum_e = _dot(jnp.concatenate([a_hi, a_mid, a_lo], axis=1), exp_ref[0])
    s_hi, s_mid, _ = _split3(s_c)
    s_e = _dot(jnp.concatenate([s_hi, s_mid], axis=1), exp_ref[0, 0:2 * LANES, :])
    decay_e = jnp.exp(acum_e)
    xw = (xs * s_e).astype(BF16)
    xsb = xs.astype(BF16)

    project_piece()
    dt_p = _softplus(dtp_ref[0, 0] + jnp.concatenate([bpair_ref[0]] * n_chunks, axis=1))
    acum_p = dt_p * -jnp.exp(jnp.concatenate([apair_ref[0]] * n_chunks, axis=1))
    lc = lax.broadcasted_iota(jnp.int32, acum_p.shape, 1) & (c - 1)
    for s in (1, 2, 4, 8, 16, 32):
        acum_p = acum_p + jnp.where(lc >= s, pltpu.roll(acum_p, s, axis=1), 0.0)

    ii = lax.broadcasted_iota(jnp.int32, (c, LANES), 0)
    ll = lax.broadcasted_iota(jnp.int32, (c, LANES), 1)
    lower2 = ii >= (ll & (c - 1))
    left = ll < hp
    while pieces:
        project_piece()

    chunk_rows = [slice(ci * c, (ci + 1) * c) for ci in range(n_chunks)]
    cb2 = [_dot_nt(cm[r], jnp.concatenate([bm[r], bm[r]], axis=0)) for r in chunk_rows]
    upd = [_dot_tn(bm[r], xw[r]) for r in chunk_rows]
    y_diag = []
    for ci, r in enumerate(chunk_rows):
        tiles = []
        for pr in range(SSD_PAIRS):
            lanes = slice(2 * pr * hp, (2 * pr + 2) * hp)
            tok = slice(ci * LANES, (ci + 1) * LANES)
            lmat = jnp.where(lower2, jnp.exp(jnp.minimum(acum_e[r, lanes] - acum_p[pr:pr + 1, tok], 0.0)), 0.0)
            xp = xsb[r, lanes]
            zero = jnp.zeros_like(xp)
            rhs = jnp.concatenate([jnp.where(left, xp, zero), jnp.where(left, zero, xp)], axis=0)
            tiles.append(_dot((cb2[ci] * lmat * dt_p[pr:pr + 1, tok]).astype(BF16), rhs))
        y_diag.append(jnp.concatenate(tiles, axis=1))
    ht = ht_ref[...]
    y_off = []
    for ci, r in enumerate(chunk_rows):
        y_off.append(_dot(cm[r], ht.astype(BF16)))
        ht = ht * decay_e[ci * c + c - 1:ci * c + c, :] + upd[ci]
    ht_ref[...] = ht
    y = jnp.concatenate(y_diag, axis=0) + jnp.concatenate(y_off, axis=0) * decay_e + de_ref[...] * xs
    y_ref[0] = y * _silu(proj_ref[:, 0:gw])
    proj_ref[...] = pnext_ref[...]


def ssd_groups(xn, w_t, layer, dt_col, dt_pairs, conv_w, conv_b, dt_bias, a_log, d_skip, tt):
    b, s, _ = dt_col.shape
    k = xn.shape[1]
    n_t = s // tt
    g_w = SSD_GROUP_WIDTH
    n_xblk = SSD_GROUPS
    expand = _ssd_expand_table()
    rep = lambda p: jnp.repeat(p.astype(F32), SSD_HEADDIM).reshape(1, -1)
    row2 = lambda v: v.reshape(1, -1)
    lane_vec = lambda p: jnp.pad(p.astype(F32), (0, LANES - p.shape[0])).reshape(1, LANES)

    def pair_tile(p):
        t = jnp.repeat(p.astype(F32).reshape(SSD_GROUPS, SSD_PAIRS, 2, 1), CHUNK, axis=-1)
        t = t.reshape(SSD_GROUPS, SSD_PAIRS, LANES)
        return jnp.pad(t, ((0, 0), (0, SUBLANES - SSD_PAIRS), (0, 0)))
    x_off = n_xblk
    b_off = 2 * n_xblk * (g_w // LANES)
    c_off = b_off + SSD_GROUPS
    p_w = 2 * g_w + 2 * LANES
    w_wide = lambda o: pl.BlockSpec((1, g_w, k), lambda g, i, t: (layer, o + g, 0))
    w_narrow = lambda o: pl.BlockSpec((1, LANES, k), lambda g, i, t: (layer, o + g, 0))
    return pl.pallas_call(
        functools.partial(_ssd_kernel, tt=tt),
        grid=(SSD_GROUPS, b, n_t),
        in_specs=[
            pl.BlockSpec((tt, k), lambda g, i, t: (0, 0)),
            pl.BlockSpec((tt, k), lambda g, i, t: ((i * n_t + t + 1) % (b * n_t), 0)),
            w_wide(0), w_wide(x_off), w_narrow(b_off), w_narrow(c_off),
            pl.BlockSpec((1, tt, LANES), lambda g, i, t: (i, t, 0)),
            pl.BlockSpec((1, 1, SUBLANES, 2 * tt), lambda g, i, t: (i, g, 0, t)),
            pl.BlockSpec((CONV_K, g_w), lambda g, i, t: (0, g)),
            pl.BlockSpec((CONV_K, LANES), lambda g, i, t: (0, n_xblk * (g_w // LANES) + g)),
            pl.BlockSpec((CONV_K, LANES), lambda g, i, t: (0, n_xblk * (g_w // LANES) + SSD_GROUPS + g)),
            pl.BlockSpec((1, g_w), lambda g, i, t: (0, g)),
            pl.BlockSpec((1, LANES), lambda g, i, t: (0, n_xblk * (g_w // LANES) + g)),
            pl.BlockSpec((1, LANES), lambda g, i, t: (0, n_xblk * (g_w // LANES) + SSD_GROUPS + g)),
            pl.BlockSpec((1, LANES), lambda g, i, t: (0, 0)),
            pl.BlockSpec((1, LANES), lambda g, i, t: (0, 0)),
            pl.BlockSpec((1, SUBLANES, LANES), lambda g, i, t: (g, 0, 0)),
            pl.BlockSpec((1, SUBLANES, LANES), lambda g, i, t: (g, 0, 0)),
            pl.BlockSpec((1, g_w), lambda g, i, t: (0, g)),
            pl.BlockSpec((1, 3 * LANES, g_w), lambda g, i, t: (g, 0, 0)),
        ],
        out_specs=pl.BlockSpec((1, tt, g_w), lambda g, i, t: (i, t, g)),
        out_shape=jax.ShapeDtypeStruct((b, s, SSD_GROUPS * g_w), F32),
        scratch_shapes=[
            pltpu.VMEM((SSD_DSTATE, g_w), F32),
            pltpu.VMEM((SUBLANES, g_w), F32),
            pltpu.VMEM((SUBLANES, LANES), F32),
            pltpu.VMEM((SUBLANES, LANES), F32),
            pltpu.VMEM((k, p_w), BF16),
            pltpu.VMEM((tt, p_w), F32),
            pltpu.VMEM((tt, p_w), F32),
        ],
        compiler_params=_cparams("arbitrary", "arbitrary", "arbitrary"),
        name="ssd_groups",
    )(xn, xn, w_t, w_t, w_t, w_t, dt_col, dt_pairs, conv_w, conv_w, conv_w,
      row2(conv_b), row2(conv_b), row2(conv_b), lane_vec(dt_bias), lane_vec(a_log),
      pair_tile(dt_bias), pair_tile(a_log), rep(d_skip), expand)


def _tail_weight(w_t, layer, n_main):
    tail = w_t[layer, n_main:, :]
    return jnp.pad(tail, ((0, LANES - tail.shape[0]), (0, 0)))[None]


def _tile(n, pref):
    t = min(n, pref)
    while n % t:
        t //= 2
    return t


def kernel(x, mem, norm_mix, norm_xattn, norm_mem, norm_ffn, norm_final, hy_w_in, hgrn_lb_logits, hgrn_norm,
           gdn_conv_w, gdn_a_log, gdn_dt_bias, gdn_norm, hy_w_out, ssd_w_in, ssd_conv_w, ssd_conv_b,
           ssd_dt_bias, ssd_a_log, ssd_d, ssd_norm, ssd_w_out, xa_wq, xa_wk, xa_wv, xa_wo, ffn_w_gate,
           ffn_w_up, ffn_w_down):
    bsz, seq, d = x.shape
    tok = bsz * seq
    depth = norm_mix.shape[0]
    tm = _tile(tok, TOKEN_TILE)
    tt = _tile(seq, SEQ_TILE)
    hy_w_in_t = jnp.swapaxes(hy_w_in, 1, 2)
    ssd_w_in_t = jnp.swapaxes(ssd_w_in, 1, 2)

    h = x.reshape(tok, d)
    for layer in range(depth):
        if layer % 2 == 0:
            e = layer // 2
            n_main = 8 * N_HEADS * HEAD_DIM
            xn, tail = norm_cast(h, norm_mix[layer], _tail_weight(hy_w_in_t, e, n_main), tm)
            proj = ws_matmul([xn], hy_w_in_t, e, n_main, tm, _tile(n_main, COLUMN_TILE),
                             w_transposed=True).reshape(bsz, seq, n_main)
            gates_col = tail.reshape(bsz, seq, LANES)
            gates_row = jnp.swapaxes(gates_col[:, :, :2 * N_HEADS], 1, 2)
            y_a = hgrn_heads(proj, hgrn_lb_logits.astype(F32), hgrn_norm[e], e, 0, tt)
            y_b = gdn_heads(proj, gates_col, gates_row, gdn_conv_w[e].astype(F32), gdn_a_log[e].astype(F32),
                            gdn_dt_bias[e].astype(F32), gdn_norm[e], 4, _tile(seq, GDN_SEQ_TILE))
            h = ws_matmul([y_a.reshape(tok, -1), y_b.reshape(tok, -1)], hy_w_out, e, d, tm,
                          _tile(d, COLUMN_TILE), resid=h)
        else:
            o = layer // 2
            n_heads = ssd_dt_bias.shape[1]
            n_main = ssd_w_in.shape[2] - n_heads
            xn, tail = norm_cast(h, norm_mix[layer], _tail_weight(ssd_w_in_t, o, n_main), tm)
            dt_col = tail.reshape(bsz, seq, LANES)
            dt_pairs = dt_col[:, :, :n_heads].reshape(bsz, seq // CHUNK, CHUNK, SSD_GROUPS, SSD_PAIRS, 2)
            dt_pairs = dt_pairs.transpose(0, 3, 4, 1, 5, 2).reshape(bsz, SSD_GROUPS, SSD_PAIRS, 2 * seq)
            dt_pairs = jnp.pad(dt_pairs, ((0, 0), (0, 0), (0, SUBLANES - SSD_PAIRS), (0, 0)))
            y = ssd_groups(xn, ssd_w_in_t, o, dt_col, dt_pairs, ssd_conv_w[o].astype(F32), ssd_conv_b[o].astype(F32),
                           ssd_dt_bias[o], ssd_a_log[o], ssd_d[o], tt)
            h = ws_matmul([y.reshape(tok, -1)], ssd_w_out, o, d, _tile(tok, TOKEN_TILE // 2),
                          _tile(d, COLUMN_TILE), gain=ssd_norm[o], resid=h, single_buffer_w=True)
        k_mem, v_mem = mem_kv(mem, norm_mem[layer], xa_wk, xa_wv, layer)
        h = xattn_block(h.reshape(bsz, seq, d), norm_xattn[layer], xa_wq, k_mem, v_mem, xa_wo, layer,
                        _tile(seq, 512)).reshape(tok, d)
        h = ffn_block(h, norm_ffn[layer], ffn_w_gate, ffn_w_up, ffn_w_down, layer, norm_final,
                      _tile(tok, FFN_TOKEN_TILE), _tile(ffn_w_gate.shape[2], FFN_TILE), layer == depth - 1)
    return h.reshape(bsz, seq, d)
```

```python
import functools

import numpy as np
import jax
import jax.numpy as jnp
from jax import lax
from jax.experimental import pallas as pl
from jax.experimental.pallas import tpu as pltpu

F32 = jnp.float32
BF16 = jnp.bfloat16

NORM_EPS = 1e-6
L2_EPS = 1e-6
CHUNK = 64
CONV_K = 4
LANES = 128
SUBLANES = 8
HEAD_DIM = 128
N_HEADS = 8
SSD_HEADDIM = 64
SSD_GROUPS = 8
SSD_HEADS_PER_GROUP = 8
SSD_GROUP_WIDTH = SSD_HEADDIM * SSD_HEADS_PER_GROUP
SSD_DSTATE = 128
XA_HEADS = 4
XA_HEAD_DIM = 128
VMEM_LIMIT_BYTES = 56 * 1024 * 1024
TOKEN_TILE = 1024
COLUMN_TILE = 1024
PROJ_PIECE = 256
FFN_TILE = 256
FFN_TOKEN_TILE = 2048
SEQ_TILE = 512
GDN_SEQ_TILE = 256


def _cparams(*sem):
    return pltpu.CompilerParams(dimension_semantics=sem, vmem_limit_bytes=VMEM_LIMIT_BYTES)


def _dot(a, b):
    return jnp.dot(a, b, preferred_element_type=F32)


def _dot_nt(a, b):
    return lax.dot_general(a, b, (((1,), (1,)), ((), ())), preferred_element_type=F32)


def _dot_tn(a, b):
    return lax.dot_general(a, b, (((0,), (0,)), ((), ())), preferred_element_type=F32)


def _split3(x):
    hi = x.astype(BF16)
    r1 = x - hi.astype(F32)
    mid = r1.astype(BF16)
    lo = (r1 - mid.astype(F32)).astype(BF16)
    return hi, mid, lo


def _silu(x):
    return x * jax.nn.sigmoid(x)


def _softplus(x):
    return jnp.maximum(x, 0.0) + jnp.log(1.0 + jnp.exp(-jnp.abs(x)))


def _rms_rows(x, gain):
    ms = jnp.mean(x * x, axis=-1, keepdims=True)
    return x * lax.rsqrt(ms + NORM_EPS) * gain


def _causal_conv(carry_ref, x, w):
    rows = x.shape[0]
    ext = jnp.concatenate([carry_ref[...], x], axis=0)
    y = w[CONV_K - 1:CONV_K, :] * x
    for k in range(CONV_K - 1):
        y = y + w[k:k + 1, :] * pltpu.roll(ext, CONV_K - 1 - k, axis=0)[SUBLANES:]
    carry_ref[...] = x[rows - SUBLANES:]
    return y


def _norm_cast_kernel(x_ref, g_ref, wt_ref, o_ref, tail_ref):
    xn = _rms_rows(x_ref[...], g_ref[...]).astype(BF16)
    o_ref[...] = xn
    tail_ref[...] = _dot_nt(xn, wt_ref[0].astype(BF16))


def norm_cast(x, gain, w_tail_t, tm):
    m, k = x.shape
    return pl.pallas_call(
        _norm_cast_kernel,
        grid=(m // tm,),
        in_specs=[pl.BlockSpec((tm, k), lambda i: (i, 0)), pl.BlockSpec((1, k), lambda i: (0, 0)),
                  pl.BlockSpec((1, LANES, k), lambda i: (0, 0, 0))],
        out_specs=[pl.BlockSpec((tm, k), lambda i: (i, 0)), pl.BlockSpec((tm, LANES), lambda i: (i, 0))],
        out_shape=[jax.ShapeDtypeStruct((m, k), BF16), jax.ShapeDtypeStruct((m, LANES), F32)],
        compiler_params=_cparams("parallel"),
        name="norm_cast",
    )(x, gain.reshape(1, k), w_tail_t)


def _ws_matmul_kernel(*refs, n_a, normalize, residual, w_transposed):
    a_refs = refs[:n_a]
    rest = list(refs[n_a:])
    g_ref = rest.pop(0) if normalize else None
    w_ref = rest.pop(0)
    r_ref = rest.pop(0) if residual else None
    o_ref, wb_ref = rest

    @pl.when(pl.program_id(1) == 0)
    def _():
        w = w_ref[0]
        wb_ref[...] = (w.T if w_transposed else w).astype(BF16)

    acc = None
    k0 = 0
    for a_ref in a_refs:
        a = a_ref[...]
        if normalize:
            a = _rms_rows(a, g_ref[...]).astype(BF16)
        part = _dot(a, wb_ref[k0:k0 + a.shape[1], :])
        k0 += a.shape[1]
        acc = part if acc is None else acc + part
    o_ref[...] = r_ref[...] + acc if residual else acc


def ws_matmul(a_list, w, layer, n, tm, tn, gain=None, resid=None, single_buffer_w=False, w_transposed=False):
    m = a_list[0].shape[0]
    k = w.shape[2] if w_transposed else w.shape[1]
    normalize = gain is not None
    residual = resid is not None
    w_mode = dict(pipeline_mode=pl.Buffered(1)) if single_buffer_w else {}
    in_specs = [pl.BlockSpec((tm, a.shape[1]), lambda j, i: (i, 0)) for a in a_list]
    args = list(a_list)
    if normalize:
        in_specs.append(pl.BlockSpec((1, k), lambda j, i: (0, 0)))
        args.append(gain.reshape(1, k))
    if w_transposed:
        in_specs.append(pl.BlockSpec((1, tn, k), lambda j, i: (layer, j, 0), **w_mode))
    else:
        in_specs.append(pl.BlockSpec((1, k, tn), lambda j, i: (layer, 0, j), **w_mode))
    args.append(w)
    if residual:
        in_specs.append(pl.BlockSpec((tm, tn), lambda j, i: (i, j)))
        args.append(resid)
    return pl.pallas_call(
        functools.partial(_ws_matmul_kernel, n_a=len(a_list), normalize=normalize, residual=residual,
                          w_transposed=w_transposed),
        grid=(n // tn, m // tm),
        in_specs=in_specs,
        out_specs=pl.BlockSpec((tm, tn), lambda j, i: (i, j)),
        out_shape=jax.ShapeDtypeStruct((m, n), F32),
        scratch_shapes=[pltpu.VMEM((k, tn), BF16)],
        compiler_params=_cparams("arbitrary", "arbitrary"),
        name="ws_matmul",
    )(*args)


def _ffn_kernel(x_hbm, g_ref, wg_ref, wu_ref, wd_ref, gf_ref, o_ref, xn_ref, sem, *, final_norm):
    f = pl.program_id(1)

    @pl.when(f == 0)
    def _():
        tm = o_ref.shape[0]
        rows = pl.ds(pl.multiple_of(pl.program_id(0) * tm, tm), tm)
        copy = pltpu.make_async_copy(x_hbm.at[rows, :], o_ref, sem)
        copy.start()
        copy.wait()
        xn_ref[...] = _rms_rows(o_ref[...], g_ref[...]).astype(BF16)

    xn = xn_ref[...]
    act = (_silu(_dot(xn, wg_ref[0].astype(BF16))) * _dot(xn, wu_ref[0].astype(BF16))).astype(BF16)
    o_ref[...] += _dot(act, wd_ref[0].astype(BF16))

    if final_norm:
        @pl.when(f == pl.num_programs(1) - 1)
        def _():
            o_ref[...] = _rms_rows(o_ref[...], gf_ref[...])


def ffn_block(x, gain, wg, wu, wd, layer, gain_final, tm, tf, final_norm):
    m, d = x.shape
    dff = wg.shape[2]
    return pl.pallas_call(
        functools.partial(_ffn_kernel, final_norm=final_norm),
        grid=(m // tm, dff // tf),
        in_specs=[
            pl.BlockSpec(memory_space=pl.ANY),
            pl.BlockSpec((1, d), lambda i, f: (0, 0)),
            pl.BlockSpec((1, d, tf), lambda i, f: (layer, 0, f)),
            pl.BlockSpec((1, d, tf), lambda i, f: (layer, 0, f)),
            pl.BlockSpec((1, tf, d), lambda i, f: (layer, f, 0)),
            pl.BlockSpec((1, d), lambda i, f: (0, 0)),
        ],
        out_specs=pl.BlockSpec((tm, d), lambda i, f: (i, 0), pipeline_mode=pl.Buffered(1)),
        out_shape=jax.ShapeDtypeStruct((m, d), F32),
        scratch_shapes=[pltpu.VMEM((tm, d), BF16), pltpu.SemaphoreType.DMA],
        compiler_params=_cparams("parallel", "arbitrary"),
        name="ffn_block",
    )(x, gain.reshape(1, d), wg, wu, wd, gain_final.reshape(1, d))


def _mem_kv_kernel(m_ref, g_ref, wk_ref, wv_ref, k_ref, v_ref):
    mn = _rms_rows(m_ref[0], g_ref[...]).astype(BF16)
    k_ref[0] = _dot(mn, wk_ref[0].astype(BF16)).astype(BF16)
    v_ref[0] = _dot(mn, wv_ref[0].astype(BF16)).astype(BF16)


def mem_kv(mem, gain, wk, wv, layer):
    b, ml, d = mem.shape
    xa = wk.shape[2]
    return pl.pallas_call(
        _mem_kv_kernel,
        grid=(b,),
        in_specs=[
            pl.BlockSpec((1, ml, d), lambda i: (i, 0, 0)),
            pl.BlockSpec((1, d), lambda i: (0, 0)),
            pl.BlockSpec((1, d, xa), lambda i: (layer, 0, 0)),
            pl.BlockSpec((1, d, xa), lambda i: (layer, 0, 0)),
        ],
        out_specs=[pl.BlockSpec((1, ml, xa), lambda i: (i, 0, 0))] * 2,
        out_shape=[jax.ShapeDtypeStruct((b, ml, xa), BF16)] * 2,
        compiler_params=_cparams("parallel"),
        name="mem_kv",
    )(mem, gain.reshape(1, d), wk, wv)


def _xattn_kernel(x_ref, g_ref, wq_ref, k_ref, v_ref, wo_ref, o_ref, wqb_ref, wob_ref):
    @pl.when((pl.program_id(0) == 0) & (pl.program_id(1) == 0))
    def _():
        wqb_ref[...] = wq_ref[0].astype(BF16)
        wob_ref[...] = wo_ref[0].astype(BF16)

    x = x_ref[0]
    xn = _rms_rows(x, g_ref[...]).astype(BF16)
    q = (_dot(xn, wqb_ref[...]) * (XA_HEAD_DIM ** -0.5)).astype(BF16)
    outs = []
    for h in range(XA_HEADS):
        sl = slice(h * XA_HEAD_DIM, (h + 1) * XA_HEAD_DIM)
        s = _dot_nt(q[:, sl], k_ref[0, :, sl])
        p = jnp.exp(s - jnp.max(s, axis=-1, keepdims=True))
        den = jnp.sum(p, axis=-1, keepdims=True)
        outs.append(_dot(p.astype(BF16), v_ref[0, :, sl]) / den)
    o = jnp.concatenate(outs, axis=-1).astype(BF16)
    o_ref[0] = x + _dot(o, wob_ref[...])


def xattn_block(x, gain, wq, k, v, wo, layer, tm):
    b, s, d = x.shape
    ml, xa = k.shape[1], k.shape[2]
    return pl.pallas_call(
        _xattn_kernel,
        grid=(b, s // tm),
        in_specs=[
            pl.BlockSpec((1, tm, d), lambda i, t: (i, t, 0)),
            pl.BlockSpec((1, d), lambda i, t: (0, 0)),
            pl.BlockSpec((1, d, xa), lambda i, t: (layer, 0, 0)),
            pl.BlockSpec((1, ml, xa), lambda i, t: (i, 0, 0)),
            pl.BlockSpec((1, ml, xa), lambda i, t: (i, 0, 0)),
            pl.BlockSpec((1, xa, d), lambda i, t: (layer, 0, 0)),
        ],
        out_specs=pl.BlockSpec((1, tm, d), lambda i, t: (i, t, 0)),
        out_shape=jax.ShapeDtypeStruct((b, s, d), F32),
        scratch_shapes=[pltpu.VMEM((d, xa), BF16), pltpu.VMEM((xa, d), BF16)],
        compiler_params=_cparams("arbitrary", "arbitrary"),
        name="xattn_block",
    )(x, gain.reshape(1, d), wq, k, v, wo)


_HGRN_LEVELS = (32, 16, 8, 4, 2, 1)


def _hgrn_masks():
    c = CHUNK
    i = np.arange(c)[:, None]
    j = np.arange(c)[None, :]
    masks = [(i // (2 * s) == j // (2 * s)) & (i % (2 * s) >= s) & (j % (2 * s) < s) for s in _HGRN_LEVELS]
    masks.append(i == j)
    return jnp.asarray(np.stack(masks).astype(np.float32))


def _segment_row(x, seg, idx):
    rows, lanes = x.shape
    x3 = x.reshape(rows // seg, seg, lanes)
    return jnp.broadcast_to(x3[:, idx:idx + 1, :], x3.shape).reshape(rows, lanes)


def _hgrn_kernel(q_ref, f_ref, i_ref, gate_ref, lbl_ref, nw_ref, masks_ref, y_ref, st_ref, *, layer, n_chunks):
    c = CHUNK
    tt = n_chunks * c

    @pl.when(pl.program_id(2) == 0)
    def _():
        st_ref[...] = jnp.zeros_like(st_ref)

    logits = lbl_ref[...]
    ex = jnp.exp(logits - jnp.max(logits, axis=0, keepdims=True))
    sm = ex / jnp.sum(ex, axis=0, keepdims=True)
    lb = jnp.sum(sm[0:layer + 1, :], axis=0, keepdims=True)

    sig_f = jax.nn.sigmoid(f_ref[0])
    q = _silu(q_ref[0])
    log_f = jnp.log(lb + (1.0 - lb) * sig_f)
    k = (1.0 - lb) * (1.0 - sig_f)
    vb = i_ref[0].astype(BF16)

    row = lax.broadcasted_iota(jnp.int32, (tt, HEAD_DIM), 0)
    rc = row & (c - 1)
    g = log_f
    for s in (1, 2, 4, 8, 16, 32):
        g = g + jnp.where(rc >= s, pltpu.roll(g, s, axis=0), 0.0)
    g_last = _segment_row(g, c, c - 1)

    exps = []
    for s in _HGRN_LEVELS:
        if 2 * s >= SUBLANES:
            exps.append(-jnp.abs(g - _segment_row(g, 2 * s, s - 1)))
        elif s == 2:
            r4 = row & 3
            below = pltpu.roll(log_f, 1, axis=0)
            above = pltpu.roll(log_f, tt - 1, axis=0)
            exps.append(jnp.where(r4 == 0, above, jnp.where(r4 == 1, 0.0,
                                                            jnp.where(r4 == 2, log_f, log_f + below))))
        else:
            exps.append(jnp.where((row & 1) == 1, log_f, 0.0))

    def chunks(x):
        xb = x.astype(BF16)
        return [xb[ci * c:(ci + 1) * c] for ci in range(n_chunks)]

    qs, ks = chunks(q), chunks(k)
    n_lv = len(_HGRN_LEVELS)
    on_diag = masks_ref[n_lv] > 0.5
    attn = [jnp.where(on_diag, _dot_nt(qs[ci], ks[ci]), 0.0) for ci in range(n_chunks)]
    for lv in range(n_lv):
        sc = jnp.exp(exps[lv])
        ql, kl = chunks(q * sc), chunks(k * sc)
        m = masks_ref[lv] > 0.5
        attn = [jnp.where(m, _dot_nt(ql[ci], kl[ci]), attn[ci]) for ci in range(n_chunks)]

    vs = chunks(vb)
    qg = chunks(q * jnp.exp(g))
    kd = chunks(k * jnp.exp(g_last - g))
    upd = [_dot_tn(vs[ci], kd[ci]) for ci in range(n_chunks)]
    intra = [_dot(attn[ci].astype(BF16), vs[ci]) for ci in range(n_chunks)]
    dec = jnp.exp(g_last)
    st = st_ref[...]
    outs = []
    for ci in range(n_chunks):
        outs.append(intra[ci] + _dot_nt(qg[ci], st.astype(BF16)))
        st = st * dec[ci * c:ci * c + 1, :] + upd[ci]
    st_ref[...] = st
    o = jnp.concatenate(outs, axis=0)
    y = o * lax.rsqrt(jnp.mean(o * o, axis=-1, keepdims=True) + NORM_EPS)
    y_ref[0] = (y * nw_ref[...] * _silu(gate_ref[0])).astype(y_ref.dtype)


def hgrn_heads(proj, lb_logits, norm_w, layer, col0, tt):
    b, s, _ = proj.shape
    nl = lb_logits.shape[0]
    masks = _hgrn_masks()
    blk = lambda o: pl.BlockSpec((1, tt, HEAD_DIM), lambda i, h, t: (i, t, col0 + o + h))
    return pl.pallas_call(
        functools.partial(_hgrn_kernel, layer=layer, n_chunks=tt // CHUNK),
        grid=(b, N_HEADS, s // tt),
        in_specs=[
            blk(0), blk(N_HEADS), blk(2 * N_HEADS), blk(3 * N_HEADS),
            pl.BlockSpec((nl, HEAD_DIM), lambda i, h, t: (0, h)),
            pl.BlockSpec((1, HEAD_DIM), lambda i, h, t: (0, h)),
            pl.BlockSpec(masks.shape, lambda i, h, t: (0, 0, 0)),
        ],
        out_specs=pl.BlockSpec((1, tt, HEAD_DIM), lambda i, h, t: (i, t, h)),
        out_shape=jax.ShapeDtypeStruct((b, s, N_HEADS * HEAD_DIM), BF16),
        scratch_shapes=[pltpu.VMEM((HEAD_DIM, HEAD_DIM), F32)],
        compiler_params=_cparams("parallel", "parallel", "arbitrary"),
        name="hgrn_heads",
    )(proj, proj, proj, proj, lb_logits, norm_w.reshape(1, -1), masks)


def _gdn_kernel(q_ref, k_ref, v_ref, gate_ref, gc_ref, gr_ref, cw_ref, alane_ref, blane_ref, acol_ref, bcol_ref,
                nw_ref, y_ref, s_ref, xq_ref, xk_ref, xv_ref, *, tt):
    c = CHUNK
    hd = HEAD_DIM
    pair = 2 * c
    n_pairs = tt // pair
    n_chunks = tt // c
    width = N_HEADS * hd

    @pl.when(pl.program_id(1) == 0)
    def _():
        s_ref[...] = jnp.zeros_like(s_ref)
        for r in (xq_ref, xk_ref, xv_ref):
            r[...] = jnp.zeros_like(r)

    q_all = _silu(_causal_conv(xq_ref, q_ref[0], cw_ref[:, 0:width]))
    k_all = _silu(_causal_conv(xk_ref, k_ref[0], cw_ref[:, width:2 * width]))
    v_all = _silu(_causal_conv(xv_ref, v_ref[0], cw_ref[:, 2 * width:3 * width]))

    gates = gc_ref[0]
    beta_all = jax.nn.sigmoid(gates)
    gam_all = -jnp.exp(alane_ref[...]) * _softplus(gates + blane_ref[...])
    row = lax.broadcasted_iota(jnp.int32, (tt, LANES), 0)
    rc = row & (c - 1)
    for s in (1, 2, 4, 8, 16, 32):
        gam_all = gam_all + jnp.where(rc >= s, pltpu.roll(gam_all, s, axis=0), 0.0)
    glast_all = _segment_row(gam_all, c, c - 1)
    gam_rows = -jnp.exp(acol_ref[...]) * _softplus(gr_ref[0] + bcol_ref[...])
    lc = lax.broadcasted_iota(jnp.int32, gam_rows.shape, 1) & (c - 1)
    for s in (1, 2, 4, 8, 16, 32):
        gam_rows = gam_rows + jnp.where(lc >= s, pltpu.roll(gam_rows, s, axis=1), 0.0)

    ii = lax.broadcasted_iota(jnp.int32, (c, LANES), 0)
    ll = lax.broadcasted_iota(jnp.int32, (c, LANES), 1)
    jj = ll & (c - 1)
    left = ll < c
    lower = ii >= jj
    strict = ii > jj
    eye2 = jnp.where(ii == jj, 1.0, 0.0)

    def block_diag(m):
        zero = jnp.zeros_like(m)
        return jnp.concatenate([jnp.where(left, m, zero), jnp.where(left, zero, m)], axis=0)

    def split2(x):
        hi = x.astype(BF16)
        return hi, (x - hi.astype(F32)).astype(BF16)

    def times_block_diag(lhs, p_hi, p_lo):
        l_hi, l_lo = split2(lhs)
        return _dot(jnp.concatenate([l_hi, l_lo, l_hi], axis=1), jnp.concatenate([p_hi, p_hi, p_lo], axis=0))

    units = [(h, p) for h in range(N_HEADS) for p in range(n_pairs)]
    qd, kd, rhs, khb, kbb, qhb, gam_h = [], [], [], [], [], [], []
    for h in range(N_HEADS):
        lanes = slice(h * hd, (h + 1) * hd)
        qh, kh, vh = q_all[:, lanes], k_all[:, lanes], v_all[:, lanes]
        qh = qh * lax.rsqrt(jnp.sum(qh * qh, axis=-1, keepdims=True) + L2_EPS) * (hd ** -0.5)
        kh = kh * lax.rsqrt(jnp.sum(kh * kh, axis=-1, keepdims=True) + L2_EPS)
        beta = beta_all[:, h:h + 1]
        gam = gam_all[:, N_HEADS + h:N_HEADS + h + 1]
        eg = jnp.exp(gam)
        kb = kh * beta
        gam_h.append(gam)
        rhs.append(jnp.concatenate([kb * eg, vh * beta], axis=1).astype(BF16))
        qd.append((qh * eg).astype(BF16))
        kd.append((kh * jnp.exp(glast_all[:, N_HEADS + h:N_HEADS + h + 1] - gam)).astype(BF16))
        khb.append(kh.astype(BF16))
        kbb.append(kb.astype(BF16))
        qhb.append(qh.astype(BF16))

    x_mats, qk_tiles = [], [[] for _ in range(N_HEADS)]
    for h, p in units:
        r0 = p * pair
        rows = slice(r0, r0 + pair)
        res = _dot_nt(jnp.concatenate([kbb[h][rows], qhb[h][rows]], axis=0), khb[h][rows])
        col = jnp.where(left, jnp.broadcast_to(gam_h[h][r0:r0 + c], (c, LANES)),
                        jnp.broadcast_to(gam_h[h][r0 + c:r0 + pair], (c, LANES)))
        decay = jnp.exp(jnp.minimum(col - gam_rows[N_HEADS + h:N_HEADS + h + 1, rows], 0.0))
        x_mats.append(jnp.where(strict, jnp.where(left, res[0:c], res[c:pair]) * -decay, 0.0))
        qk_tiles[h].append(jnp.where(lower & left, res[pair:pair + c] * decay, 0.0).astype(BF16))
        qk_tiles[h].append(jnp.where(lower & (~left), res[pair + c:2 * pair] * decay, 0.0).astype(BF16))

    t_mats = [eye2 + x for x in x_mats]
    powers = []
    for x in x_mats:
        p_hi, p_lo = split2(x)
        powers.append(times_block_diag(x, block_diag(p_hi), block_diag(p_lo)))
    for it in range(5):
        for u in range(len(units)):
            p_hi, p_lo = split2(powers[u])
            bd_hi, bd_lo = block_diag(p_hi), block_diag(p_lo)
            if it < 4:
                both = times_block_diag(jnp.concatenate([powers[u], t_mats[u]], axis=0), bd_hi, bd_lo)
                powers[u], t_mats[u] = both[0:c], t_mats[u] + both[c:pair]
            else:
                t_mats[u] = t_mats[u] + times_block_diag(t_mats[u], bd_hi, bd_lo)
    wu = [[] for _ in range(N_HEADS)]
    for u, (h, p) in enumerate(units):
        wu[h].append(_dot(block_diag(t_mats[u].astype(BF16)), rhs[h][p * pair:(p + 1) * pair]))

    outs = [[] for _ in range(N_HEADS)]
    zeros_half = jnp.zeros((c, hd), BF16)
    heads = range(N_HEADS)
    states = [s_ref[h] for h in heads]
    chunk_decay = jnp.exp(glast_all)
    for ci in range(n_chunks):
        rows = slice(ci * c, (ci + 1) * c)
        half = slice((ci % 2) * c, (ci % 2 + 1) * c)
        stb = [states[h].astype(BF16) for h in heads]
        wu_c = [wu[h][ci // 2][half] for h in heads]
        w_s = [_dot(wu_c[h][:, :hd].astype(BF16), stb[h]) for h in heads]
        vnb = [(wu_c[h][:, hd:] - w_s[h]).astype(BF16) for h in heads]
        for h in heads:
            v2 = jnp.concatenate([vnb[h], zeros_half] if ci % 2 == 0 else [zeros_half, vnb[h]], axis=0)
            outs[h].append(_dot(qd[h][rows], stb[h]) + _dot(qk_tiles[h][ci], v2))
        states = [states[h] * chunk_decay[ci * c:ci * c + 1, N_HEADS + h:N_HEADS + h + 1]
                  + _dot_tn(kd[h][rows], vnb[h]) for h in heads]
    for h in heads:
        s_ref[h] = states[h]

    for h in range(N_HEADS):
        lanes = slice(h * hd, (h + 1) * hd)
        o = jnp.concatenate(outs[h], axis=0)
        y = o * lax.rsqrt(jnp.mean(o * o, axis=-1, keepdims=True) + NORM_EPS)
        y_ref[0, :, lanes] = (y * nw_ref[:, lanes] * _silu(gate_ref[0, :, lanes])).astype(y_ref.dtype)


def gdn_heads(proj, gates_col, gates_row, conv_w, a_log, dt_bias, norm_w, col0, tt):
    b, s, _ = proj.shape
    width = N_HEADS * HEAD_DIM
    blk = lambda o: pl.BlockSpec((1, tt, width), lambda i, t: (i, t, col0 + o))
    whole = lambda shape: pl.BlockSpec(shape, lambda i, t: (0,) * len(shape))
    lane_vec = jnp.zeros((1, LANES), F32).at[0, N_HEADS:2 * N_HEADS]
    col_vec = jnp.zeros((2 * N_HEADS, 1), F32).at[N_HEADS:, 0]
    return pl.pallas_call(
        functools.partial(_gdn_kernel, tt=tt),
        grid=(b, s // tt),
        in_specs=[
            blk(0), blk(1), blk(2), blk(3),
            pl.BlockSpec((1, tt, LANES), lambda i, t: (i, t, 0)),
            pl.BlockSpec((1, 2 * N_HEADS, tt), lambda i, t: (i, 0, t)),
            whole(conv_w.shape),
            whole((1, LANES)), whole((1, LANES)), whole((2 * N_HEADS, 1)), whole((2 * N_HEADS, 1)),
            whole((1, width)),
        ],
        out_specs=pl.BlockSpec((1, tt, width), lambda i, t: (i, t, 0)),
        out_shape=jax.ShapeDtypeStruct((b, s, width), BF16),
        scratch_shapes=[
            pltpu.VMEM((N_HEADS, HEAD_DIM, HEAD_DIM), F32),
            pltpu.VMEM((SUBLANES, width), F32),
            pltpu.VMEM((SUBLANES, width), F32),
            pltpu.VMEM((SUBLANES, width), F32),
        ],
        compiler_params=_cparams("parallel", "arbitrary"),
        name="gdn_heads",
    )(proj, proj, proj, proj, gates_col, gates_row, conv_w,
      lane_vec.set(a_log), lane_vec.set(dt_bias), col_vec.set(a_log), col_vec.set(dt_bias),
      norm_w.reshape(1, -1))


SSD_PAIRS = SSD_HEADS_PER_GROUP // 2


def _ssd_expand_table():
    expand = np.zeros((SSD_GROUPS, LANES, SSD_GROUP_WIDTH), np.float32)
    for g in range(SSD_GROUPS):
        for hh in range(SSD_HEADS_PER_GROUP):
            expand[g, g * SSD_HEADS_PER_GROUP + hh, hh * SSD_HEADDIM:(hh + 1) * SSD_HEADDIM] = 1.0
    return jnp.asarray(np.concatenate([expand] * 3, axis=1), BF16)


def _ssd_kernel(xfirst_ref, xnext_ref, wz_ref, wx_ref, wb_ref, wc_ref, dtc_ref, dtp_ref, cwx_ref, cwb_ref, cwc_ref,
                cbx_ref, cbb_ref, cbc_ref, blane_ref, alane_ref, bpair_ref, apair_ref, de_ref, exp_ref,
                y_ref, ht_ref, xx_ref, xb_ref, xc_ref, wbf_ref, proj_ref, pnext_ref, *, tt):
    c = CHUNK
    n_chunks = tt // c
    hp = SSD_HEADDIM
    gw = SSD_GROUP_WIDTH

    @pl.when((pl.program_id(1) == 0) & (pl.program_id(2) == 0))
    def _():
        wbf_ref[:, 0:gw] = wz_ref[0].T.astype(BF16)
        wbf_ref[:, gw:2 * gw] = wx_ref[0].T.astype(BF16)
        wbf_ref[:, 2 * gw:2 * gw + LANES] = wb_ref[0].T.astype(BF16)
        wbf_ref[:, 2 * gw + LANES:2 * gw + 2 * LANES] = wc_ref[0].T.astype(BF16)
        proj_ref[...] = _dot(xfirst_ref[...], wbf_ref[...])

    @pl.when(pl.program_id(2) == 0)
    def _():
        for r in (ht_ref, xx_ref, xb_ref, xc_ref):
            r[...] = jnp.zeros_like(r)

    piece_w = PROJ_PIECE
    pieces = [slice(p0, p0 + piece_w) for p0 in range(0, wbf_ref.shape[1], piece_w)]

    def project_piece():
        if pieces:
            cols = pieces.pop(0)
            pnext_ref[:, cols] = _dot(xnext_ref[...], wbf_ref[:, cols])

    project_piece()
    xs_tiles = []
    for lo in range(0, gw, PROJ_PIECE):
        lanes = slice(lo, lo + PROJ_PIECE)
        xs_tiles.append(_silu(_causal_conv(xx_ref.at[:, lanes], proj_ref[:, gw + lo:gw + lo + PROJ_PIECE],
                                           cwx_ref[:, lanes]) + cbx_ref[:, lanes]))
        project_piece()
    xs = jnp.concatenate(xs_tiles, axis=1)
    bm = _silu(_causal_conv(xb_ref, proj_ref[:, 2 * gw:2 * gw + LANES], cwb_ref[...])
               + cbb_ref[...]).astype(BF16)
    cm = _silu(_causal_conv(xc_ref, proj_ref[:, 2 * gw + LANES:2 * gw + 2 * LANES], cwc_ref[...])
               + cbc_ref[...]).astype(BF16)

    dt_c = _softplus(dtc_ref[0] + blane_ref[...])
    acum_c = dt_c * -jnp.exp(alane_ref[...])
    rc = lax.broadcasted_iota(jnp.int32, (tt, LANES), 0) & (c - 1)
    for s in (1, 2, 4, 8, 16, 32):
        acum_c = acum_c + jnp.where(rc >= s, pltpu.roll(acum_c, s, axis=0), 0.0)
    s_c = dt_c * jnp.exp(_segment_row(acum_c, c, c - 1) - acum_c)
    a_hi, a_mid, a_lo = _split3(acum_c)
    acum_e = _dot(jnp.concatenate([a_hi, a_mid, a_lo], axis=1), exp_ref[0])
    s_hi, s_mid, _ = _split3(s_c)
    s_e = _dot(jnp.concatenate([s_hi, s_mid], axis=1), exp_ref[0, 0:2 * LANES, :])
    decay_e = jnp.exp(acum_e)
    xw = (xs * s_e).astype(BF16)
    xsb = xs.astype(BF16)

    project_piece()
    dt_p = _softplus(dtp_ref[0, 0] + jnp.concatenate([bpair_ref[0]] * n_chunks, axis=1))
    acum_p = dt_p * -jnp.exp(jnp.concatenate([apair_ref[0]] * n_chunks, axis=1))
    lc = lax.broadcasted_iota(jnp.int32, acum_p.shape, 1) & (c - 1)
    for s in (1, 2, 4, 8, 16, 32):
        acum_p = acum_p + jnp.where(lc >= s, pltpu.roll(acum_p, s, axis=1), 0.0)

    ii = lax.broadcasted_iota(jnp.int32, (c, LANES), 0)
    ll = lax.broadcasted_iota(jnp.int32, (c, LANES), 1)
    lower2 = ii >= (ll & (c - 1))
    left = ll < hp
    while pieces:
        project_piece()

    chunk_rows = [slice(ci * c, (ci + 1) * c) for ci in range(n_chunks)]
    cb2 = [_dot_nt(cm[r], jnp.concatenate([bm[r], bm[r]], axis=0)) for r in chunk_rows]
    upd = [_dot_tn(bm[r], xw[r]) for r in chunk_rows]
    y_diag = []
    for ci, r in enumerate(chunk_rows):
        tiles = []
        for pr in range(SSD_PAIRS):
            lanes = slice(2 * pr * hp, (2 * pr + 2) * hp)
            tok = slice(ci * LANES, (ci + 1) * LANES)
            lmat = jnp.where(lower2, jnp.exp(jnp.minimum(acum_e[r, lanes] - acum_p[pr:pr + 1, tok], 0.0)), 0.0)
            xp = xsb[r, lanes]
            zero = jnp.zeros_like(xp)
            rhs = jnp.concatenate([jnp.where(left, xp, zero), jnp.where(left, zero, xp)], axis=0)
            tiles.append(_dot((cb2[ci] * lmat * dt_p[pr:pr + 1, tok]).astype(BF16), rhs))
        y_diag.append(jnp.concatenate(tiles, axis=1))
    ht = ht_ref[...]
    y_off = []
    for ci, r in enumerate(chunk_rows):
        y_off.append(_dot(cm[r], ht.astype(BF16)))
        ht = ht * decay_e[ci * c + c - 1:ci * c + c, :] + upd[ci]
    ht_ref[...] = ht
    y = jnp.concatenate(y_diag, axis=0) + jnp.concatenate(y_off, axis=0) * decay_e + de_ref[...] * xs
    y_ref[0] = y * _silu(proj_ref[:, 0:gw])
    proj_ref[...] = pnext_ref[...]


def ssd_groups(xn, w_t, layer, dt_col, dt_pairs, conv_w, conv_b, dt_bias, a_log, d_skip, tt):
    b, s, _ = dt_col.shape
    k = xn.shape[1]
    n_t = s // tt
    g_w = SSD_GROUP_WIDTH
    n_xblk = SSD_GROUPS
    expand = _ssd_expand_table()
    rep = lambda p: jnp.repeat(p.astype(F32), SSD_HEADDIM).reshape(1, -1)
    row2 = lambda v: v.reshape(1, -1)
    lane_vec = lambda p: jnp.pad(p.astype(F32), (0, LANES - p.shape[0])).reshape(1, LANES)

    def pair_tile(p):
        t = jnp.repeat(p.astype(F32).reshape(SSD_GROUPS, SSD_PAIRS, 2, 1), CHUNK, axis=-1)
        t = t.reshape(SSD_GROUPS, SSD_PAIRS, LANES)
        return jnp.pad(t, ((0, 0), (0, SUBLANES - SSD_PAIRS), (0, 0)))
    x_off = n_xblk
    b_off = 2 * n_xblk * (g_w // LANES)
    c_off = b_off + SSD_GROUPS
    p_w = 2 * g_w + 2 * LANES
    w_wide = lambda o: pl.BlockSpec((1, g_w, k), lambda g, i, t: (layer, o + g, 0))
    w_narrow = lambda o: pl.BlockSpec((1, LANES, k), lambda g, i, t: (layer, o + g, 0))
    return pl.pallas_call(
        functools.partial(_ssd_kernel, tt=tt),
        grid=(SSD_GROUPS, b, n_t),
        in_specs=[
            pl.BlockSpec((tt, k), lambda g, i, t: (0, 0)),
            pl.BlockSpec((tt, k), lambda g, i, t: ((i * n_t + t + 1) % (b * n_t), 0)),
            w_wide(0), w_wide(x_off), w_narrow(b_off), w_narrow(c_off),
            pl.BlockSpec((1, tt, LANES), lambda g, i, t: (i, t, 0)),
            pl.BlockSpec((1, 1, SUBLANES, 2 * tt), lambda g, i, t: (i, g, 0, t)),
            pl.BlockSpec((CONV_K, g_w), lambda g, i, t: (0, g)),
            pl.BlockSpec((CONV_K, LANES), lambda g, i, t: (0, n_xblk * (g_w // LANES) + g)),
            pl.BlockSpec((CONV_K, LANES), lambda g, i, t: (0, n_xblk * (g_w // LANES) + SSD_GROUPS + g)),
            pl.BlockSpec((1, g_w), lambda g, i, t: (0, g)),
            pl.BlockSpec((1, LANES), lambda g, i, t: (0, n_xblk * (g_w // LANES) + g)),
            pl.BlockSpec((1, LANES), lambda g, i, t: (0, n_xblk * (g_w // LANES) + SSD_GROUPS + g)),
            pl.BlockSpec((1, LANES), lambda g, i, t: (0, 0)),
            pl.BlockSpec((1, LANES), lambda g, i, t: (0, 0)),
            pl.BlockSpec((1, SUBLANES, LANES), lambda g, i, t: (g, 0, 0)),
            pl.BlockSpec((1, SUBLANES, LANES), lambda g, i, t: (g, 0, 0)),
            pl.BlockSpec((1, g_w), lambda g, i, t: (0, g)),
            pl.BlockSpec((1, 3 * LANES, g_w), lambda g, i, t: (g, 0, 0)),
        ],
        out_specs=pl.BlockSpec((1, tt, g_w), lambda g, i, t: (i, t, g)),
        out_shape=jax.ShapeDtypeStruct((b, s, SSD_GROUPS * g_w), F32),
        scratch_shapes=[
            pltpu.VMEM((SSD_DSTATE, g_w), F32),
            pltpu.VMEM((SUBLANES, g_w), F32),
            pltpu.VMEM((SUBLANES, LANES), F32),
            pltpu.VMEM((SUBLANES, LANES), F32),
            pltpu.VMEM((k, p_w), BF16),
            pltpu.VMEM((tt, p_w), F32),
            pltpu.VMEM((tt, p_w), F32),
        ],
        compiler_params=_cparams("arbitrary", "arbitrary", "arbitrary"),
        name="ssd_groups",
    )(xn, xn, w_t, w_t, w_t, w_t, dt_col, dt_pairs, conv_w, conv_w, conv_w,
      row2(conv_b), row2(conv_b), row2(conv_b), lane_vec(dt_bias), lane_vec(a_log),
      pair_tile(dt_bias), pair_tile(a_log), rep(d_skip), expand)


def _tail_weight(w_t, layer, n_main):
    tail = w_t[layer, n_main:, :]
    return jnp.pad(tail, ((0, LANES - tail.shape[0]), (0, 0)))[None]


def _tile(n, pref):
    t = min(n, pref)
    while n % t:
        t //= 2
    return t


def kernel(x, mem, norm_mix, norm_xattn, norm_mem, norm_ffn, norm_final, hy_w_in, hgrn_lb_logits, hgrn_norm,
           gdn_conv_w, gdn_a_log, gdn_dt_bias, gdn_norm, hy_w_out, ssd_w_in, ssd_conv_w, ssd_conv_b,
           ssd_dt_bias, ssd_a_log, ssd_d, ssd_norm, ssd_w_out, xa_wq, xa_wk, xa_wv, xa_wo, ffn_w_gate,
           ffn_w_up, ffn_w_down):
    bsz, seq, d = x.shape
    tok = bsz * seq
    depth = norm_mix.shape[0]
    tm = _tile(tok, TOKEN_TILE)
    tt = _tile(seq, SEQ_TILE)
    hy_w_in_t = jnp.swapaxes(hy_w_in, 1, 2)
    ssd_w_in_t = jnp.swapaxes(ssd_w_in, 1, 2)

    h = x.reshape(tok, d)
    for layer in range(depth):
        if layer % 2 == 0:
            e = layer // 2
            n_main = 8 * N_HEADS * HEAD_DIM
            xn, tail = norm_cast(h, norm_mix[layer], _tail_weight(hy_w_in_t, e, n_main), tm)
            proj = ws_matmul([xn], hy_w_in_t, e, n_main, tm, _tile(n_main, COLUMN_TILE),
                             w_transposed=True).reshape(bsz, seq, n_main)
            gates_col = tail.reshape(bsz, seq, LANES)
            gates_row = jnp.swapaxes(gates_col[:, :, :2 * N_HEADS], 1, 2)
            y_a = hgrn_heads(proj, hgrn_lb_logits.astype(F32), hgrn_norm[e], e, 0, tt)
            y_b = gdn_heads(proj, gates_col, gates_row, gdn_conv_w[e].astype(F32), gdn_a_log[e].astype(F32),
                            gdn_dt_bias[e].astype(F32), gdn_norm[e], 4, _tile(seq, GDN_SEQ_TILE))
            h = ws_matmul([y_a.reshape(tok, -1), y_b.reshape(tok, -1)], hy_w_out, e, d, _tile(tok, TOKEN_TILE // 2),
                          d, resid=h, single_buffer_w=True)
        else:
            o = layer // 2
            n_heads = ssd_dt_bias.shape[1]
            n_main = ssd_w_in.shape[2] - n_heads
            xn, tail = norm_cast(h, norm_mix[layer], _tail_weight(ssd_w_in_t, o, n_main), tm)
            dt_col = tail.reshape(bsz, seq, LANES)
            dt_pairs = dt_col[:, :, :n_heads].reshape(bsz, seq // CHUNK, CHUNK, SSD_GROUPS, SSD_PAIRS, 2)
            dt_pairs = dt_pairs.transpose(0, 3, 4, 1, 5, 2).reshape(bsz, SSD_GROUPS, SSD_PAIRS, 2 * seq)
            dt_pairs = jnp.pad(dt_pairs, ((0, 0), (0, 0), (0, SUBLANES - SSD_PAIRS), (0, 0)))
            y = ssd_groups(xn, ssd_w_in_t, o, dt_col, dt_pairs, ssd_conv_w[o].astype(F32), ssd_conv_b[o].astype(F32),
                           ssd_dt_bias[o], ssd_a_log[o], ssd_d[o], tt)
            h = ws_matmul([y.reshape(tok, -1)], ssd_w_out, o, d, _tile(tok, TOKEN_TILE // 2),
                          _tile(d, COLUMN_TILE), gain=ssd_norm[o], resid=h, single_buffer_w=True)
        k_mem, v_mem = mem_kv(mem, norm_mem[layer], xa_wk, xa_wv, layer)
        h = xattn_block(h.reshape(bsz, seq, d), norm_xattn[layer], xa_wq, k_mem, v_mem, xa_wo, layer,
                        _tile(seq, 512)).reshape(tok, d)
        h = ffn_block(h, norm_ffn[layer], ffn_w_gate, ffn_w_up, ffn_w_down, layer, norm_final,
                      _tile(tok, FFN_TOKEN_TILE), _tile(ffn_w_gate.shape[2], FFN_TILE), layer == depth - 1)
    return h.reshape(bsz, seq, d)
```

```python
import functools

import numpy as np
import jax
import jax.numpy as jnp
from jax import lax
from jax.experimental import pallas as pl
from jax.experimental.pallas import tpu as pltpu

F32 = jnp.float32
BF16 = jnp.bfloat16

NORM_EPS = 1e-6
L2_EPS = 1e-6
CHUNK = 64
CONV_K = 4
LANES = 128
SUBLANES = 8
HEAD_DIM = 128
N_HEADS = 8
SSD_HEADDIM = 64
SSD_GROUPS = 8
SSD_HEADS_PER_GROUP = 8
SSD_GROUP_WIDTH = SSD_HEADDIM * SSD_HEADS_PER_GROUP
SSD_DSTATE = 128
XA_HEADS = 4
XA_HEAD_DIM = 128
VMEM_LIMIT_BYTES = 56 * 1024 * 1024
TOKEN_TILE = 1024
COLUMN_TILE = 1024
PROJ_PIECE = 256
FFN_TILE = 256
FFN_TOKEN_TILE = 2048
SEQ_TILE = 512
OUT_PROJ_TOKEN_TILE = TOKEN_TILE // 2
GDN_SEQ_TILE = 256


def _cparams(*sem):
    return pltpu.CompilerParams(dimension_semantics=sem, vmem_limit_bytes=VMEM_LIMIT_BYTES)


def _dot(a, b):
    return jnp.dot(a, b, preferred_element_type=F32)


def _dot_nt(a, b):
    return lax.dot_general(a, b, (((1,), (1,)), ((), ())), preferred_element_type=F32)


def _dot_tn(a, b):
    return lax.dot_general(a, b, (((0,), (0,)), ((), ())), preferred_element_type=F32)


def _split3(x):
    hi = x.astype(BF16)
    r1 = x - hi.astype(F32)
    mid = r1.astype(BF16)
    lo = (r1 - mid.astype(F32)).astype(BF16)
    return hi, mid, lo


def _silu(x):
    return x * jax.nn.sigmoid(x)


def _softplus(x):
    return jnp.maximum(x, 0.0) + jnp.log(1.0 + jnp.exp(-jnp.abs(x)))


def _rms_rows(x, gain):
    ms = jnp.mean(x * x, axis=-1, keepdims=True)
    return x * lax.rsqrt(ms + NORM_EPS) * gain


def _causal_conv(carry_ref, x, w):
    rows = x.shape[0]
    ext = jnp.concatenate([carry_ref[...], x], axis=0)
    y = w[CONV_K - 1:CONV_K, :] * x
    for k in range(CONV_K - 1):
        y = y + w[k:k + 1, :] * pltpu.roll(ext, CONV_K - 1 - k, axis=0)[SUBLANES:]
    carry_ref[...] = x[rows - SUBLANES:]
    return y


def _norm_cast_kernel(x_ref, g_ref, wt_ref, o_ref, tail_ref):
    xn = _rms_rows(x_ref[...], g_ref[...]).astype(BF16)
    o_ref[...] = xn
    tail_ref[...] = _dot_nt(xn, wt_ref[0].astype(BF16))


def norm_cast(x, gain, w_tail_t, tm):
    m, k = x.shape
    return pl.pallas_call(
        _norm_cast_kernel,
        grid=(m // tm,),
        in_specs=[pl.BlockSpec((tm, k), lambda i: (i, 0)), pl.BlockSpec((1, k), lambda i: (0, 0)),
                  pl.BlockSpec((1, LANES, k), lambda i: (0, 0, 0))],
        out_specs=[pl.BlockSpec((tm, k), lambda i: (i, 0)), pl.BlockSpec((tm, LANES), lambda i: (i, 0))],
        out_shape=[jax.ShapeDtypeStruct((m, k), BF16), jax.ShapeDtypeStruct((m, LANES), F32)],
        compiler_params=_cparams("parallel"),
        name="norm_cast",
    )(x, gain.reshape(1, k), w_tail_t)


def _ws_matmul_kernel(*refs, n_a, normalize, residual, w_transposed):
    a_refs = refs[:n_a]
    rest = list(refs[n_a:])
    g_ref = rest.pop(0) if normalize else None
    w_ref = rest.pop(0)
    r_ref = rest.pop(0) if residual else None
    o_ref, wb_ref = rest

    @pl.when(pl.program_id(1) == 0)
    def _():
        w = w_ref[0]
        wb_ref[...] = (w.T if w_transposed else w).astype(BF16)

    acc = None
    k0 = 0
    for a_ref in a_refs:
        a = a_ref[...]
        if normalize:
            a = _rms_rows(a, g_ref[...]).astype(BF16)
        part = _dot(a, wb_ref[k0:k0 + a.shape[1], :])
        k0 += a.shape[1]
        acc = part if acc is None else acc + part
    o_ref[...] = r_ref[...] + acc if residual else acc


def ws_matmul(a_list, w, layer, n, tm, tn, gain=None, resid=None, single_buffer_w=False, w_transposed=False):
    m = a_list[0].shape[0]
    k = w.shape[2] if w_transposed else w.shape[1]
    normalize = gain is not None
    residual = resid is not None
    w_mode = dict(pipeline_mode=pl.Buffered(1)) if single_buffer_w else {}
    in_specs = [pl.BlockSpec((tm, a.shape[1]), lambda j, i: (i, 0)) for a in a_list]
    args = list(a_list)
    if normalize:
        in_specs.append(pl.BlockSpec((1, k), lambda j, i: (0, 0)))
        args.append(gain.reshape(1, k))
    if w_transposed:
        in_specs.append(pl.BlockSpec((1, tn, k), lambda j, i: (layer, j, 0), **w_mode))
    else:
        in_specs.append(pl.BlockSpec((1, k, tn), lambda j, i: (layer, 0, j), **w_mode))
    args.append(w)
    if residual:
        in_specs.append(pl.BlockSpec((tm, tn), lambda j, i: (i, j)))
        args.append(resid)
    return pl.pallas_call(
        functools.partial(_ws_matmul_kernel, n_a=len(a_list), normalize=normalize, residual=residual,
                          w_transposed=w_transposed),
        grid=(n // tn, m // tm),
        in_specs=in_specs,
        out_specs=pl.BlockSpec((tm, tn), lambda j, i: (i, j)),
        out_shape=jax.ShapeDtypeStruct((m, n), F32),
        scratch_shapes=[pltpu.VMEM((k, tn), BF16)],
        compiler_params=_cparams("arbitrary", "arbitrary"),
        name="ws_matmul",
    )(*args)


def _ffn_kernel(x_hbm, g_ref, wg_ref, wu_ref, wd_ref, gf_ref, o_ref, xn_ref, sem, *, final_norm):
    f = pl.program_id(1)

    @pl.when(f == 0)
    def _():
        tm = o_ref.shape[0]
        rows = pl.ds(pl.multiple_of(pl.program_id(0) * tm, tm), tm)
        copy = pltpu.make_async_copy(x_hbm.at[rows, :], o_ref, sem)
        copy.start()
        copy.wait()
        xn_ref[...] = _rms_rows(o_ref[...], g_ref[...]).astype(BF16)

    xn = xn_ref[...]
    act = (_silu(_dot(xn, wg_ref[0].astype(BF16))) * _dot(xn, wu_ref[0].astype(BF16))).astype(BF16)
    o_ref[...] += _dot(act, wd_ref[0].astype(BF16))

    if final_norm:
        @pl.when(f == pl.num_programs(1) - 1)
        def _():
            o_ref[...] = _rms_rows(o_ref[...], gf_ref[...])


def ffn_block(x, gain, wg, wu, wd, layer, gain_final, tm, tf, final_norm):
    m, d = x.shape
    dff = wg.shape[2]
    return pl.pallas_call(
        functools.partial(_ffn_kernel, final_norm=final_norm),
        grid=(m // tm, dff // tf),
        in_specs=[
            pl.BlockSpec(memory_space=pl.ANY),
            pl.BlockSpec((1, d), lambda i, f: (0, 0)),
            pl.BlockSpec((1, d, tf), lambda i, f: (layer, 0, f)),
            pl.BlockSpec((1, d, tf), lambda i, f: (layer, 0, f)),
            pl.BlockSpec((1, tf, d), lambda i, f: (layer, f, 0)),
            pl.BlockSpec((1, d), lambda i, f: (0, 0)),
        ],
        out_specs=pl.BlockSpec((tm, d), lambda i, f: (i, 0), pipeline_mode=pl.Buffered(1)),
        out_shape=jax.ShapeDtypeStruct((m, d), F32),
        scratch_shapes=[pltpu.VMEM((tm, d), BF16), pltpu.SemaphoreType.DMA],
        compiler_params=_cparams("parallel", "arbitrary"),
        name="ffn_block",
    )(x, gain.reshape(1, d), wg, wu, wd, gain_final.reshape(1, d))


def _mem_kv_kernel(m_ref, g_ref, wk_ref, wv_ref, k_ref, v_ref):
    mn = _rms_rows(m_ref[0], g_ref[...]).astype(BF16)
    k_ref[0] = _dot(mn, wk_ref[0].astype(BF16)).astype(BF16)
    v_ref[0] = _dot(mn, wv_ref[0].astype(BF16)).astype(BF16)


def mem_kv(mem, gain, wk, wv, layer):
    b, ml, d = mem.shape
    xa = wk.shape[2]
    return pl.pallas_call(
        _mem_kv_kernel,
        grid=(b,),
        in_specs=[
            pl.BlockSpec((1, ml, d), lambda i: (i, 0, 0)),
            pl.BlockSpec((1, d), lambda i: (0, 0)),
            pl.BlockSpec((1, d, xa), lambda i: (layer, 0, 0)),
            pl.BlockSpec((1, d, xa), lambda i: (layer, 0, 0)),
        ],
        out_specs=[pl.BlockSpec((1, ml, xa), lambda i: (i, 0, 0))] * 2,
        out_shape=[jax.ShapeDtypeStruct((b, ml, xa), BF16)] * 2,
        compiler_params=_cparams("parallel"),
        name="mem_kv",
    )(mem, gain.reshape(1, d), wk, wv)


def _xattn_kernel(x_ref, g_ref, wq_ref, k_ref, v_ref, wo_ref, o_ref, wqb_ref, wob_ref):
    @pl.when((pl.program_id(0) == 0) & (pl.program_id(1) == 0))
    def _():
        wqb_ref[...] = wq_ref[0].astype(BF16)
        wob_ref[...] = wo_ref[0].astype(BF16)

    x = x_ref[0]
    xn = _rms_rows(x, g_ref[...]).astype(BF16)
    q = (_dot(xn, wqb_ref[...]) * (XA_HEAD_DIM ** -0.5)).astype(BF16)
    outs = []
    for h in range(XA_HEADS):
        sl = slice(h * XA_HEAD_DIM, (h + 1) * XA_HEAD_DIM)
        s = _dot_nt(q[:, sl], k_ref[0, :, sl])
        p = jnp.exp(s - jnp.max(s, axis=-1, keepdims=True))
        den = jnp.sum(p, axis=-1, keepdims=True)
        outs.append(_dot(p.astype(BF16), v_ref[0, :, sl]) / den)
    o = jnp.concatenate(outs, axis=-1).astype(BF16)
    o_ref[0] = x + _dot(o, wob_ref[...])


def xattn_block(x, gain, wq, k, v, wo, layer, tm):
    b, s, d = x.shape
    ml, xa = k.shape[1], k.shape[2]
    return pl.pallas_call(
        _xattn_kernel,
        grid=(b, s // tm),
        in_specs=[
            pl.BlockSpec((1, tm, d), lambda i, t: (i, t, 0)),
            pl.BlockSpec((1, d), lambda i, t: (0, 0)),
            pl.BlockSpec((1, d, xa), lambda i, t: (layer, 0, 0)),
            pl.BlockSpec((1, ml, xa), lambda i, t: (i, 0, 0)),
            pl.BlockSpec((1, ml, xa), lambda i, t: (i, 0, 0)),
            pl.BlockSpec((1, xa, d), lambda i, t: (layer, 0, 0)),
        ],
        out_specs=pl.BlockSpec((1, tm, d), lambda i, t: (i, t, 0)),
        out_shape=jax.ShapeDtypeStruct((b, s, d), F32),
        scratch_shapes=[pltpu.VMEM((d, xa), BF16), pltpu.VMEM((xa, d), BF16)],
        compiler_params=_cparams("arbitrary", "arbitrary"),
        name="xattn_block",
    )(x, gain.reshape(1, d), wq, k, v, wo)


_HGRN_LEVELS = (32, 16, 8, 4, 2, 1)


def _hgrn_masks():
    c = CHUNK
    i = np.arange(c)[:, None]
    j = np.arange(c)[None, :]
    masks = [(i // (2 * s) == j // (2 * s)) & (i % (2 * s) >= s) & (j % (2 * s) < s) for s in _HGRN_LEVELS]
    masks.append(i == j)
    return jnp.asarray(np.stack(masks).astype(np.float32))


def _segment_row(x, seg, idx):
    rows, lanes = x.shape
    x3 = x.reshape(rows // seg, seg, lanes)
    return jnp.broadcast_to(x3[:, idx:idx + 1, :], x3.shape).reshape(rows, lanes)


def _hgrn_kernel(q_ref, f_ref, i_ref, gate_ref, lbl_ref, nw_ref, masks_ref, y_ref, st_ref, *, layer, n_chunks):
    c = CHUNK
    tt = n_chunks * c

    @pl.when(pl.program_id(2) == 0)
    def _():
        st_ref[...] = jnp.zeros_like(st_ref)

    logits = lbl_ref[...]
    ex = jnp.exp(logits - jnp.max(logits, axis=0, keepdims=True))
    sm = ex / jnp.sum(ex, axis=0, keepdims=True)
    lb = jnp.sum(sm[0:layer + 1, :], axis=0, keepdims=True)

    sig_f = jax.nn.sigmoid(f_ref[0])
    q = _silu(q_ref[0])
    log_f = jnp.log(lb + (1.0 - lb) * sig_f)
    k = (1.0 - lb) * (1.0 - sig_f)
    vb = i_ref[0].astype(BF16)

    row = lax.broadcasted_iota(jnp.int32, (tt, HEAD_DIM), 0)
    rc = row & (c - 1)
    g = log_f
    for s in (1, 2, 4, 8, 16, 32):
        g = g + jnp.where(rc >= s, pltpu.roll(g, s, axis=0), 0.0)
    g_last = _segment_row(g, c, c - 1)

    exps = []
    for s in _HGRN_LEVELS:
        if 2 * s >= SUBLANES:
            exps.append(-jnp.abs(g - _segment_row(g, 2 * s, s - 1)))
        elif s == 2:
            r4 = row & 3
            below = pltpu.roll(log_f, 1, axis=0)
            above = pltpu.roll(log_f, tt - 1, axis=0)
            exps.append(jnp.where(r4 == 0, above, jnp.where(r4 == 1, 0.0,
                                                            jnp.where(r4 == 2, log_f, log_f + below))))
        else:
            exps.append(jnp.where((row & 1) == 1, log_f, 0.0))

    def chunks(x):
        xb = x.astype(BF16)
        return [xb[ci * c:(ci + 1) * c] for ci in range(n_chunks)]

    qs, ks = chunks(q), chunks(k)
    n_lv = len(_HGRN_LEVELS)
    on_diag = masks_ref[n_lv] > 0.5
    attn = [jnp.where(on_diag, _dot_nt(qs[ci], ks[ci]), 0.0) for ci in range(n_chunks)]
    for lv in range(n_lv):
        sc = jnp.exp(exps[lv])
        ql, kl = chunks(q * sc), chunks(k * sc)
        m = masks_ref[lv] > 0.5
        attn = [jnp.where(m, _dot_nt(ql[ci], kl[ci]), attn[ci]) for ci in range(n_chunks)]

    vs = chunks(vb)
    qg = chunks(q * jnp.exp(g))
    kd = chunks(k * jnp.exp(g_last - g))
    upd = [_dot_tn(vs[ci], kd[ci]) for ci in range(n_chunks)]
    intra = [_dot(attn[ci].astype(BF16), vs[ci]) for ci in range(n_chunks)]
    dec = jnp.exp(g_last)
    st = st_ref[...]
    outs = []
    for ci in range(n_chunks):
        outs.append(intra[ci] + _dot_nt(qg[ci], st.astype(BF16)))
        st = st * dec[ci * c:ci * c + 1, :] + upd[ci]
    st_ref[...] = st
    o = jnp.concatenate(outs, axis=0)
    y = o * lax.rsqrt(jnp.mean(o * o, axis=-1, keepdims=True) + NORM_EPS)
    y_ref[0] = (y * nw_ref[...] * _silu(gate_ref[0])).astype(y_ref.dtype)


def hgrn_heads(proj, lb_logits, norm_w, layer, col0, tt):
    b, s, _ = proj.shape
    nl = lb_logits.shape[0]
    masks = _hgrn_masks()
    blk = lambda o: pl.BlockSpec((1, tt, HEAD_DIM), lambda i, h, t: (i, t, col0 + o + h))
    return pl.pallas_call(
        functools.partial(_hgrn_kernel, layer=layer, n_chunks=tt // CHUNK),
        grid=(b, N_HEADS, s // tt),
        in_specs=[
            blk(0), blk(N_HEADS), blk(2 * N_HEADS), blk(3 * N_HEADS),
            pl.BlockSpec((nl, HEAD_DIM), lambda i, h, t: (0, h)),
            pl.BlockSpec((1, HEAD_DIM), lambda i, h, t: (0, h)),
            pl.BlockSpec(masks.shape, lambda i, h, t: (0, 0, 0)),
        ],
        out_specs=pl.BlockSpec((1, tt, HEAD_DIM), lambda i, h, t: (i, t, h)),
        out_shape=jax.ShapeDtypeStruct((b, s, N_HEADS * HEAD_DIM), BF16),
        scratch_shapes=[pltpu.VMEM((HEAD_DIM, HEAD_DIM), F32)],
        compiler_params=_cparams("parallel", "parallel", "arbitrary"),
        name="hgrn_heads",
    )(proj, proj, proj, proj, lb_logits, norm_w.reshape(1, -1), masks)


def _gdn_kernel(q_ref, k_ref, v_ref, gate_ref, gc_ref, gr_ref, cw_ref, alane_ref, blane_ref, acol_ref, bcol_ref,
                nw_ref, y_ref, s_ref, xq_ref, xk_ref, xv_ref, *, tt):
    c = CHUNK
    hd = HEAD_DIM
    pair = 2 * c
    n_pairs = tt // pair
    n_chunks = tt // c
    width = N_HEADS * hd

    @pl.when(pl.program_id(1) == 0)
    def _():
        s_ref[...] = jnp.zeros_like(s_ref)
        for r in (xq_ref, xk_ref, xv_ref):
            r[...] = jnp.zeros_like(r)

    q_all = _silu(_causal_conv(xq_ref, q_ref[0], cw_ref[:, 0:width]))
    k_all = _silu(_causal_conv(xk_ref, k_ref[0], cw_ref[:, width:2 * width]))
    v_all = _silu(_causal_conv(xv_ref, v_ref[0], cw_ref[:, 2 * width:3 * width]))

    gates = gc_ref[0]
    beta_all = jax.nn.sigmoid(gates)
    gam_all = -jnp.exp(alane_ref[...]) * _softplus(gates + blane_ref[...])
    row = lax.broadcasted_iota(jnp.int32, (tt, LANES), 0)
    rc = row & (c - 1)
    for s in (1, 2, 4, 8, 16, 32):
        gam_all = gam_all + jnp.where(rc >= s, pltpu.roll(gam_all, s, axis=0), 0.0)
    glast_all = _segment_row(gam_all, c, c - 1)
    gam_rows = -jnp.exp(acol_ref[...]) * _softplus(gr_ref[0] + bcol_ref[...])
    lc = lax.broadcasted_iota(jnp.int32, gam_rows.shape, 1) & (c - 1)
    for s in (1, 2, 4, 8, 16, 32):
        gam_rows = gam_rows + jnp.where(lc >= s, pltpu.roll(gam_rows, s, axis=1), 0.0)

    ii = lax.broadcasted_iota(jnp.int32, (c, LANES), 0)
    ll = lax.broadcasted_iota(jnp.int32, (c, LANES), 1)
    jj = ll & (c - 1)
    left = ll < c
    lower = ii >= jj
    strict = ii > jj
    eye2 = jnp.where(ii == jj, 1.0, 0.0)

    def block_diag(m):
        zero = jnp.zeros_like(m)
        return jnp.concatenate([jnp.where(left, m, zero), jnp.where(left, zero, m)], axis=0)

    def split2(x):
        hi = x.astype(BF16)
        return hi, (x - hi.astype(F32)).astype(BF16)

    def times_block_diag(lhs, p_hi, p_lo):
        l_hi, l_lo = split2(lhs)
        return _dot(jnp.concatenate([l_hi, l_lo, l_hi], axis=1), jnp.concatenate([p_hi, p_hi, p_lo], axis=0))

    units = [(h, p) for h in range(N_HEADS) for p in range(n_pairs)]
    qd, kd, rhs, khb, kbb, qhb, gam_h = [], [], [], [], [], [], []
    for h in range(N_HEADS):
        lanes = slice(h * hd, (h + 1) * hd)
        qh, kh, vh = q_all[:, lanes], k_all[:, lanes], v_all[:, lanes]
        qh = qh * lax.rsqrt(jnp.sum(qh * qh, axis=-1, keepdims=True) + L2_EPS) * (hd ** -0.5)
        kh = kh * lax.rsqrt(jnp.sum(kh * kh, axis=-1, keepdims=True) + L2_EPS)
        beta = beta_all[:, h:h + 1]
        gam = gam_all[:, N_HEADS + h:N_HEADS + h + 1]
        eg = jnp.exp(gam)
        kb = kh * beta
        gam_h.append(gam)
        rhs.append(jnp.concatenate([kb * eg, vh * beta], axis=1).astype(BF16))
        qd.append((qh * eg).astype(BF16))
        kd.append((kh * jnp.exp(glast_all[:, N_HEADS + h:N_HEADS + h + 1] - gam)).astype(BF16))
        khb.append(kh.astype(BF16))
        kbb.append(kb.astype(BF16))
        qhb.append(qh.astype(BF16))

    x_mats, qk_tiles = [], [[] for _ in range(N_HEADS)]
    for h, p in units:
        r0 = p * pair
        rows = slice(r0, r0 + pair)
        res = _dot_nt(jnp.concatenate([kbb[h][rows], qhb[h][rows]], axis=0), khb[h][rows])
        col = jnp.where(left, jnp.broadcast_to(gam_h[h][r0:r0 + c], (c, LANES)),
                        jnp.broadcast_to(gam_h[h][r0 + c:r0 + pair], (c, LANES)))
        decay = jnp.exp(jnp.minimum(col - gam_rows[N_HEADS + h:N_HEADS + h + 1, rows], 0.0))
        x_mats.append(jnp.where(strict, jnp.where(left, res[0:c], res[c:pair]) * -decay, 0.0))
        qk_tiles[h].append(jnp.where(lower & left, res[pair:pair + c] * decay, 0.0).astype(BF16))
        qk_tiles[h].append(jnp.where(lower & (~left), res[pair + c:2 * pair] * decay, 0.0).astype(BF16))

    t_mats = [eye2 + x for x in x_mats]
    powers = []
    for x in x_mats:
        p_hi, p_lo = split2(x)
        powers.append(times_block_diag(x, block_diag(p_hi), block_diag(p_lo)))
    for it in range(5):
        for u in range(len(units)):
            p_hi, p_lo = split2(powers[u])
            bd_hi, bd_lo = block_diag(p_hi), block_diag(p_lo)
            if it < 4:
                both = times_block_diag(jnp.concatenate([powers[u], t_mats[u]], axis=0), bd_hi, bd_lo)
                powers[u], t_mats[u] = both[0:c], t_mats[u] + both[c:pair]
            else:
                t_mats[u] = t_mats[u] + times_block_diag(t_mats[u], bd_hi, bd_lo)
    wu = [[] for _ in range(N_HEADS)]
    for u, (h, p) in enumerate(units):
        wu[h].append(_dot(block_diag(t_mats[u].astype(BF16)), rhs[h][p * pair:(p + 1) * pair]))

    outs = [[] for _ in range(N_HEADS)]
    zeros_half = jnp.zeros((c, hd), BF16)
    heads = range(N_HEADS)
    states = [s_ref[h] for h in heads]
    chunk_decay = jnp.exp(glast_all)
    for ci in range(n_chunks):
        rows = slice(ci * c, (ci + 1) * c)
        half = slice((ci % 2) * c, (ci % 2 + 1) * c)
        stb = [states[h].astype(BF16) for h in heads]
        wu_c = [wu[h][ci // 2][half] for h in heads]
        w_s = [_dot(wu_c[h][:, :hd].astype(BF16), stb[h]) for h in heads]
        vnb = [(wu_c[h][:, hd:] - w_s[h]).astype(BF16) for h in heads]
        for h in heads:
            v2 = jnp.concatenate([vnb[h], zeros_half] if ci % 2 == 0 else [zeros_half, vnb[h]], axis=0)
            outs[h].append(_dot(qd[h][rows], stb[h]) + _dot(qk_tiles[h][ci], v2))
        states = [states[h] * chunk_decay[ci * c:ci * c + 1, N_HEADS + h:N_HEADS + h + 1]
                  + _dot_tn(kd[h][rows], vnb[h]) for h in heads]
    for h in heads:
        s_ref[h] = states[h]

    for h in range(N_HEADS):
        lanes = slice(h * hd, (h + 1) * hd)
        o = jnp.concatenate(outs[h], axis=0)
        y = o * lax.rsqrt(jnp.mean(o * o, axis=-1, keepdims=True) + NORM_EPS)
        y_ref[0, :, lanes] = (y * nw_ref[:, lanes] * _silu(gate_ref[0, :, lanes])).astype(y_ref.dtype)


def gdn_heads(proj, gates_col, gates_row, conv_w, a_log, dt_bias, norm_w, col0, tt):
    b, s, _ = proj.shape
    width = N_HEADS * HEAD_DIM
    blk = lambda o: pl.BlockSpec((1, tt, width), lambda i, t: (i, t, col0 + o))
    whole = lambda shape: pl.BlockSpec(shape, lambda i, t: (0,) * len(shape))
    lane_vec = jnp.zeros((1, LANES), F32).at[0, N_HEADS:2 * N_HEADS]
    col_vec = jnp.zeros((2 * N_HEADS, 1), F32).at[N_HEADS:, 0]
    return pl.pallas_call(
        functools.partial(_gdn_kernel, tt=tt),
        grid=(b, s // tt),
        in_specs=[
            blk(0), blk(1), blk(2), blk(3),
            pl.BlockSpec((1, tt, LANES), lambda i, t: (i, t, 0)),
            pl.BlockSpec((1, 2 * N_HEADS, tt), lambda i, t: (i, 0, t)),
            whole(conv_w.shape),
            whole((1, LANES)), whole((1, LANES)), whole((2 * N_HEADS, 1)), whole((2 * N_HEADS, 1)),
            whole((1, width)),
        ],
        out_specs=pl.BlockSpec((1, tt, width), lambda i, t: (i, t, 0)),
        out_shape=jax.ShapeDtypeStruct((b, s, width), BF16),
        scratch_shapes=[
            pltpu.VMEM((N_HEADS, HEAD_DIM, HEAD_DIM), F32),
            pltpu.VMEM((SUBLANES, width), F32),
            pltpu.VMEM((SUBLANES, width), F32),
            pltpu.VMEM((SUBLANES, width), F32),
        ],
        compiler_params=_cparams("parallel", "arbitrary"),
        name="gdn_heads",
    )(proj, proj, proj, proj, gates_col, gates_row, conv_w,
      lane_vec.set(a_log), lane_vec.set(dt_bias), col_vec.set(a_log), col_vec.set(dt_bias),
      norm_w.reshape(1, -1))


SSD_PAIRS = SSD_HEADS_PER_GROUP // 2


def _ssd_expand_table():
    expand = np.zeros((SSD_GROUPS, LANES, SSD_GROUP_WIDTH), np.float32)
    for g in range(SSD_GROUPS):
        for hh in range(SSD_HEADS_PER_GROUP):
            expand[g, g * SSD_HEADS_PER_GROUP + hh, hh * SSD_HEADDIM:(hh + 1) * SSD_HEADDIM] = 1.0
    return jnp.asarray(np.concatenate([expand] * 3, axis=1), BF16)


def _ssd_kernel(xfirst_ref, xnext_ref, wz_ref, wx_ref, wb_ref, wc_ref, dtc_ref, dtp_ref, cwx_ref, cwb_ref, cwc_ref,
                cbx_ref, cbb_ref, cbc_ref, blane_ref, alane_ref, bpair_ref, apair_ref, de_ref, exp_ref,
                y_ref, ht_ref, xx_ref, xb_ref, xc_ref, wbf_ref, proj_ref, pnext_ref, *, tt):
    c = CHUNK
    n_chunks = tt // c
    hp = SSD_HEADDIM
    gw = SSD_GROUP_WIDTH

    @pl.when((pl.program_id(1) == 0) & (pl.program_id(2) == 0))
    def _():
        wbf_ref[:, 0:gw] = wz_ref[0].T.astype(BF16)
        wbf_ref[:, gw:2 * gw] = wx_ref[0].T.astype(BF16)
        wbf_ref[:, 2 * gw:2 * gw + LANES] = wb_ref[0].T.astype(BF16)
        wbf_ref[:, 2 * gw + LANES:2 * gw + 2 * LANES] = wc_ref[0].T.astype(BF16)
        proj_ref[...] = _dot(xfirst_ref[...], wbf_ref[...])

    @pl.when(pl.program_id(2) == 0)
    def _():
        for r in (ht_ref, xx_ref, xb_ref, xc_ref):
            r[...] = jnp.zeros_like(r)

    piece_w = PROJ_PIECE
    pieces = [slice(p0, p0 + piece_w) for p0 in range(0, wbf_ref.shape[1], piece_w)]

    def project_piece():
        if pieces:
            cols = pieces.pop(0)
            pnext_ref[:, cols] = _dot(xnext_ref[...], wbf_ref[:, cols])

    project_piece()
    xs_tiles = []
    for lo in range(0, gw, PROJ_PIECE):
        lanes = slice(lo, lo + PROJ_PIECE)
        xs_tiles.append(_silu(_causal_conv(xx_ref.at[:, lanes], proj_ref[:, gw + lo:gw + lo + PROJ_PIECE],
                                           cwx_ref[:, lanes]) + cbx_ref[:, lanes]))
        project_piece()
    xs = jnp.concatenate(xs_tiles, axis=1)
    bm = _silu(_causal_conv(xb_ref, proj_ref[:, 2 * gw:2 * gw + LANES], cwb_ref[...])
               + cbb_ref[...]).astype(BF16)
    cm = _silu(_causal_conv(xc_ref, proj_ref[:, 2 * gw + LANES:2 * gw + 2 * LANES], cwc_ref[...])
               + cbc_ref[...]).astype(BF16)

    dt_c = _softplus(dtc_ref[0] + blane_ref[...])
    acum_c = dt_c * -jnp.exp(alane_ref[...])
    rc = lax.broadcasted_iota(jnp.int32, (tt, LANES), 0) & (c - 1)
    for s in (1, 2, 4, 8, 16, 32):
        acum_c = acum_c + jnp.where(rc >= s, pltpu.roll(acum_c, s, axis=0), 0.0)
    s_c = dt_c * jnp.exp(_segment_row(acum_c, c, c - 1) - acum_c)
    a_hi, a_mid, a_lo = _split3(acum_c)
    acum_e = _dot(jnp.concatenate([a_hi, a_mid, a_lo], axis=1), exp_ref[0])
    s_hi, s_mid, _ = _split3(s_c)
    s_e = _dot(jnp.concatenate([s_hi, s_mid], axis=1), exp_ref[0, 0:2 * LANES, :])
    decay_e = jnp.exp(acum_e)
    xw = (xs * s_e).astype(BF16)
    xsb = xs.astype(BF16)

    project_piece()
    dt_p = _softplus(dtp_ref[0, 0] + jnp.concatenate([bpair_ref[0]] * n_chunks, axis=1))
    acum_p = dt_p * -jnp.exp(jnp.concatenate([apair_ref[0]] * n_chunks, axis=1))
    lc = lax.broadcasted_iota(jnp.int32, acum_p.shape, 1) & (c - 1)
    for s in (1, 2, 4, 8, 16, 32):
        acum_p = acum_p + jnp.where(lc >= s, pltpu.roll(acum_p, s, axis=1), 0.0)

    ii = lax.broadcasted_iota(jnp.int32, (c, LANES), 0)
    ll = lax.broadcasted_iota(jnp.int32, (c, LANES), 1)
    lower2 = ii >= (ll & (c - 1))
    left = ll < hp
    while pieces:
        project_piece()

    chunk_rows = [slice(ci * c, (ci + 1) * c) for ci in range(n_chunks)]
    cb2 = [_dot_nt(cm[r], jnp.concatenate([bm[r], bm[r]], axis=0)) for r in chunk_rows]
    upd = [_dot_tn(bm[r], xw[r]) for r in chunk_rows]
    y_diag = []
    for ci, r in enumerate(chunk_rows):
        tiles = []
        for pr in range(SSD_PAIRS):
            lanes = slice(2 * pr * hp, (2 * pr + 2) * hp)
            tok = slice(ci * LANES, (ci + 1) * LANES)
            lmat = jnp.where(lower2, jnp.exp(jnp.minimum(acum_e[r, lanes] - acum_p[pr:pr + 1, tok], 0.0)), 0.0)
            xp = xsb[r, lanes]
            zero = jnp.zeros_like(xp)
            rhs = jnp.concatenate([jnp.where(left, xp, zero), jnp.where(left, zero, xp)], axis=0)
            tiles.append(_dot((cb2[ci] * lmat * dt_p[pr:pr + 1, tok]).astype(BF16), rhs))
        y_diag.append(jnp.concatenate(tiles, axis=1))
    ht = ht_ref[...]
    y_off = []
    for ci, r in enumerate(chunk_rows):
        y_off.append(_dot(cm[r], ht.astype(BF16)))
        ht = ht * decay_e[ci * c + c - 1:ci * c + c, :] + upd[ci]
    ht_ref[...] = ht
    y = jnp.concatenate(y_diag, axis=0) + jnp.concatenate(y_off, axis=0) * decay_e + de_ref[...] * xs
    y_ref[0] = y * _silu(proj_ref[:, 0:gw])
    proj_ref[...] = pnext_ref[...]


def ssd_groups(xn, w_t, layer, dt_col, dt_pairs, conv_w, conv_b, dt_bias, a_log, d_skip, tt):
    b, s, _ = dt_col.shape
    k = xn.shape[1]
    n_t = s // tt
    g_w = SSD_GROUP_WIDTH
    n_xblk = SSD_GROUPS
    expand = _ssd_expand_table()
    rep = lambda p: jnp.repeat(p.astype(F32), SSD_HEADDIM).reshape(1, -1)
    row2 = lambda v: v.reshape(1, -1)
    lane_vec = lambda p: jnp.pad(p.astype(F32), (0, LANES - p.shape[0])).reshape(1, LANES)

    def pair_tile(p):
        t = jnp.repeat(p.astype(F32).reshape(SSD_GROUPS, SSD_PAIRS, 2, 1), CHUNK, axis=-1)
        t = t.reshape(SSD_GROUPS, SSD_PAIRS, LANES)
        return jnp.pad(t, ((0, 0), (0, SUBLANES - SSD_PAIRS), (0, 0)))
    x_off = n_xblk
    b_off = 2 * n_xblk * (g_w // LANES)
    c_off = b_off + SSD_GROUPS
    p_w = 2 * g_w + 2 * LANES
    w_wide = lambda o: pl.BlockSpec((1, g_w, k), lambda g, i, t: (layer, o + g, 0))
    w_narrow = lambda o: pl.BlockSpec((1, LANES, k), lambda g, i, t: (layer, o + g, 0))
    return pl.pallas_call(
        functools.partial(_ssd_kernel, tt=tt),
        grid=(SSD_GROUPS, b, n_t),
        in_specs=[
            pl.BlockSpec((tt, k), lambda g, i, t: (0, 0)),
            pl.BlockSpec((tt, k), lambda g, i, t: ((i * n_t + t + 1) % (b * n_t), 0)),
            w_wide(0), w_wide(x_off), w_narrow(b_off), w_narrow(c_off),
            pl.BlockSpec((1, tt, LANES), lambda g, i, t: (i, t, 0)),
            pl.BlockSpec((1, 1, SUBLANES, 2 * tt), lambda g, i, t: (i, g, 0, t)),
            pl.BlockSpec((CONV_K, g_w), lambda g, i, t: (0, g)),
            pl.BlockSpec((CONV_K, LANES), lambda g, i, t: (0, n_xblk * (g_w // LANES) + g)),
            pl.BlockSpec((CONV_K, LANES), lambda g, i, t: (0, n_xblk * (g_w // LANES) + SSD_GROUPS + g)),
            pl.BlockSpec((1, g_w), lambda g, i, t: (0, g)),
            pl.BlockSpec((1, LANES), lambda g, i, t: (0, n_xblk * (g_w // LANES) + g)),
            pl.BlockSpec((1, LANES), lambda g, i, t: (0, n_xblk * (g_w // LANES) + SSD_GROUPS + g)),
            pl.BlockSpec((1, LANES), lambda g, i, t: (0, 0)),
            pl.BlockSpec((1, LANES), lambda g, i, t: (0, 0)),
            pl.BlockSpec((1, SUBLANES, LANES), lambda g, i, t: (g, 0, 0)),
            pl.BlockSpec((1, SUBLANES, LANES), lambda g, i, t: (g, 0, 0)),
            pl.BlockSpec((1, g_w), lambda g, i, t: (0, g)),
            pl.BlockSpec((1, 3 * LANES, g_w), lambda g, i, t: (g, 0, 0)),
        ],
        out_specs=pl.BlockSpec((1, tt, g_w), lambda g, i, t: (i, t, g)),
        out_shape=jax.ShapeDtypeStruct((b, s, SSD_GROUPS * g_w), F32),
        scratch_shapes=[
            pltpu.VMEM((SSD_DSTATE, g_w), F32),
            pltpu.VMEM((SUBLANES, g_w), F32),
            pltpu.VMEM((SUBLANES, LANES), F32),
            pltpu.VMEM((SUBLANES, LANES), F32),
            pltpu.VMEM((k, p_w), BF16),
            pltpu.VMEM((tt, p_w), F32),
            pltpu.VMEM((tt, p_w), F32),
        ],
        compiler_params=_cparams("arbitrary", "arbitrary", "arbitrary"),
        name="ssd_groups",
    )(xn, xn, w_t, w_t, w_t, w_t, dt_col, dt_pairs, conv_w, conv_w, conv_w,
      row2(conv_b), row2(conv_b), row2(conv_b), lane_vec(dt_bias), lane_vec(a_log),
      pair_tile(dt_bias), pair_tile(a_log), rep(d_skip), expand)


def _tail_weight(w_t, layer, n_main):
    tail = w_t[layer, n_main:, :]
    return jnp.pad(tail, ((0, LANES - tail.shape[0]), (0, 0)))[None]


def _tile(n, pref):
    t = min(n, pref)
    while n % t:
        t //= 2
    return t


def kernel(x, mem, norm_mix, norm_xattn, norm_mem, norm_ffn, norm_final, hy_w_in, hgrn_lb_logits, hgrn_norm,
           gdn_conv_w, gdn_a_log, gdn_dt_bias, gdn_norm, hy_w_out, ssd_w_in, ssd_conv_w, ssd_conv_b,
           ssd_dt_bias, ssd_a_log, ssd_d, ssd_norm, ssd_w_out, xa_wq, xa_wk, xa_wv, xa_wo, ffn_w_gate,
           ffn_w_up, ffn_w_down):
    bsz, seq, d = x.shape
    tok = bsz * seq
    depth = norm_mix.shape[0]
    tm = _tile(tok, TOKEN_TILE)
    tt = _tile(seq, SEQ_TILE)
    hy_w_in_t = jnp.swapaxes(hy_w_in, 1, 2)
    ssd_w_in_t = jnp.swapaxes(ssd_w_in, 1, 2)

    h = x.reshape(tok, d)
    for layer in range(depth):
        if layer % 2 == 0:
            e = layer // 2
            n_main = 8 * N_HEADS * HEAD_DIM
            xn, tail = norm_cast(h, norm_mix[layer], _tail_weight(hy_w_in_t, e, n_main), tm)
            proj = ws_matmul([xn], hy_w_in_t, e, n_main, tm, _tile(n_main, COLUMN_TILE),
                             w_transposed=True).reshape(bsz, seq, n_main)
            gates_col = tail.reshape(bsz, seq, LANES)
            gates_row = jnp.swapaxes(gates_col[:, :, :2 * N_HEADS], 1, 2)
            y_a = hgrn_heads(proj, hgrn_lb_logits.astype(F32), hgrn_norm[e], e, 0, tt)
            y_b = gdn_heads(proj, gates_col, gates_row, gdn_conv_w[e].astype(F32), gdn_a_log[e].astype(F32),
                            gdn_dt_bias[e].astype(F32), gdn_norm[e], 4, _tile(seq, GDN_SEQ_TILE))
            h = ws_matmul([y_a.reshape(tok, -1), y_b.reshape(tok, -1)], hy_w_out, e, d, _tile(tok, OUT_PROJ_TOKEN_TILE),
                          d, resid=h, single_buffer_w=True)
        else:
            o = layer // 2
            n_heads = ssd_dt_bias.shape[1]
            n_main = ssd_w_in.shape[2] - n_heads
            xn, tail = norm_cast(h, norm_mix[layer], _tail_weight(ssd_w_in_t, o, n_main), tm)
            dt_col = tail.reshape(bsz, seq, LANES)
            dt_pairs = dt_col[:, :, :n_heads].reshape(bsz, seq // CHUNK, CHUNK, SSD_GROUPS, SSD_PAIRS, 2)
            dt_pairs = dt_pairs.transpose(0, 3, 4, 1, 5, 2).reshape(bsz, SSD_GROUPS, SSD_PAIRS, 2 * seq)
            dt_pairs = jnp.pad(dt_pairs, ((0, 0), (0, 0), (0, SUBLANES - SSD_PAIRS), (0, 0)))
            y = ssd_groups(xn, ssd_w_in_t, o, dt_col, dt_pairs, ssd_conv_w[o].astype(F32), ssd_conv_b[o].astype(F32),
                           ssd_dt_bias[o], ssd_a_log[o], ssd_d[o], tt)
            h = ws_matmul([y.reshape(tok, -1)], ssd_w_out, o, d, _tile(tok, OUT_PROJ_TOKEN_TILE),
                          _tile(d, COLUMN_TILE), gain=ssd_norm[o], resid=h, single_buffer_w=True)
        k_mem, v_mem = mem_kv(mem, norm_mem[layer], xa_wk, xa_wv, layer)
        h = xattn_block(h.reshape(bsz, seq, d), norm_xattn[layer], xa_wq, k_mem, v_mem, xa_wo, layer,
                        _tile(seq, SEQ_TILE)).reshape(tok, d)
        h = ffn_block(h, norm_ffn[layer], ffn_w_gate, ffn_w_up, ffn_w_down, layer, norm_final,
                      _tile(tok, FFN_TOKEN_TILE), _tile(ffn_w_gate.shape[2], FFN_TILE), layer == depth - 1)
    return h.reshape(bsz, seq, d)
```

```python
import functools

import numpy as np
import jax
import jax.numpy as jnp
from jax import lax
from jax.experimental import pallas as pl
from jax.experimental.pallas import tpu as pltpu

F32 = jnp.float32
BF16 = jnp.bfloat16

NORM_EPS = 1e-6
L2_EPS = 1e-6
CHUNK = 64
CONV_K = 4
LANES = 128
SUBLANES = 8
HEAD_DIM = 128
N_HEADS = 8
SSD_HEADDIM = 64
SSD_GROUPS = 8
SSD_HEADS_PER_GROUP = 8
SSD_GROUP_WIDTH = SSD_HEADDIM * SSD_HEADS_PER_GROUP
SSD_DSTATE = 128
XA_HEADS = 4
XA_HEAD_DIM = 128
VMEM_LIMIT_BYTES = 56 * 1024 * 1024
TOKEN_TILE = 1024
COLUMN_TILE = 1024
WEIGHT_STAGE_ROWS = 512
PROJ_PIECE = 256
FFN_TILE = 256
FFN_TOKEN_TILE = 2048
SEQ_TILE = 512
OUT_PROJ_TOKEN_TILE = TOKEN_TILE // 2
GDN_SEQ_TILE = 256


def _cparams(*sem):
    return pltpu.CompilerParams(dimension_semantics=sem, vmem_limit_bytes=VMEM_LIMIT_BYTES)


def _dot(a, b):
    return jnp.dot(a, b, preferred_element_type=F32)


def _dot_nt(a, b):
    return lax.dot_general(a, b, (((1,), (1,)), ((), ())), preferred_element_type=F32)


def _dot_tn(a, b):
    return lax.dot_general(a, b, (((0,), (0,)), ((), ())), preferred_element_type=F32)


def _split3(x):
    hi = x.astype(BF16)
    r1 = x - hi.astype(F32)
    mid = r1.astype(BF16)
    lo = (r1 - mid.astype(F32)).astype(BF16)
    return hi, mid, lo


def _silu(x):
    return x * jax.nn.sigmoid(x)


def _softplus(x):
    return jnp.maximum(x, 0.0) + jnp.log(1.0 + jnp.exp(-jnp.abs(x)))


def _rms_rows(x, gain):
    ms = jnp.mean(x * x, axis=-1, keepdims=True)
    return x * lax.rsqrt(ms + NORM_EPS) * gain


def _causal_conv(carry_ref, x, w):
    rows = x.shape[0]
    ext = jnp.concatenate([carry_ref[...], x], axis=0)
    y = w[CONV_K - 1:CONV_K, :] * x
    for k in range(CONV_K - 1):
        y = y + w[k:k + 1, :] * pltpu.roll(ext, CONV_K - 1 - k, axis=0)[SUBLANES:]
    carry_ref[...] = x[rows - SUBLANES:]
    return y


def _norm_cast_kernel(x_ref, g_ref, wt_ref, o_ref, tail_ref):
    xn = _rms_rows(x_ref[...], g_ref[...]).astype(BF16)
    o_ref[...] = xn
    tail_ref[...] = _dot_nt(xn, wt_ref[0].astype(BF16))


def norm_cast(x, gain, w_tail_t, tm):
    m, k = x.shape
    return pl.pallas_call(
        _norm_cast_kernel,
        grid=(m // tm,),
        in_specs=[pl.BlockSpec((tm, k), lambda i: (i, 0)), pl.BlockSpec((1, k), lambda i: (0, 0)),
                  pl.BlockSpec((1, LANES, k), lambda i: (0, 0, 0))],
        out_specs=[pl.BlockSpec((tm, k), lambda i: (i, 0)), pl.BlockSpec((tm, LANES), lambda i: (i, 0))],
        out_shape=[jax.ShapeDtypeStruct((m, k), BF16), jax.ShapeDtypeStruct((m, LANES), F32)],
        compiler_params=_cparams("parallel"),
        name="norm_cast",
    )(x, gain.reshape(1, k), w_tail_t)


def _ws_matmul_kernel(*refs, n_a, normalize, residual, w_transposed):
    a_refs = refs[:n_a]
    rest = list(refs[n_a:])
    g_ref = rest.pop(0) if normalize else None
    w_ref = rest.pop(0)
    r_ref = rest.pop(0) if residual else None
    o_ref, wb_ref = rest

    @pl.when(pl.program_id(1) == 0)
    def _():
        w = w_ref[0]
        wb_ref[...] = (w.T if w_transposed else w).astype(BF16)

    acc = None
    k0 = 0
    for a_ref in a_refs:
        a = a_ref[...]
        if normalize:
            a = _rms_rows(a, g_ref[...]).astype(BF16)
        part = _dot(a, wb_ref[k0:k0 + a.shape[1], :])
        k0 += a.shape[1]
        acc = part if acc is None else acc + part
    o_ref[...] = r_ref[...] + acc if residual else acc


def ws_matmul(a_list, w, layer, n, tm, tn, gain=None, resid=None, single_buffer_w=False, w_transposed=False):
    m = a_list[0].shape[0]
    k = w.shape[2] if w_transposed else w.shape[1]
    normalize = gain is not None
    residual = resid is not None
    w_mode = dict(pipeline_mode=pl.Buffered(1)) if single_buffer_w else {}
    in_specs = [pl.BlockSpec((tm, a.shape[1]), lambda j, i: (i, 0)) for a in a_list]
    args = list(a_list)
    if normalize:
        in_specs.append(pl.BlockSpec((1, k), lambda j, i: (0, 0)))
        args.append(gain.reshape(1, k))
    if w_transposed:
        in_specs.append(pl.BlockSpec((1, tn, k), lambda j, i: (layer, j, 0), **w_mode))
    else:
        in_specs.append(pl.BlockSpec((1, k, tn), lambda j, i: (layer, 0, j), **w_mode))
    args.append(w)
    if residual:
        in_specs.append(pl.BlockSpec((tm, tn), lambda j, i: (i, j)))
        args.append(resid)
    return pl.pallas_call(
        functools.partial(_ws_matmul_kernel, n_a=len(a_list), normalize=normalize, residual=residual,
                          w_transposed=w_transposed),
        grid=(n // tn, m // tm),
        in_specs=in_specs,
        out_specs=pl.BlockSpec((tm, tn), lambda j, i: (i, j)),
        out_shape=jax.ShapeDtypeStruct((m, n), F32),
        scratch_shapes=[pltpu.VMEM((k, tn), BF16)],
        compiler_params=_cparams("arbitrary", "arbitrary"),
        name="ws_matmul",
    )(*args)


def _ffn_kernel(x_hbm, g_ref, wg_ref, wu_ref, wd_ref, gf_ref, o_ref, xn_ref, sem, *, final_norm):
    f = pl.program_id(1)

    @pl.when(f == 0)
    def _():
        tm = o_ref.shape[0]
        rows = pl.ds(pl.multiple_of(pl.program_id(0) * tm, tm), tm)
        copy = pltpu.make_async_copy(x_hbm.at[rows, :], o_ref, sem)
        copy.start()
        copy.wait()
        xn_ref[...] = _rms_rows(o_ref[...], g_ref[...]).astype(BF16)

    xn = xn_ref[...]
    act = (_silu(_dot(xn, wg_ref[0].astype(BF16))) * _dot(xn, wu_ref[0].astype(BF16))).astype(BF16)
    o_ref[...] += _dot(act, wd_ref[0].astype(BF16))

    if final_norm:
        @pl.when(f == pl.num_programs(1) - 1)
        def _():
            o_ref[...] = _rms_rows(o_ref[...], gf_ref[...])


def ffn_block(x, gain, wg, wu, wd, layer, gain_final, tm, tf, final_norm):
    m, d = x.shape
    dff = wg.shape[2]
    return pl.pallas_call(
        functools.partial(_ffn_kernel, final_norm=final_norm),
        grid=(m // tm, dff // tf),
        in_specs=[
            pl.BlockSpec(memory_space=pl.ANY),
            pl.BlockSpec((1, d), lambda i, f: (0, 0)),
            pl.BlockSpec((1, d, tf), lambda i, f: (layer, 0, f)),
            pl.BlockSpec((1, d, tf), lambda i, f: (layer, 0, f)),
            pl.BlockSpec((1, tf, d), lambda i, f: (layer, f, 0)),
            pl.BlockSpec((1, d), lambda i, f: (0, 0)),
        ],
        out_specs=pl.BlockSpec((tm, d), lambda i, f: (i, 0), pipeline_mode=pl.Buffered(1)),
        out_shape=jax.ShapeDtypeStruct((m, d), F32),
        scratch_shapes=[pltpu.VMEM((tm, d), BF16), pltpu.SemaphoreType.DMA],
        compiler_params=_cparams("parallel", "arbitrary"),
        name="ffn_block",
    )(x, gain.reshape(1, d), wg, wu, wd, gain_final.reshape(1, d))


def _mem_kv_kernel(m_ref, g_ref, wk_ref, wv_ref, k_ref, v_ref):
    mn = _rms_rows(m_ref[0], g_ref[...]).astype(BF16)
    k_ref[0] = _dot(mn, wk_ref[0].astype(BF16)).astype(BF16)
    v_ref[0] = _dot(mn, wv_ref[0].astype(BF16)).astype(BF16)


def mem_kv(mem, gain, wk, wv, layer):
    b, ml, d = mem.shape
    xa = wk.shape[2]
    return pl.pallas_call(
        _mem_kv_kernel,
        grid=(b,),
        in_specs=[
            pl.BlockSpec((1, ml, d), lambda i: (i, 0, 0)),
            pl.BlockSpec((1, d), lambda i: (0, 0)),
            pl.BlockSpec((1, d, xa), lambda i: (layer, 0, 0)),
            pl.BlockSpec((1, d, xa), lambda i: (layer, 0, 0)),
        ],
        out_specs=[pl.BlockSpec((1, ml, xa), lambda i: (i, 0, 0))] * 2,
        out_shape=[jax.ShapeDtypeStruct((b, ml, xa), BF16)] * 2,
        compiler_params=_cparams("parallel"),
        name="mem_kv",
    )(mem, gain.reshape(1, d), wk, wv)


def _xattn_rows(x, g_ref, wqb_ref, k_ref, v_ref, wob_ref):
    xn = _rms_rows(x, g_ref[...]).astype(BF16)
    q = (_dot(xn, wqb_ref[...]) * (XA_HEAD_DIM ** -0.5)).astype(BF16)
    outs = []
    for h in range(XA_HEADS):
        sl = slice(h * XA_HEAD_DIM, (h + 1) * XA_HEAD_DIM)
        s = _dot_nt(q[:, sl], k_ref[0, :, sl])
        p = jnp.exp(s - jnp.max(s, axis=-1, keepdims=True))
        den = jnp.sum(p, axis=-1, keepdims=True)
        outs.append(_dot(p.astype(BF16), v_ref[0, :, sl]) / den)
    o = jnp.concatenate(outs, axis=-1).astype(BF16)
    return x + _dot(o, wob_ref[...])


def _xattn_kernel(x_ref, g_ref, wq_ref, k_ref, v_ref, wo_ref, o_ref, wqb_ref, wob_ref):
    @pl.when((pl.program_id(0) == 0) & (pl.program_id(1) == 0))
    def _():
        wqb_ref[...] = wq_ref[0].astype(BF16)
        wob_ref[...] = wo_ref[0].astype(BF16)

    o_ref[0] = _xattn_rows(x_ref[0], g_ref, wqb_ref, k_ref, v_ref, wob_ref)


def _outproj_xattn_kernel(ya_ref, yb_ref, x_ref, wout_hbm, g_ref, wq_ref, k_ref, v_ref, wo_ref, o_ref,
                          woutb_ref, wqb_ref, wob_ref, stage_ref, sem, *, layer):
    @pl.when((pl.program_id(0) == 0) & (pl.program_id(1) == 0))
    def _():
        ch = stage_ref.shape[0]
        for r0 in range(0, woutb_ref.shape[0], ch):
            copy = pltpu.make_async_copy(wout_hbm.at[layer, pl.ds(r0, ch), :], stage_ref, sem)
            copy.start()
            copy.wait()
            woutb_ref[r0:r0 + ch, :] = stage_ref[...].astype(BF16)
        wqb_ref[...] = wq_ref[0].astype(BF16)
        wob_ref[...] = wo_ref[0].astype(BF16)

    ka = ya_ref.shape[2]
    h = x_ref[0] + _dot(ya_ref[0], woutb_ref[0:ka, :]) + _dot(yb_ref[0], woutb_ref[ka:, :])
    o_ref[0] = _xattn_rows(h, g_ref, wqb_ref, k_ref, v_ref, wob_ref)


def outproj_xattn_block(y_a, y_b, x, w_out, gain, wq, k, v, wo, layer_out, layer, tm):
    b, s, d = x.shape
    ka = y_a.shape[2]
    ml, xa = k.shape[1], k.shape[2]
    single = dict(pipeline_mode=pl.Buffered(1))
    return pl.pallas_call(
        functools.partial(_outproj_xattn_kernel, layer=layer_out),
        grid=(b, s // tm),
        in_specs=[
            pl.BlockSpec((1, tm, ka), lambda i, t: (i, t, 0)),
            pl.BlockSpec((1, tm, ka), lambda i, t: (i, t, 0)),
            pl.BlockSpec((1, tm, d), lambda i, t: (i, t, 0)),
            pl.BlockSpec(memory_space=pl.ANY),
            pl.BlockSpec((1, d), lambda i, t: (0, 0)),
            pl.BlockSpec((1, d, xa), lambda i, t: (layer, 0, 0), **single),
            pl.BlockSpec((1, ml, xa), lambda i, t: (i, 0, 0)),
            pl.BlockSpec((1, ml, xa), lambda i, t: (i, 0, 0)),
            pl.BlockSpec((1, xa, d), lambda i, t: (layer, 0, 0), **single),
        ],
        out_specs=pl.BlockSpec((1, tm, d), lambda i, t: (i, t, 0)),
        out_shape=jax.ShapeDtypeStruct((b, s, d), F32),
        scratch_shapes=[
            pltpu.VMEM((2 * ka, d), BF16),
            pltpu.VMEM((d, xa), BF16),
            pltpu.VMEM((xa, d), BF16),
            pltpu.VMEM((min(2 * ka, WEIGHT_STAGE_ROWS), d), F32),
            pltpu.SemaphoreType.DMA,
        ],
        compiler_params=_cparams("arbitrary", "arbitrary"),
        name="outproj_xattn_block",
    )(y_a, y_b, x, w_out, gain.reshape(1, d), wq, k, v, wo)


def xattn_block(x, gain, wq, k, v, wo, layer, tm):
    b, s, d = x.shape
    ml, xa = k.shape[1], k.shape[2]
    return pl.pallas_call(
        _xattn_kernel,
        grid=(b, s // tm),
        in_specs=[
            pl.BlockSpec((1, tm, d), lambda i, t: (i, t, 0)),
            pl.BlockSpec((1, d), lambda i, t: (0, 0)),
            pl.BlockSpec((1, d, xa), lambda i, t: (layer, 0, 0)),
            pl.BlockSpec((1, ml, xa), lambda i, t: (i, 0, 0)),
            pl.BlockSpec((1, ml, xa), lambda i, t: (i, 0, 0)),
            pl.BlockSpec((1, xa, d), lambda i, t: (layer, 0, 0)),
        ],
        out_specs=pl.BlockSpec((1, tm, d), lambda i, t: (i, t, 0)),
        out_shape=jax.ShapeDtypeStruct((b, s, d), F32),
        scratch_shapes=[pltpu.VMEM((d, xa), BF16), pltpu.VMEM((xa, d), BF16)],
        compiler_params=_cparams("arbitrary", "arbitrary"),
        name="xattn_block",
    )(x, gain.reshape(1, d), wq, k, v, wo)


_HGRN_LEVELS = (32, 16, 8, 4, 2, 1)


def _hgrn_masks():
    c = CHUNK
    i = np.arange(c)[:, None]
    j = np.arange(c)[None, :]
    masks = [(i // (2 * s) == j // (2 * s)) & (i % (2 * s) >= s) & (j % (2 * s) < s) for s in _HGRN_LEVELS]
    masks.append(i == j)
    return jnp.asarray(np.stack(masks).astype(np.float32))


def _segment_row(x, seg, idx):
    rows, lanes = x.shape
    x3 = x.reshape(rows // seg, seg, lanes)
    return jnp.broadcast_to(x3[:, idx:idx + 1, :], x3.shape).reshape(rows, lanes)


def _hgrn_kernel(q_ref, f_ref, i_ref, gate_ref, lbl_ref, nw_ref, masks_ref, y_ref, st_ref, *, layer, n_chunks):
    c = CHUNK
    tt = n_chunks * c

    @pl.when(pl.program_id(2) == 0)
    def _():
        st_ref[...] = jnp.zeros_like(st_ref)

    logits = lbl_ref[...]
    ex = jnp.exp(logits - jnp.max(logits, axis=0, keepdims=True))
    sm = ex / jnp.sum(ex, axis=0, keepdims=True)
    lb = jnp.sum(sm[0:layer + 1, :], axis=0, keepdims=True)

    sig_f = jax.nn.sigmoid(f_ref[0])
    q = _silu(q_ref[0])
    log_f = jnp.log(lb + (1.0 - lb) * sig_f)
    k = (1.0 - lb) * (1.0 - sig_f)
    vb = i_ref[0].astype(BF16)

    row = lax.broadcasted_iota(jnp.int32, (tt, HEAD_DIM), 0)
    rc = row & (c - 1)
    g = log_f
    for s in (1, 2, 4, 8, 16, 32):
        g = g + jnp.where(rc >= s, pltpu.roll(g, s, axis=0), 0.0)
    g_last = _segment_row(g, c, c - 1)

    exps = []
    for s in _HGRN_LEVELS:
        if 2 * s >= SUBLANES:
            exps.append(-jnp.abs(g - _segment_row(g, 2 * s, s - 1)))
        elif s == 2:
            r4 = row & 3
            below = pltpu.roll(log_f, 1, axis=0)
            above = pltpu.roll(log_f, tt - 1, axis=0)
            exps.append(jnp.where(r4 == 0, above, jnp.where(r4 == 1, 0.0,
                                                            jnp.where(r4 == 2, log_f, log_f + below))))
        else:
            exps.append(jnp.where((row & 1) == 1, log_f, 0.0))

    def chunks(x):
        xb = x.astype(BF16)
        return [xb[ci * c:(ci + 1) * c] for ci in range(n_chunks)]

    qs, ks = chunks(q), chunks(k)
    n_lv = len(_HGRN_LEVELS)
    on_diag = masks_ref[n_lv] > 0.5
    attn = [jnp.where(on_diag, _dot_nt(qs[ci], ks[ci]), 0.0) for ci in range(n_chunks)]
    for lv in range(n_lv):
        sc = jnp.exp(exps[lv])
        ql, kl = chunks(q * sc), chunks(k * sc)
        m = masks_ref[lv] > 0.5
        attn = [jnp.where(m, _dot_nt(ql[ci], kl[ci]), attn[ci]) for ci in range(n_chunks)]

    vs = chunks(vb)
    qg = chunks(q * jnp.exp(g))
    kd = chunks(k * jnp.exp(g_last - g))
    upd = [_dot_tn(vs[ci], kd[ci]) for ci in range(n_chunks)]
    intra = [_dot(attn[ci].astype(BF16), vs[ci]) for ci in range(n_chunks)]
    dec = jnp.exp(g_last)
    st = st_ref[...]
    outs = []
    for ci in range(n_chunks):
        outs.append(intra[ci] + _dot_nt(qg[ci], st.astype(BF16)))
        st = st * dec[ci * c:ci * c + 1, :] + upd[ci]
    st_ref[...] = st
    o = jnp.concatenate(outs, axis=0)
    y = o * lax.rsqrt(jnp.mean(o * o, axis=-1, keepdims=True) + NORM_EPS)
    y_ref[0] = (y * nw_ref[...] * _silu(gate_ref[0])).astype(y_ref.dtype)


def hgrn_heads(proj, lb_logits, norm_w, layer, col0, tt):
    b, s, _ = proj.shape
    nl = lb_logits.shape[0]
    masks = _hgrn_masks()
    blk = lambda o: pl.BlockSpec((1, tt, HEAD_DIM), lambda i, h, t: (i, t, col0 + o + h))
    return pl.pallas_call(
        functools.partial(_hgrn_kernel, layer=layer, n_chunks=tt // CHUNK),
        grid=(b, N_HEADS, s // tt),
        in_specs=[
            blk(0), blk(N_HEADS), blk(2 * N_HEADS), blk(3 * N_HEADS),
            pl.BlockSpec((nl, HEAD_DIM), lambda i, h, t: (0, h)),
            pl.BlockSpec((1, HEAD_DIM), lambda i, h, t: (0, h)),
            pl.BlockSpec(masks.shape, lambda i, h, t: (0, 0, 0)),
        ],
        out_specs=pl.BlockSpec((1, tt, HEAD_DIM), lambda i, h, t: (i, t, h)),
        out_shape=jax.ShapeDtypeStruct((b, s, N_HEADS * HEAD_DIM), BF16),
        scratch_shapes=[pltpu.VMEM((HEAD_DIM, HEAD_DIM), F32)],
        compiler_params=_cparams("parallel", "parallel", "arbitrary"),
        name="hgrn_heads",
    )(proj, proj, proj, proj, lb_logits, norm_w.reshape(1, -1), masks)


def _gdn_kernel(q_ref, k_ref, v_ref, gate_ref, gc_ref, gr_ref, cw_ref, alane_ref, blane_ref, acol_ref, bcol_ref,
                nw_ref, y_ref, s_ref, xq_ref, xk_ref, xv_ref, *, tt):
    c = CHUNK
    hd = HEAD_DIM
    pair = 2 * c
    n_pairs = tt // pair
    n_chunks = tt // c
    width = N_HEADS * hd

    @pl.when(pl.program_id(1) == 0)
    def _():
        s_ref[...] = jnp.zeros_like(s_ref)
        for r in (xq_ref, xk_ref, xv_ref):
            r[...] = jnp.zeros_like(r)

    q_all = _silu(_causal_conv(xq_ref, q_ref[0], cw_ref[:, 0:width]))
    k_all = _silu(_causal_conv(xk_ref, k_ref[0], cw_ref[:, width:2 * width]))
    v_all = _silu(_causal_conv(xv_ref, v_ref[0], cw_ref[:, 2 * width:3 * width]))

    gates = gc_ref[0]
    beta_all = jax.nn.sigmoid(gates)
    gam_all = -jnp.exp(alane_ref[...]) * _softplus(gates + blane_ref[...])
    row = lax.broadcasted_iota(jnp.int32, (tt, LANES), 0)
    rc = row & (c - 1)
    for s in (1, 2, 4, 8, 16, 32):
        gam_all = gam_all + jnp.where(rc >= s, pltpu.roll(gam_all, s, axis=0), 0.0)
    glast_all = _segment_row(gam_all, c, c - 1)
    gam_rows = -jnp.exp(acol_ref[...]) * _softplus(gr_ref[0] + bcol_ref[...])
    lc = lax.broadcasted_iota(jnp.int32, gam_rows.shape, 1) & (c - 1)
    for s in (1, 2, 4, 8, 16, 32):
        gam_rows = gam_rows + jnp.where(lc >= s, pltpu.roll(gam_rows, s, axis=1), 0.0)

    ii = lax.broadcasted_iota(jnp.int32, (c, LANES), 0)
    ll = lax.broadcasted_iota(jnp.int32, (c, LANES), 1)
    jj = ll & (c - 1)
    left = ll < c
    lower = ii >= jj
    strict = ii > jj
    eye2 = jnp.where(ii == jj, 1.0, 0.0)

    def block_diag(m):
        zero = jnp.zeros_like(m)
        return jnp.concatenate([jnp.where(left, m, zero), jnp.where(left, zero, m)], axis=0)

    def split2(x):
        hi = x.astype(BF16)
        return hi, (x - hi.astype(F32)).astype(BF16)

    def times_block_diag(lhs, p_hi, p_lo):
        l_hi, l_lo = split2(lhs)
        return _dot(jnp.concatenate([l_hi, l_lo, l_hi], axis=1), jnp.concatenate([p_hi, p_hi, p_lo], axis=0))

    units = [(h, p) for h in range(N_HEADS) for p in range(n_pairs)]
    qd, kd, rhs, khb, kbb, qhb, gam_h = [], [], [], [], [], [], []
    for h in range(N_HEADS):
        lanes = slice(h * hd, (h + 1) * hd)
        qh, kh, vh = q_all[:, lanes], k_all[:, lanes], v_all[:, lanes]
        qh = qh * lax.rsqrt(jnp.sum(qh * qh, axis=-1, keepdims=True) + L2_EPS) * (hd ** -0.5)
        kh = kh * lax.rsqrt(jnp.sum(kh * kh, axis=-1, keepdims=True) + L2_EPS)
        beta = beta_all[:, h:h + 1]
        gam = gam_all[:, N_HEADS + h:N_HEADS + h + 1]
        eg = jnp.exp(gam)
        kb = kh * beta
        gam_h.append(gam)
        rhs.append(jnp.concatenate([kb * eg, vh * beta], axis=1).astype(BF16))
        qd.append((qh * eg).astype(BF16))
        kd.append((kh * jnp.exp(glast_all[:, N_HEADS + h:N_HEADS + h + 1] - gam)).astype(BF16))
        khb.append(kh.astype(BF16))
        kbb.append(kb.astype(BF16))
        qhb.append(qh.astype(BF16))

    x_mats, qk_tiles = [], [[] for _ in range(N_HEADS)]
    for h, p in units:
        r0 = p * pair
        rows = slice(r0, r0 + pair)
        res = _dot_nt(jnp.concatenate([kbb[h][rows], qhb[h][rows]], axis=0), khb[h][rows])
        col = jnp.where(left, jnp.broadcast_to(gam_h[h][r0:r0 + c], (c, LANES)),
                        jnp.broadcast_to(gam_h[h][r0 + c:r0 + pair], (c, LANES)))
        decay = jnp.exp(jnp.minimum(col - gam_rows[N_HEADS + h:N_HEADS + h + 1, rows], 0.0))
        x_mats.append(jnp.where(strict, jnp.where(left, res[0:c], res[c:pair]) * -decay, 0.0))
        qk_tiles[h].append(jnp.where(lower & left, res[pair:pair + c] * decay, 0.0).astype(BF16))
        qk_tiles[h].append(jnp.where(lower & (~left), res[pair + c:2 * pair] * decay, 0.0).astype(BF16))

    t_mats = [eye2 + x for x in x_mats]
    powers = []
    for x in x_mats:
        p_hi, p_lo = split2(x)
        powers.append(times_block_diag(x, block_diag(p_hi), block_diag(p_lo)))
    for it in range(5):
        for u in range(len(units)):
            p_hi, p_lo = split2(powers[u])
            bd_hi, bd_lo = block_diag(p_hi), block_diag(p_lo)
            if it < 4:
                both = times_block_diag(jnp.concatenate([powers[u], t_mats[u]], axis=0), bd_hi, bd_lo)
                powers[u], t_mats[u] = both[0:c], t_mats[u] + both[c:pair]
            else:
                t_mats[u] = t_mats[u] + times_block_diag(t_mats[u], bd_hi, bd_lo)
    wu = [[] for _ in range(N_HEADS)]
    for u, (h, p) in enumerate(units):
        wu[h].append(_dot(block_diag(t_mats[u].astype(BF16)), rhs[h][p * pair:(p + 1) * pair]))

    outs = [[] for _ in range(N_HEADS)]
    zeros_half = jnp.zeros((c, hd), BF16)
    heads = range(N_HEADS)
    states = [s_ref[h] for h in heads]
    chunk_decay = jnp.exp(glast_all)
    for ci in range(n_chunks):
        rows = slice(ci * c, (ci + 1) * c)
        half = slice((ci % 2) * c, (ci % 2 + 1) * c)
        stb = [states[h].astype(BF16) for h in heads]
        wu_c = [wu[h][ci // 2][half] for h in heads]
        w_s = [_dot(wu_c[h][:, :hd].astype(BF16), stb[h]) for h in heads]
        vnb = [(wu_c[h][:, hd:] - w_s[h]).astype(BF16) for h in heads]
        for h in heads:
            v2 = jnp.concatenate([vnb[h], zeros_half] if ci % 2 == 0 else [zeros_half, vnb[h]], axis=0)
            outs[h].append(_dot(qd[h][rows], stb[h]) + _dot(qk_tiles[h][ci], v2))
        states = [states[h] * chunk_decay[ci * c:ci * c + 1, N_HEADS + h:N_HEADS + h + 1]
                  + _dot_tn(kd[h][rows], vnb[h]) for h in heads]
    for h in heads:
        s_ref[h] = states[h]

    for h in range(N_HEADS):
        lanes = slice(h * hd, (h + 1) * hd)
        o = jnp.concatenate(outs[h], axis=0)
        y = o * lax.rsqrt(jnp.mean(o * o, axis=-1, keepdims=True) + NORM_EPS)
        y_ref[0, :, lanes] = (y * nw_ref[:, lanes] * _silu(gate_ref[0, :, lanes])).astype(y_ref.dtype)


def gdn_heads(proj, gates_col, gates_row, conv_w, a_log, dt_bias, norm_w, col0, tt):
    b, s, _ = proj.shape
    width = N_HEADS * HEAD_DIM
    blk = lambda o: pl.BlockSpec((1, tt, width), lambda i, t: (i, t, col0 + o))
    whole = lambda shape: pl.BlockSpec(shape, lambda i, t: (0,) * len(shape))
    lane_vec = jnp.zeros((1, LANES), F32).at[0, N_HEADS:2 * N_HEADS]
    col_vec = jnp.zeros((2 * N_HEADS, 1), F32).at[N_HEADS:, 0]
    return pl.pallas_call(
        functools.partial(_gdn_kernel, tt=tt),
        grid=(b, s // tt),
        in_specs=[
            blk(0), blk(1), blk(2), blk(3),
            pl.BlockSpec((1, tt, LANES), lambda i, t: (i, t, 0)),
            pl.BlockSpec((1, 2 * N_HEADS, tt), lambda i, t: (i, 0, t)),
            whole(conv_w.shape),
            whole((1, LANES)), whole((1, LANES)), whole((2 * N_HEADS, 1)), whole((2 * N_HEADS, 1)),
            whole((1, width)),
        ],
        out_specs=pl.BlockSpec((1, tt, width), lambda i, t: (i, t, 0)),
        out_shape=jax.ShapeDtypeStruct((b, s, width), BF16),
        scratch_shapes=[
            pltpu.VMEM((N_HEADS, HEAD_DIM, HEAD_DIM), F32),
            pltpu.VMEM((SUBLANES, width), F32),
            pltpu.VMEM((SUBLANES, width), F32),
            pltpu.VMEM((SUBLANES, width), F32),
        ],
        compiler_params=_cparams("parallel", "arbitrary"),
        name="gdn_heads",
    )(proj, proj, proj, proj, gates_col, gates_row, conv_w,
      lane_vec.set(a_log), lane_vec.set(dt_bias), col_vec.set(a_log), col_vec.set(dt_bias),
      norm_w.reshape(1, -1))


SSD_PAIRS = SSD_HEADS_PER_GROUP // 2


def _ssd_expand_table():
    expand = np.zeros((SSD_GROUPS, LANES, SSD_GROUP_WIDTH), np.float32)
    for g in range(SSD_GROUPS):
        for hh in range(SSD_HEADS_PER_GROUP):
            expand[g, g * SSD_HEADS_PER_GROUP + hh, hh * SSD_HEADDIM:(hh + 1) * SSD_HEADDIM] = 1.0
    return jnp.asarray(np.concatenate([expand] * 3, axis=1), BF16)


def _ssd_kernel(xfirst_ref, xnext_ref, wz_ref, wx_ref, wb_ref, wc_ref, dtc_ref, dtp_ref, cwx_ref, cwb_ref, cwc_ref,
                cbx_ref, cbb_ref, cbc_ref, blane_ref, alane_ref, bpair_ref, apair_ref, de_ref, exp_ref,
                y_ref, ht_ref, xx_ref, xb_ref, xc_ref, wbf_ref, proj_ref, pnext_ref, *, tt):
    c = CHUNK
    n_chunks = tt // c
    hp = SSD_HEADDIM
    gw = SSD_GROUP_WIDTH

    @pl.when((pl.program_id(1) == 0) & (pl.program_id(2) == 0))
    def _():
        wbf_ref[:, 0:gw] = wz_ref[0].T.astype(BF16)
        wbf_ref[:, gw:2 * gw] = wx_ref[0].T.astype(BF16)
        wbf_ref[:, 2 * gw:2 * gw + LANES] = wb_ref[0].T.astype(BF16)
        wbf_ref[:, 2 * gw + LANES:2 * gw + 2 * LANES] = wc_ref[0].T.astype(BF16)
        proj_ref[...] = _dot(xfirst_ref[...], wbf_ref[...])

    @pl.when(pl.program_id(2) == 0)
    def _():
        for r in (ht_ref, xx_ref, xb_ref, xc_ref):
            r[...] = jnp.zeros_like(r)

    piece_w = PROJ_PIECE
    pieces = [slice(p0, p0 + piece_w) for p0 in range(0, wbf_ref.shape[1], piece_w)]

    def project_piece():
        if pieces:
            cols = pieces.pop(0)
            pnext_ref[:, cols] = _dot(xnext_ref[...], wbf_ref[:, cols])

    project_piece()
    xs_tiles = []
    for lo in range(0, gw, PROJ_PIECE):
        lanes = slice(lo, lo + PROJ_PIECE)
        xs_tiles.append(_silu(_causal_conv(xx_ref.at[:, lanes], proj_ref[:, gw + lo:gw + lo + PROJ_PIECE],
                                           cwx_ref[:, lanes]) + cbx_ref[:, lanes]))
        project_piece()
    xs = jnp.concatenate(xs_tiles, axis=1)
    bm = _silu(_causal_conv(xb_ref, proj_ref[:, 2 * gw:2 * gw + LANES], cwb_ref[...])
               + cbb_ref[...]).astype(BF16)
    cm = _silu(_causal_conv(xc_ref, proj_ref[:, 2 * gw + LANES:2 * gw + 2 * LANES], cwc_ref[...])
               + cbc_ref[...]).astype(BF16)

    dt_c = _softplus(dtc_ref[0] + blane_ref[...])
    acum_c = dt_c * -jnp.exp(alane_ref[...])
    rc = lax.broadcasted_iota(jnp.int32, (tt, LANES), 0) & (c - 1)
    for s in (1, 2, 4, 8, 16, 32):
        acum_c = acum_c + jnp.where(rc >= s, pltpu.roll(acum_c, s, axis=0), 0.0)
    s_c = dt_c * jnp.exp(_segment_row(acum_c, c, c - 1) - acum_c)
    a_hi, a_mid, a_lo = _split3(acum_c)
    acum_e = _dot(jnp.concatenate([a_hi, a_mid, a_lo], axis=1), exp_ref[0])
    s_hi, s_mid, _ = _split3(s_c)
    s_e = _dot(jnp.concatenate([s_hi, s_mid], axis=1), exp_ref[0, 0:2 * LANES, :])
    decay_e = jnp.exp(acum_e)
    xw = (xs * s_e).astype(BF16)
    xsb = xs.astype(BF16)

    project_piece()
    dt_p = _softplus(dtp_ref[0, 0] + jnp.concatenate([bpair_ref[0]] * n_chunks, axis=1))
    acum_p = dt_p * -jnp.exp(jnp.concatenate([apair_ref[0]] * n_chunks, axis=1))
    lc = lax.broadcasted_iota(jnp.int32, acum_p.shape, 1) & (c - 1)
    for s in (1, 2, 4, 8, 16, 32):
        acum_p = acum_p + jnp.where(lc >= s, pltpu.roll(acum_p, s, axis=1), 0.0)

    ii = lax.broadcasted_iota(jnp.int32, (c, LANES), 0)
    ll = lax.broadcasted_iota(jnp.int32, (c, LANES), 1)
    lower2 = ii >= (ll & (c - 1))
    left = ll < hp
    while pieces:
        project_piece()

    chunk_rows = [slice(ci * c, (ci + 1) * c) for ci in range(n_chunks)]
    cb2 = [_dot_nt(cm[r], jnp.concatenate([bm[r], bm[r]], axis=0)) for r in chunk_rows]
    upd = [_dot_tn(bm[r], xw[r]) for r in chunk_rows]
    y_diag = []
    for ci, r in enumerate(chunk_rows):
        tiles = []
        for pr in range(SSD_PAIRS):
            lanes = slice(2 * pr * hp, (2 * pr + 2) * hp)
            tok = slice(ci * LANES, (ci + 1) * LANES)
            lmat = jnp.where(lower2, jnp.exp(jnp.minimum(acum_e[r, lanes] - acum_p[pr:pr + 1, tok], 0.0)), 0.0)
            xp = xsb[r, lanes]
            zero = jnp.zeros_like(xp)
            rhs = jnp.concatenate([jnp.where(left, xp, zero), jnp.where(left, zero, xp)], axis=0)
            tiles.append(_dot((cb2[ci] * lmat * dt_p[pr:pr + 1, tok]).astype(BF16), rhs))
        y_diag.append(jnp.concatenate(tiles, axis=1))
    ht = ht_ref[...]
    y_off = []
    for ci, r in enumerate(chunk_rows):
        y_off.append(_dot(cm[r], ht.astype(BF16)))
        ht = ht * decay_e[ci * c + c - 1:ci * c + c, :] + upd[ci]
    ht_ref[...] = ht
    y = jnp.concatenate(y_diag, axis=0) + jnp.concatenate(y_off, axis=0) * decay_e + de_ref[...] * xs
    y_ref[0] = y * _silu(proj_ref[:, 0:gw])
    proj_ref[...] = pnext_ref[...]


def ssd_groups(xn, w_t, layer, dt_col, dt_pairs, conv_w, conv_b, dt_bias, a_log, d_skip, tt):
    b, s, _ = dt_col.shape
    k = xn.shape[1]
    n_t = s // tt
    g_w = SSD_GROUP_WIDTH
    n_xblk = SSD_GROUPS
    expand = _ssd_expand_table()
    rep = lambda p: jnp.repeat(p.astype(F32), SSD_HEADDIM).reshape(1, -1)
    row2 = lambda v: v.reshape(1, -1)
    lane_vec = lambda p: jnp.pad(p.astype(F32), (0, LANES - p.shape[0])).reshape(1, LANES)

    def pair_tile(p):
        t = jnp.repeat(p.astype(F32).reshape(SSD_GROUPS, SSD_PAIRS, 2, 1), CHUNK, axis=-1)
        t = t.reshape(SSD_GROUPS, SSD_PAIRS, LANES)
        return jnp.pad(t, ((0, 0), (0, SUBLANES - SSD_PAIRS), (0, 0)))
    x_off = n_xblk
    b_off = 2 * n_xblk * (g_w // LANES)
    c_off = b_off + SSD_GROUPS
    p_w = 2 * g_w + 2 * LANES
    w_wide = lambda o: pl.BlockSpec((1, g_w, k), lambda g, i, t: (layer, o + g, 0))
    w_narrow = lambda o: pl.BlockSpec((1, LANES, k), lambda g, i, t: (layer, o + g, 0))
    return pl.pallas_call(
        functools.partial(_ssd_kernel, tt=tt),
        grid=(SSD_GROUPS, b, n_t),
        in_specs=[
            pl.BlockSpec((tt, k), lambda g, i, t: (0, 0)),
            pl.BlockSpec((tt, k), lambda g, i, t: ((i * n_t + t + 1) % (b * n_t), 0)),
            w_wide(0), w_wide(x_off), w_narrow(b_off), w_narrow(c_off),
            pl.BlockSpec((1, tt, LANES), lambda g, i, t: (i, t, 0)),
            pl.BlockSpec((1, 1, SUBLANES, 2 * tt), lambda g, i, t: (i, g, 0, t)),
            pl.BlockSpec((CONV_K, g_w), lambda g, i, t: (0, g)),
            pl.BlockSpec((CONV_K, LANES), lambda g, i, t: (0, n_xblk * (g_w // LANES) + g)),
            pl.BlockSpec((CONV_K, LANES), lambda g, i, t: (0, n_xblk * (g_w // LANES) + SSD_GROUPS + g)),
            pl.BlockSpec((1, g_w), lambda g, i, t: (0, g)),
            pl.BlockSpec((1, LANES), lambda g, i, t: (0, n_xblk * (g_w // LANES) + g)),
            pl.BlockSpec((1, LANES), lambda g, i, t: (0, n_xblk * (g_w // LANES) + SSD_GROUPS + g)),
            pl.BlockSpec((1, LANES), lambda g, i, t: (0, 0)),
            pl.BlockSpec((1, LANES), lambda g, i, t: (0, 0)),
            pl.BlockSpec((1, SUBLANES, LANES), lambda g, i, t: (g, 0, 0)),
            pl.BlockSpec((1, SUBLANES, LANES), lambda g, i, t: (g, 0, 0)),
            pl.BlockSpec((1, g_w), lambda g, i, t: (0, g)),
            pl.BlockSpec((1, 3 * LANES, g_w), lambda g, i, t: (g, 0, 0)),
        ],
        out_specs=pl.BlockSpec((1, tt, g_w), lambda g, i, t: (i, t, g)),
        out_shape=jax.ShapeDtypeStruct((b, s, SSD_GROUPS * g_w), F32),
        scratch_shapes=[
            pltpu.VMEM((SSD_DSTATE, g_w), F32),
            pltpu.VMEM((SUBLANES, g_w), F32),
            pltpu.VMEM((SUBLANES, LANES), F32),
            pltpu.VMEM((SUBLANES, LANES), F32),
            pltpu.VMEM((k, p_w), BF16),
            pltpu.VMEM((tt, p_w), F32),
            pltpu.VMEM((tt, p_w), F32),
        ],
        compiler_params=_cparams("arbitrary", "arbitrary", "arbitrary"),
        name="ssd_groups",
    )(xn, xn, w_t, w_t, w_t, w_t, dt_col, dt_pairs, conv_w, conv_w, conv_w,
      row2(conv_b), row2(conv_b), row2(conv_b), lane_vec(dt_bias), lane_vec(a_log),
      pair_tile(dt_bias), pair_tile(a_log), rep(d_skip), expand)


def _tail_weight(w_t, layer, n_main):
    tail = w_t[layer, n_main:, :]
    return jnp.pad(tail, ((0, LANES - tail.shape[0]), (0, 0)))[None]


def _tile(n, pref):
    t = min(n, pref)
    while n % t:
        t //= 2
    return t


def kernel(x, mem, norm_mix, norm_xattn, norm_mem, norm_ffn, norm_final, hy_w_in, hgrn_lb_logits, hgrn_norm,
           gdn_conv_w, gdn_a_log, gdn_dt_bias, gdn_norm, hy_w_out, ssd_w_in, ssd_conv_w, ssd_conv_b,
           ssd_dt_bias, ssd_a_log, ssd_d, ssd_norm, ssd_w_out, xa_wq, xa_wk, xa_wv, xa_wo, ffn_w_gate,
           ffn_w_up, ffn_w_down):
    bsz, seq, d = x.shape
    tok = bsz * seq
    depth = norm_mix.shape[0]
    tm = _tile(tok, TOKEN_TILE)
    tt = _tile(seq, SEQ_TILE)
    hy_w_in_t = jnp.swapaxes(hy_w_in, 1, 2)
    ssd_w_in_t = jnp.swapaxes(ssd_w_in, 1, 2)

    h = x.reshape(tok, d)
    for layer in range(depth):
        k_mem, v_mem = mem_kv(mem, norm_mem[layer], xa_wk, xa_wv, layer)
        if layer % 2 == 0:
            e = layer // 2
            n_main = 8 * N_HEADS * HEAD_DIM
            xn, tail = norm_cast(h, norm_mix[layer], _tail_weight(hy_w_in_t, e, n_main), tm)
            proj = ws_matmul([xn], hy_w_in_t, e, n_main, tm, _tile(n_main, COLUMN_TILE),
                             w_transposed=True).reshape(bsz, seq, n_main)
            gates_col = tail.reshape(bsz, seq, LANES)
            gates_row = jnp.swapaxes(gates_col[:, :, :2 * N_HEADS], 1, 2)
            y_a = hgrn_heads(proj, hgrn_lb_logits.astype(F32), hgrn_norm[e], e, 0, tt)
            y_b = gdn_heads(proj, gates_col, gates_row, gdn_conv_w[e].astype(F32), gdn_a_log[e].astype(F32),
                            gdn_dt_bias[e].astype(F32), gdn_norm[e], 4, _tile(seq, GDN_SEQ_TILE))
            h = outproj_xattn_block(y_a, y_b, h.reshape(bsz, seq, d), hy_w_out, norm_xattn[layer], xa_wq, k_mem,
                                    v_mem, xa_wo, e, layer, _tile(seq, SEQ_TILE)).reshape(tok, d)
        else:
            o = layer // 2
            n_heads = ssd_dt_bias.shape[1]
            n_main = ssd_w_in.shape[2] - n_heads
            xn, tail = norm_cast(h, norm_mix[layer], _tail_weight(ssd_w_in_t, o, n_main), tm)
            dt_col = tail.reshape(bsz, seq, LANES)
            dt_pairs = dt_col[:, :, :n_heads].reshape(bsz, seq // CHUNK, CHUNK, SSD_GROUPS, SSD_PAIRS, 2)
            dt_pairs = dt_pairs.transpose(0, 3, 4, 1, 5, 2).reshape(bsz, SSD_GROUPS, SSD_PAIRS, 2 * seq)
            dt_pairs = jnp.pad(dt_pairs, ((0, 0), (0, 0), (0, SUBLANES - SSD_PAIRS), (0, 0)))
            y = ssd_groups(xn, ssd_w_in_t, o, dt_col, dt_pairs, ssd_conv_w[o].astype(F32), ssd_conv_b[o].astype(F32),
                           ssd_dt_bias[o], ssd_a_log[o], ssd_d[o], tt)
            h = ws_matmul([y.reshape(tok, -1)], ssd_w_out, o, d, _tile(tok, OUT_PROJ_TOKEN_TILE),
                          _tile(d, COLUMN_TILE), gain=ssd_norm[o], resid=h, single_buffer_w=True)
            h = xattn_block(h.reshape(bsz, seq, d), norm_xattn[layer], xa_wq, k_mem, v_mem, xa_wo, layer,
                            _tile(seq, SEQ_TILE)).reshape(tok, d)
        h = ffn_block(h, norm_ffn[layer], ffn_w_gate, ffn_w_up, ffn_w_down, layer, norm_final,
                      _tile(tok, FFN_TOKEN_TILE), _tile(ffn_w_gate.shape[2], FFN_TILE), layer == depth - 1)
    return h.reshape(bsz, seq, d)
```

```python
import functools

import numpy as np
import jax
import jax.numpy as jnp
from jax import lax
from jax.experimental import pallas as pl
from jax.experimental.pallas import tpu as pltpu

F32 = jnp.float32
BF16 = jnp.bfloat16

NORM_EPS = 1e-6
L2_EPS = 1e-6
CHUNK = 64
CONV_K = 4
LANES = 128
SUBLANES = 8
HEAD_DIM = 128
N_HEADS = 8
SSD_HEADDIM = 64
SSD_GROUPS = 8
SSD_HEADS_PER_GROUP = 8
SSD_GROUP_WIDTH = SSD_HEADDIM * SSD_HEADS_PER_GROUP
SSD_DSTATE = 128
XA_HEADS = 4
XA_HEAD_DIM = 128
VMEM_LIMIT_BYTES = 56 * 1024 * 1024
TOKEN_TILE = 1024
COLUMN_TILE = 1024
XA_RING = 3
WEIGHT_STAGE_ROWS = 512
PROJ_PIECE = 256
FFN_TILE = 256
FFN_TOKEN_TILE = 2048
SEQ_TILE = 512
OUT_PROJ_TOKEN_TILE = TOKEN_TILE // 2
GDN_SEQ_TILE = 256


def _cparams(*sem):
    return pltpu.CompilerParams(dimension_semantics=sem, vmem_limit_bytes=VMEM_LIMIT_BYTES)


def _dot(a, b):
    return jnp.dot(a, b, preferred_element_type=F32)


def _dot_nt(a, b):
    return lax.dot_general(a, b, (((1,), (1,)), ((), ())), preferred_element_type=F32)


def _dot_tn(a, b):
    return lax.dot_general(a, b, (((0,), (0,)), ((), ())), preferred_element_type=F32)


def _split3(x):
    hi = x.astype(BF16)
    r1 = x - hi.astype(F32)
    mid = r1.astype(BF16)
    lo = (r1 - mid.astype(F32)).astype(BF16)
    return hi, mid, lo


def _silu(x):
    return x * jax.nn.sigmoid(x)


def _softplus(x):
    return jnp.maximum(x, 0.0) + jnp.log(1.0 + jnp.exp(-jnp.abs(x)))


def _rms_rows(x, gain):
    ms = jnp.mean(x * x, axis=-1, keepdims=True)
    return x * lax.rsqrt(ms + NORM_EPS) * gain


def _causal_conv(carry_ref, x, w):
    rows = x.shape[0]
    ext = jnp.concatenate([carry_ref[...], x], axis=0)
    y = w[CONV_K - 1:CONV_K, :] * x
    for k in range(CONV_K - 1):
        y = y + w[k:k + 1, :] * pltpu.roll(ext, CONV_K - 1 - k, axis=0)[SUBLANES:]
    carry_ref[...] = x[rows - SUBLANES:]
    return y


def _norm_cast_kernel(x_ref, g_ref, wt_ref, o_ref, tail_ref):
    xn = _rms_rows(x_ref[...], g_ref[...]).astype(BF16)
    o_ref[...] = xn
    tail_ref[...] = _dot_nt(xn, wt_ref[0].astype(BF16))


def norm_cast(x, gain, w_tail_t, tm):
    m, k = x.shape
    return pl.pallas_call(
        _norm_cast_kernel,
        grid=(m // tm,),
        in_specs=[pl.BlockSpec((tm, k), lambda i: (i, 0)), pl.BlockSpec((1, k), lambda i: (0, 0)),
                  pl.BlockSpec((1, LANES, k), lambda i: (0, 0, 0))],
        out_specs=[pl.BlockSpec((tm, k), lambda i: (i, 0)), pl.BlockSpec((tm, LANES), lambda i: (i, 0))],
        out_shape=[jax.ShapeDtypeStruct((m, k), BF16), jax.ShapeDtypeStruct((m, LANES), F32)],
        compiler_params=_cparams("parallel"),
        name="norm_cast",
    )(x, gain.reshape(1, k), w_tail_t)


def _ws_matmul_kernel(*refs, n_a, normalize, residual, w_transposed):
    a_refs = refs[:n_a]
    rest = list(refs[n_a:])
    g_ref = rest.pop(0) if normalize else None
    w_ref = rest.pop(0)
    r_ref = rest.pop(0) if residual else None
    o_ref, wb_ref = rest

    @pl.when(pl.program_id(1) == 0)
    def _():
        w = w_ref[0]
        wb_ref[...] = (w.T if w_transposed else w).astype(BF16)

    acc = None
    k0 = 0
    for a_ref in a_refs:
        a = a_ref[...]
        if normalize:
            a = _rms_rows(a, g_ref[...]).astype(BF16)
        part = _dot(a, wb_ref[k0:k0 + a.shape[1], :])
        k0 += a.shape[1]
        acc = part if acc is None else acc + part
    o_ref[...] = r_ref[...] + acc if residual else acc


def ws_matmul(a_list, w, layer, n, tm, tn, gain=None, resid=None, single_buffer_w=False, w_transposed=False):
    m = a_list[0].shape[0]
    k = w.shape[2] if w_transposed else w.shape[1]
    normalize = gain is not None
    residual = resid is not None
    w_mode = dict(pipeline_mode=pl.Buffered(1)) if single_buffer_w else {}
    in_specs = [pl.BlockSpec((tm, a.shape[1]), lambda j, i: (i, 0)) for a in a_list]
    args = list(a_list)
    if normalize:
        in_specs.append(pl.BlockSpec((1, k), lambda j, i: (0, 0)))
        args.append(gain.reshape(1, k))
    if w_transposed:
        in_specs.append(pl.BlockSpec((1, tn, k), lambda j, i: (layer, j, 0), **w_mode))
    else:
        in_specs.append(pl.BlockSpec((1, k, tn), lambda j, i: (layer, 0, j), **w_mode))
    args.append(w)
    if residual:
        in_specs.append(pl.BlockSpec((tm, tn), lambda j, i: (i, j)))
        args.append(resid)
    return pl.pallas_call(
        functools.partial(_ws_matmul_kernel, n_a=len(a_list), normalize=normalize, residual=residual,
                          w_transposed=w_transposed),
        grid=(n // tn, m // tm),
        in_specs=in_specs,
        out_specs=pl.BlockSpec((tm, tn), lambda j, i: (i, j)),
        out_shape=jax.ShapeDtypeStruct((m, n), F32),
        scratch_shapes=[pltpu.VMEM((k, tn), BF16)],
        compiler_params=_cparams("arbitrary", "arbitrary"),
        name="ws_matmul",
    )(*args)


def _ffn_kernel(x_hbm, g_ref, wg_ref, wu_ref, wd_ref, gf_ref, o_ref, xn_ref, sem, *, final_norm):
    f = pl.program_id(1)

    @pl.when(f == 0)
    def _():
        tm = o_ref.shape[0]
        rows = pl.ds(pl.multiple_of(pl.program_id(0) * tm, tm), tm)
        copy = pltpu.make_async_copy(x_hbm.at[rows, :], o_ref, sem)
        copy.start()
        copy.wait()
        xn_ref[...] = _rms_rows(o_ref[...], g_ref[...]).astype(BF16)

    xn = xn_ref[...]
    act = (_silu(_dot(xn, wg_ref[0].astype(BF16))) * _dot(xn, wu_ref[0].astype(BF16))).astype(BF16)
    o_ref[...] += _dot(act, wd_ref[0].astype(BF16))

    if final_norm:
        @pl.when(f == pl.num_programs(1) - 1)
        def _():
            o_ref[...] = _rms_rows(o_ref[...], gf_ref[...])


def ffn_block(x, gain, wg, wu, wd, layer, gain_final, tm, tf, final_norm):
    m, d = x.shape
    dff = wg.shape[2]
    return pl.pallas_call(
        functools.partial(_ffn_kernel, final_norm=final_norm),
        grid=(m // tm, dff // tf),
        in_specs=[
            pl.BlockSpec(memory_space=pl.ANY),
            pl.BlockSpec((1, d), lambda i, f: (0, 0)),
            pl.BlockSpec((1, d, tf), lambda i, f: (layer, 0, f)),
            pl.BlockSpec((1, d, tf), lambda i, f: (layer, 0, f)),
            pl.BlockSpec((1, tf, d), lambda i, f: (layer, f, 0)),
            pl.BlockSpec((1, d), lambda i, f: (0, 0)),
        ],
        out_specs=pl.BlockSpec((tm, d), lambda i, f: (i, 0), pipeline_mode=pl.Buffered(1)),
        out_shape=jax.ShapeDtypeStruct((m, d), F32),
        scratch_shapes=[pltpu.VMEM((tm, d), BF16), pltpu.SemaphoreType.DMA],
        compiler_params=_cparams("parallel", "arbitrary"),
        name="ffn_block",
    )(x, gain.reshape(1, d), wg, wu, wd, gain_final.reshape(1, d))


def _mem_kv_kernel(m_ref, g_ref, wk_ref, wv_ref, k_ref, v_ref):
    mn = _rms_rows(m_ref[0], g_ref[...]).astype(BF16)
    k_ref[0] = _dot(mn, wk_ref[0].astype(BF16)).astype(BF16)
    v_ref[0] = _dot(mn, wv_ref[0].astype(BF16)).astype(BF16)


def mem_kv(mem, gain, wk, wv, layer):
    b, ml, d = mem.shape
    xa = wk.shape[2]
    return pl.pallas_call(
        _mem_kv_kernel,
        grid=(b,),
        in_specs=[
            pl.BlockSpec((1, ml, d), lambda i: (i, 0, 0)),
            pl.BlockSpec((1, d), lambda i: (0, 0)),
            pl.BlockSpec((1, d, xa), lambda i: (layer, 0, 0)),
            pl.BlockSpec((1, d, xa), lambda i: (layer, 0, 0)),
        ],
        out_specs=[pl.BlockSpec((1, ml, xa), lambda i: (i, 0, 0))] * 2,
        out_shape=[jax.ShapeDtypeStruct((b, ml, xa), BF16)] * 2,
        compiler_params=_cparams("parallel"),
        name="mem_kv",
    )(mem, gain.reshape(1, d), wk, wv)


def _xattn_rows(x, g_ref, wqb_ref, k_ref, v_ref, wob_ref):
    xn = _rms_rows(x, g_ref[...]).astype(BF16)
    q = (_dot(xn, wqb_ref[...]) * (XA_HEAD_DIM ** -0.5)).astype(BF16)
    outs = []
    for h in range(XA_HEADS):
        sl = slice(h * XA_HEAD_DIM, (h + 1) * XA_HEAD_DIM)
        s = _dot_nt(q[:, sl], k_ref[0, :, sl])
        p = jnp.exp(s - jnp.max(s, axis=-1, keepdims=True))
        den = jnp.sum(p, axis=-1, keepdims=True)
        outs.append(_dot(p.astype(BF16), v_ref[0, :, sl]) / den)
    o = jnp.concatenate(outs, axis=-1).astype(BF16)
    return x + _dot(o, wob_ref[...])


def _xattn_kernel(x_hbm, g_ref, wq_ref, k_ref, v_ref, wo_ref, o_ref, wqb_ref, wob_ref, xbuf_ref, sems,
                  *, tm, n_t, total):
    step = pl.program_id(0) * n_t + pl.program_id(1)

    def fetch(s):
        slot = lax.rem(s, XA_RING)
        rows = pl.ds(pl.multiple_of(lax.rem(s, n_t) * tm, tm), tm)
        return pltpu.make_async_copy(x_hbm.at[s // n_t, rows, :], xbuf_ref.at[slot], sems.at[slot])

    @pl.when(step == 0)
    def _():
        wqb_ref[...] = wq_ref[0].astype(BF16)
        wob_ref[...] = wo_ref[0].astype(BF16)
        for s0 in range(min(XA_RING - 1, total)):
            fetch(jnp.int32(s0)).start()

    @pl.when(step + (XA_RING - 1) < total)
    def _():
        fetch(step + (XA_RING - 1)).start()

    fetch(step).wait()
    o_ref[0] = _xattn_rows(xbuf_ref[lax.rem(step, XA_RING)], g_ref, wqb_ref, k_ref, v_ref, wob_ref)


def _outproj_xattn_kernel(ya_ref, yb_ref, x_ref, wout_hbm, g_ref, wq_ref, k_ref, v_ref, wo_ref, o_ref,
                          woutb_ref, wqb_ref, wob_ref, stage_ref, sem, *, layer):
    @pl.when((pl.program_id(0) == 0) & (pl.program_id(1) == 0))
    def _():
        ch = stage_ref.shape[0]
        for r0 in range(0, woutb_ref.shape[0], ch):
            copy = pltpu.make_async_copy(wout_hbm.at[layer, pl.ds(r0, ch), :], stage_ref, sem)
            copy.start()
            copy.wait()
            woutb_ref[r0:r0 + ch, :] = stage_ref[...].astype(BF16)
        wqb_ref[...] = wq_ref[0].astype(BF16)
        wob_ref[...] = wo_ref[0].astype(BF16)

    ka = ya_ref.shape[2]
    h = x_ref[0] + _dot(ya_ref[0], woutb_ref[0:ka, :]) + _dot(yb_ref[0], woutb_ref[ka:, :])
    o_ref[0] = _xattn_rows(h, g_ref, wqb_ref, k_ref, v_ref, wob_ref)


def outproj_xattn_block(y_a, y_b, x, w_out, gain, wq, k, v, wo, layer_out, layer, tm):
    b, s, d = x.shape
    ka = y_a.shape[2]
    ml, xa = k.shape[1], k.shape[2]
    single = dict(pipeline_mode=pl.Buffered(1))
    return pl.pallas_call(
        functools.partial(_outproj_xattn_kernel, layer=layer_out),
        grid=(b, s // tm),
        in_specs=[
            pl.BlockSpec((1, tm, ka), lambda i, t: (i, t, 0)),
            pl.BlockSpec((1, tm, ka), lambda i, t: (i, t, 0)),
            pl.BlockSpec((1, tm, d), lambda i, t: (i, t, 0)),
            pl.BlockSpec(memory_space=pl.ANY),
            pl.BlockSpec((1, d), lambda i, t: (0, 0)),
            pl.BlockSpec((1, d, xa), lambda i, t: (layer, 0, 0), **single),
            pl.BlockSpec((1, ml, xa), lambda i, t: (i, 0, 0)),
            pl.BlockSpec((1, ml, xa), lambda i, t: (i, 0, 0)),
            pl.BlockSpec((1, xa, d), lambda i, t: (layer, 0, 0), **single),
        ],
        out_specs=pl.BlockSpec((1, tm, d), lambda i, t: (i, t, 0)),
        out_shape=jax.ShapeDtypeStruct((b, s, d), F32),
        scratch_shapes=[
            pltpu.VMEM((2 * ka, d), BF16),
            pltpu.VMEM((d, xa), BF16),
            pltpu.VMEM((xa, d), BF16),
            pltpu.VMEM((min(2 * ka, WEIGHT_STAGE_ROWS), d), F32),
            pltpu.SemaphoreType.DMA,
        ],
        compiler_params=_cparams("arbitrary", "arbitrary"),
        name="outproj_xattn_block",
    )(y_a, y_b, x, w_out, gain.reshape(1, d), wq, k, v, wo)


def xattn_block(x, gain, wq, k, v, wo, layer, tm):
    b, s, d = x.shape
    ml, xa = k.shape[1], k.shape[2]
    n_t = s // tm
    return pl.pallas_call(
        functools.partial(_xattn_kernel, tm=tm, n_t=n_t, total=b * n_t),
        grid=(b, n_t),
        in_specs=[
            pl.BlockSpec(memory_space=pl.ANY),
            pl.BlockSpec((1, d), lambda i, t: (0, 0)),
            pl.BlockSpec((1, d, xa), lambda i, t: (layer, 0, 0)),
            pl.BlockSpec((1, ml, xa), lambda i, t: (i, 0, 0)),
            pl.BlockSpec((1, ml, xa), lambda i, t: (i, 0, 0)),
            pl.BlockSpec((1, xa, d), lambda i, t: (layer, 0, 0)),
        ],
        out_specs=pl.BlockSpec((1, tm, d), lambda i, t: (i, t, 0)),
        out_shape=jax.ShapeDtypeStruct((b, s, d), F32),
        scratch_shapes=[pltpu.VMEM((d, xa), BF16), pltpu.VMEM((xa, d), BF16),
                        pltpu.VMEM((XA_RING, tm, d), F32), pltpu.SemaphoreType.DMA((XA_RING,))],
        compiler_params=_cparams("arbitrary", "arbitrary"),
        name="xattn_block",
    )(x, gain.reshape(1, d), wq, k, v, wo)


_HGRN_LEVELS = (32, 16, 8, 4, 2, 1)


def _hgrn_masks():
    c = CHUNK
    i = np.arange(c)[:, None]
    j = np.arange(c)[None, :]
    masks = [(i // (2 * s) == j // (2 * s)) & (i % (2 * s) >= s) & (j % (2 * s) < s) for s in _HGRN_LEVELS]
    masks.append(i == j)
    return jnp.asarray(np.stack(masks).astype(np.float32))


def _segment_row(x, seg, idx):
    rows, lanes = x.shape
    x3 = x.reshape(rows // seg, seg, lanes)
    return jnp.broadcast_to(x3[:, idx:idx + 1, :], x3.shape).reshape(rows, lanes)


def _hgrn_kernel(q_ref, f_ref, i_ref, gate_ref, lbl_ref, nw_ref, masks_ref, y_ref, st_ref, *, layer, n_chunks):
    c = CHUNK
    tt = n_chunks * c

    @pl.when(pl.program_id(2) == 0)
    def _():
        st_ref[...] = jnp.zeros_like(st_ref)

    logits = lbl_ref[...]
    ex = jnp.exp(logits - jnp.max(logits, axis=0, keepdims=True))
    sm = ex / jnp.sum(ex, axis=0, keepdims=True)
    lb = jnp.sum(sm[0:layer + 1, :], axis=0, keepdims=True)

    sig_f = jax.nn.sigmoid(f_ref[0])
    q = _silu(q_ref[0])
    log_f = jnp.log(lb + (1.0 - lb) * sig_f)
    k = (1.0 - lb) * (1.0 - sig_f)
    vb = i_ref[0].astype(BF16)

    row = lax.broadcasted_iota(jnp.int32, (tt, HEAD_DIM), 0)
    rc = row & (c - 1)
    g = log_f
    for s in (1, 2, 4, 8, 16, 32):
        g = g + jnp.where(rc >= s, pltpu.roll(g, s, axis=0), 0.0)
    g_last = _segment_row(g, c, c - 1)

    exps = []
    for s in _HGRN_LEVELS:
        if 2 * s >= SUBLANES:
            exps.append(-jnp.abs(g - _segment_row(g, 2 * s, s - 1)))
        elif s == 2:
            r4 = row & 3
            below = pltpu.roll(log_f, 1, axis=0)
            above = pltpu.roll(log_f, tt - 1, axis=0)
            exps.append(jnp.where(r4 == 0, above, jnp.where(r4 == 1, 0.0,
                                                            jnp.where(r4 == 2, log_f, log_f + below))))
        else:
            exps.append(jnp.where((row & 1) == 1, log_f, 0.0))

    def chunks(x):
        xb = x.astype(BF16)
        return [xb[ci * c:(ci + 1) * c] for ci in range(n_chunks)]

    qs, ks = chunks(q), chunks(k)
    n_lv = len(_HGRN_LEVELS)
    on_diag = masks_ref[n_lv] > 0.5
    attn = [jnp.where(on_diag, _dot_nt(qs[ci], ks[ci]), 0.0) for ci in range(n_chunks)]
    for lv in range(n_lv):
        sc = jnp.exp(exps[lv])
        ql, kl = chunks(q * sc), chunks(k * sc)
        m = masks_ref[lv] > 0.5
        attn = [jnp.where(m, _dot_nt(ql[ci], kl[ci]), attn[ci]) for ci in range(n_chunks)]

    vs = chunks(vb)
    qg = chunks(q * jnp.exp(g))
    kd = chunks(k * jnp.exp(g_last - g))
    upd = [_dot_tn(vs[ci], kd[ci]) for ci in range(n_chunks)]
    intra = [_dot(attn[ci].astype(BF16), vs[ci]) for ci in range(n_chunks)]
    dec = jnp.exp(g_last)
    st = st_ref[...]
    outs = []
    for ci in range(n_chunks):
        outs.append(intra[ci] + _dot_nt(qg[ci], st.astype(BF16)))
        st = st * dec[ci * c:ci * c + 1, :] + upd[ci]
    st_ref[...] = st
    o = jnp.concatenate(outs, axis=0)
    y = o * lax.rsqrt(jnp.mean(o * o, axis=-1, keepdims=True) + NORM_EPS)
    y_ref[0] = (y * nw_ref[...] * _silu(gate_ref[0])).astype(y_ref.dtype)


def hgrn_heads(proj, lb_logits, norm_w, layer, col0, tt):
    b, s, _ = proj.shape
    nl = lb_logits.shape[0]
    masks = _hgrn_masks()
    blk = lambda o: pl.BlockSpec((1, tt, HEAD_DIM), lambda i, h, t: (i, t, col0 + o + h))
    return pl.pallas_call(
        functools.partial(_hgrn_kernel, layer=layer, n_chunks=tt // CHUNK),
        grid=(b, N_HEADS, s // tt),
        in_specs=[
            blk(0), blk(N_HEADS), blk(2 * N_HEADS), blk(3 * N_HEADS),
            pl.BlockSpec((nl, HEAD_DIM), lambda i, h, t: (0, h)),
            pl.BlockSpec((1, HEAD_DIM), lambda i, h, t: (0, h)),
            pl.BlockSpec(masks.shape, lambda i, h, t: (0, 0, 0)),
        ],
        out_specs=pl.BlockSpec((1, tt, HEAD_DIM), lambda i, h, t: (i, t, h)),
        out_shape=jax.ShapeDtypeStruct((b, s, N_HEADS * HEAD_DIM), BF16),
        scratch_shapes=[pltpu.VMEM((HEAD_DIM, HEAD_DIM), F32)],
        compiler_params=_cparams("parallel", "parallel", "arbitrary"),
        name="hgrn_heads",
    )(proj, proj, proj, proj, lb_logits, norm_w.reshape(1, -1), masks)


def _gdn_kernel(q_ref, k_ref, v_ref, gate_ref, gc_ref, gr_ref, cw_ref, alane_ref, blane_ref, acol_ref, bcol_ref,
                nw_ref, y_ref, s_ref, xq_ref, xk_ref, xv_ref, *, tt):
    c = CHUNK
    hd = HEAD_DIM
    pair = 2 * c
    n_pairs = tt // pair
    n_chunks = tt // c
    width = N_HEADS * hd

    @pl.when(pl.program_id(1) == 0)
    def _():
        s_ref[...] = jnp.zeros_like(s_ref)
        for r in (xq_ref, xk_ref, xv_ref):
            r[...] = jnp.zeros_like(r)

    q_all = _silu(_causal_conv(xq_ref, q_ref[0], cw_ref[:, 0:width]))
    k_all = _silu(_causal_conv(xk_ref, k_ref[0], cw_ref[:, width:2 * width]))
    v_all = _silu(_causal_conv(xv_ref, v_ref[0], cw_ref[:, 2 * width:3 * width]))

    gates = gc_ref[0]
    beta_all = jax.nn.sigmoid(gates)
    gam_all = -jnp.exp(alane_ref[...]) * _softplus(gates + blane_ref[...])
    row = lax.broadcasted_iota(jnp.int32, (tt, LANES), 0)
    rc = row & (c - 1)
    for s in (1, 2, 4, 8, 16, 32):
        gam_all = gam_all + jnp.where(rc >= s, pltpu.roll(gam_all, s, axis=0), 0.0)
    glast_all = _segment_row(gam_all, c, c - 1)
    gam_rows = -jnp.exp(acol_ref[...]) * _softplus(gr_ref[0] + bcol_ref[...])
    lc = lax.broadcasted_iota(jnp.int32, gam_rows.shape, 1) & (c - 1)
    for s in (1, 2, 4, 8, 16, 32):
        gam_rows = gam_rows + jnp.where(lc >= s, pltpu.roll(gam_rows, s, axis=1), 0.0)

    ii = lax.broadcasted_iota(jnp.int32, (c, LANES), 0)
    ll = lax.broadcasted_iota(jnp.int32, (c, LANES), 1)
    jj = ll & (c - 1)
    left = ll < c
    lower = ii >= jj
    strict = ii > jj
    eye2 = jnp.where(ii == jj, 1.0, 0.0)

    def block_diag(m):
        zero = jnp.zeros_like(m)
        return jnp.concatenate([jnp.where(left, m, zero), jnp.where(left, zero, m)], axis=0)

    def split2(x):
        hi = x.astype(BF16)
        return hi, (x - hi.astype(F32)).astype(BF16)

    def times_block_diag(lhs, p_hi, p_lo):
        l_hi, l_lo = split2(lhs)
        return _dot(jnp.concatenate([l_hi, l_lo, l_hi], axis=1), jnp.concatenate([p_hi, p_hi, p_lo], axis=0))

    units = [(h, p) for h in range(N_HEADS) for p in range(n_pairs)]
    qd, kd, rhs, khb, kbb, qhb, gam_h = [], [], [], [], [], [], []
    for h in range(N_HEADS):
        lanes = slice(h * hd, (h + 1) * hd)
        qh, kh, vh = q_all[:, lanes], k_all[:, lanes], v_all[:, lanes]
        qh = qh * lax.rsqrt(jnp.sum(qh * qh, axis=-1, keepdims=True) + L2_EPS) * (hd ** -0.5)
        kh = kh * lax.rsqrt(jnp.sum(kh * kh, axis=-1, keepdims=True) + L2_EPS)
        beta = beta_all[:, h:h + 1]
        gam = gam_all[:, N_HEADS + h:N_HEADS + h + 1]
        eg = jnp.exp(gam)
        kb = kh * beta
        gam_h.append(gam)
        rhs.append(jnp.concatenate([kb * eg, vh * beta], axis=1).astype(BF16))
        qd.append((qh * eg).astype(BF16))
        kd.append((kh * jnp.exp(glast_all[:, N_HEADS + h:N_HEADS + h + 1] - gam)).astype(BF16))
        khb.append(kh.astype(BF16))
        kbb.append(kb.astype(BF16))
        qhb.append(qh.astype(BF16))

    x_mats, qk_tiles = [], [[] for _ in range(N_HEADS)]
    for h, p in units:
        r0 = p * pair
        rows = slice(r0, r0 + pair)
        res = _dot_nt(jnp.concatenate([kbb[h][rows], qhb[h][rows]], axis=0), khb[h][rows])
        col = jnp.where(left, jnp.broadcast_to(gam_h[h][r0:r0 + c], (c, LANES)),
                        jnp.broadcast_to(gam_h[h][r0 + c:r0 + pair], (c, LANES)))
        decay = jnp.exp(jnp.minimum(col - gam_rows[N_HEADS + h:N_HEADS + h + 1, rows], 0.0))
        x_mats.append(jnp.where(strict, jnp.where(left, res[0:c], res[c:pair]) * -decay, 0.0))
        qk_tiles[h].append(jnp.where(lower & left, res[pair:pair + c] * decay, 0.0).astype(BF16))
        qk_tiles[h].append(jnp.where(lower & (~left), res[pair + c:2 * pair] * decay, 0.0).astype(BF16))

    t_mats = [eye2 + x for x in x_mats]
    powers = []
    for x in x_mats:
        p_hi, p_lo = split2(x)
        powers.append(times_block_diag(x, block_diag(p_hi), block_diag(p_lo)))
    for it in range(5):
        for u in range(len(units)):
            p_hi, p_lo = split2(powers[u])
            bd_hi, bd_lo = block_diag(p_hi), block_diag(p_lo)
            if it < 4:
                both = times_block_diag(jnp.concatenate([powers[u], t_mats[u]], axis=0), bd_hi, bd_lo)
                powers[u], t_mats[u] = both[0:c], t_mats[u] + both[c:pair]
            else:
                t_mats[u] = t_mats[u] + times_block_diag(t_mats[u], bd_hi, bd_lo)
    wu = [[] for _ in range(N_HEADS)]
    for u, (h, p) in enumerate(units):
        wu[h].append(_dot(block_diag(t_mats[u].astype(BF16)), rhs[h][p * pair:(p + 1) * pair]))

    outs = [[] for _ in range(N_HEADS)]
    zeros_half = jnp.zeros((c, hd), BF16)
    heads = range(N_HEADS)
    states = [s_ref[h] for h in heads]
    chunk_decay = jnp.exp(glast_all)
    for ci in range(n_chunks):
        rows = slice(ci * c, (ci + 1) * c)
        half = slice((ci % 2) * c, (ci % 2 + 1) * c)
        stb = [states[h].astype(BF16) for h in heads]
        wu_c = [wu[h][ci // 2][half] for h in heads]
        w_s = [_dot(wu_c[h][:, :hd].astype(BF16), stb[h]) for h in heads]
        vnb = [(wu_c[h][:, hd:] - w_s[h]).astype(BF16) for h in heads]
        for h in heads:
            v2 = jnp.concatenate([vnb[h], zeros_half] if ci % 2 == 0 else [zeros_half, vnb[h]], axis=0)
            outs[h].append(_dot(qd[h][rows], stb[h]) + _dot(qk_tiles[h][ci], v2))
        states = [states[h] * chunk_decay[ci * c:ci * c + 1, N_HEADS + h:N_HEADS + h + 1]
                  + _dot_tn(kd[h][rows], vnb[h]) for h in heads]
    for h in heads:
        s_ref[h] = states[h]

    for h in range(N_HEADS):
        lanes = slice(h * hd, (h + 1) * hd)
        o = jnp.concatenate(outs[h], axis=0)
        y = o * lax.rsqrt(jnp.mean(o * o, axis=-1, keepdims=True) + NORM_EPS)
        y_ref[0, :, lanes] = (y * nw_ref[:, lanes] * _silu(gate_ref[0, :, lanes])).astype(y_ref.dtype)


def gdn_heads(proj, gates_col, gates_row, conv_w, a_log, dt_bias, norm_w, col0, tt):
    b, s, _ = proj.shape
    width = N_HEADS * HEAD_DIM
    blk = lambda o: pl.BlockSpec((1, tt, width), lambda i, t: (i, t, col0 + o))
    whole = lambda shape: pl.BlockSpec(shape, lambda i, t: (0,) * len(shape))
    lane_vec = jnp.zeros((1, LANES), F32).at[0, N_HEADS:2 * N_HEADS]
    col_vec = jnp.zeros((2 * N_HEADS, 1), F32).at[N_HEADS:, 0]
    return pl.pallas_call(
        functools.partial(_gdn_kernel, tt=tt),
        grid=(b, s // tt),
        in_specs=[
            blk(0), blk(1), blk(2), blk(3),
            pl.BlockSpec((1, tt, LANES), lambda i, t: (i, t, 0)),
            pl.BlockSpec((1, 2 * N_HEADS, tt), lambda i, t: (i, 0, t)),
            whole(conv_w.shape),
            whole((1, LANES)), whole((1, LANES)), whole((2 * N_HEADS, 1)), whole((2 * N_HEADS, 1)),
            whole((1, width)),
        ],
        out_specs=pl.BlockSpec((1, tt, width), lambda i, t: (i, t, 0)),
        out_shape=jax.ShapeDtypeStruct((b, s, width), BF16),
        scratch_shapes=[
            pltpu.VMEM((N_HEADS, HEAD_DIM, HEAD_DIM), F32),
            pltpu.VMEM((SUBLANES, width), F32),
            pltpu.VMEM((SUBLANES, width), F32),
            pltpu.VMEM((SUBLANES, width), F32),
        ],
        compiler_params=_cparams("parallel", "arbitrary"),
        name="gdn_heads",
    )(proj, proj, proj, proj, gates_col, gates_row, conv_w,
      lane_vec.set(a_log), lane_vec.set(dt_bias), col_vec.set(a_log), col_vec.set(dt_bias),
      norm_w.reshape(1, -1))


SSD_PAIRS = SSD_HEADS_PER_GROUP // 2


def _ssd_expand_table():
    expand = np.zeros((SSD_GROUPS, LANES, SSD_GROUP_WIDTH), np.float32)
    for g in range(SSD_GROUPS):
        for hh in range(SSD_HEADS_PER_GROUP):
            expand[g, g * SSD_HEADS_PER_GROUP + hh, hh * SSD_HEADDIM:(hh + 1) * SSD_HEADDIM] = 1.0
    return jnp.asarray(np.concatenate([expand] * 3, axis=1), BF16)


def _ssd_kernel(xfirst_ref, xnext_ref, wz_ref, wx_ref, wb_ref, wc_ref, dtc_ref, dtp_ref, cwx_ref, cwb_ref, cwc_ref,
                cbx_ref, cbb_ref, cbc_ref, blane_ref, alane_ref, bpair_ref, apair_ref, de_ref, exp_ref,
                y_ref, ht_ref, xx_ref, xb_ref, xc_ref, wbf_ref, proj_ref, pnext_ref, *, tt):
    c = CHUNK
    n_chunks = tt // c
    hp = SSD_HEADDIM
    gw = SSD_GROUP_WIDTH

    @pl.when((pl.program_id(1) == 0) & (pl.program_id(2) == 0))
    def _():
        wbf_ref[:, 0:gw] = wz_ref[0].T.astype(BF16)
        wbf_ref[:, gw:2 * gw] = wx_ref[0].T.astype(BF16)
        wbf_ref[:, 2 * gw:2 * gw + LANES] = wb_ref[0].T.astype(BF16)
        wbf_ref[:, 2 * gw + LANES:2 * gw + 2 * LANES] = wc_ref[0].T.astype(BF16)
        proj_ref[...] = _dot(xfirst_ref[...], wbf_ref[...])

    @pl.when(pl.program_id(2) == 0)
    def _():
        for r in (ht_ref, xx_ref, xb_ref, xc_ref):
            r[...] = jnp.zeros_like(r)

    piece_w = PROJ_PIECE
    pieces = [slice(p0, p0 + piece_w) for p0 in range(0, wbf_ref.shape[1], piece_w)]

    def project_piece():
        if pieces:
            cols = pieces.pop(0)
            pnext_ref[:, cols] = _dot(xnext_ref[...], wbf_ref[:, cols])

    project_piece()
    xs_tiles = []
    for lo in range(0, gw, PROJ_PIECE):
        lanes = slice(lo, lo + PROJ_PIECE)
        xs_tiles.append(_silu(_causal_conv(xx_ref.at[:, lanes], proj_ref[:, gw + lo:gw + lo + PROJ_PIECE],
                                           cwx_ref[:, lanes]) + cbx_ref[:, lanes]))
        project_piece()
    xs = jnp.concatenate(xs_tiles, axis=1)
    bm = _silu(_causal_conv(xb_ref, proj_ref[:, 2 * gw:2 * gw + LANES], cwb_ref[...])
               + cbb_ref[...]).astype(BF16)
    cm = _silu(_causal_conv(xc_ref, proj_ref[:, 2 * gw + LANES:2 * gw + 2 * LANES], cwc_ref[...])
               + cbc_ref[...]).astype(BF16)

    dt_c = _softplus(dtc_ref[0] + blane_ref[...])
    acum_c = dt_c * -jnp.exp(alane_ref[...])
    rc = lax.broadcasted_iota(jnp.int32, (tt, LANES), 0) & (c - 1)
    for s in (1, 2, 4, 8, 16, 32):
        acum_c = acum_c + jnp.where(rc >= s, pltpu.roll(acum_c, s, axis=0), 0.0)
    s_c = dt_c * jnp.exp(_segment_row(acum_c, c, c - 1) - acum_c)
    a_hi, a_mid, a_lo = _split3(acum_c)
    acum_e = _dot(jnp.concatenate([a_hi, a_mid, a_lo], axis=1), exp_ref[0])
    s_hi, s_mid, _ = _split3(s_c)
    s_e = _dot(jnp.concatenate([s_hi, s_mid], axis=1), exp_ref[0, 0:2 * LANES, :])
    decay_e = jnp.exp(acum_e)
    xw = (xs * s_e).astype(BF16)
    xsb = xs.astype(BF16)

    project_piece()
    dt_p = _softplus(dtp_ref[0, 0] + jnp.concatenate([bpair_ref[0]] * n_chunks, axis=1))
    acum_p = dt_p * -jnp.exp(jnp.concatenate([apair_ref[0]] * n_chunks, axis=1))
    lc = lax.broadcasted_iota(jnp.int32, acum_p.shape, 1) & (c - 1)
    for s in (1, 2, 4, 8, 16, 32):
        acum_p = acum_p + jnp.where(lc >= s, pltpu.roll(acum_p, s, axis=1), 0.0)

    ii = lax.broadcasted_iota(jnp.int32, (c, LANES), 0)
    ll = lax.broadcasted_iota(jnp.int32, (c, LANES), 1)
    lower2 = ii >= (ll & (c - 1))
    left = ll < hp
    while pieces:
        project_piece()

    chunk_rows = [slice(ci * c, (ci + 1) * c) for ci in range(n_chunks)]
    cb2 = [_dot_nt(cm[r], jnp.concatenate([bm[r], bm[r]], axis=0)) for r in chunk_rows]
    upd = [_dot_tn(bm[r], xw[r]) for r in chunk_rows]
    y_diag = []
    for ci, r in enumerate(chunk_rows):
        tiles = []
        for pr in range(SSD_PAIRS):
            lanes = slice(2 * pr * hp, (2 * pr + 2) * hp)
            tok = slice(ci * LANES, (ci + 1) * LANES)
            lmat = jnp.where(lower2, jnp.exp(jnp.minimum(acum_e[r, lanes] - acum_p[pr:pr + 1, tok], 0.0)), 0.0)
            xp = xsb[r, lanes]
            zero = jnp.zeros_like(xp)
            rhs = jnp.concatenate([jnp.where(left, xp, zero), jnp.where(left, zero, xp)], axis=0)
            tiles.append(_dot((cb2[ci] * lmat * dt_p[pr:pr + 1, tok]).astype(BF16), rhs))
        y_diag.append(jnp.concatenate(tiles, axis=1))
    ht = ht_ref[...]
    y_off = []
    for ci, r in enumerate(chunk_rows):
        y_off.append(_dot(cm[r], ht.astype(BF16)))
        ht = ht * decay_e[ci * c + c - 1:ci * c + c, :] + upd[ci]
    ht_ref[...] = ht
    y = jnp.concatenate(y_diag, axis=0) + jnp.concatenate(y_off, axis=0) * decay_e + de_ref[...] * xs
    y_ref[0] = y * _silu(proj_ref[:, 0:gw])
    proj_ref[...] = pnext_ref[...]


def ssd_groups(xn, w_t, layer, dt_col, dt_pairs, conv_w, conv_b, dt_bias, a_log, d_skip, tt):
    b, s, _ = dt_col.shape
    k = xn.shape[1]
    n_t = s // tt
    g_w = SSD_GROUP_WIDTH
    n_xblk = SSD_GROUPS
    expand = _ssd_expand_table()
    rep = lambda p: jnp.repeat(p.astype(F32), SSD_HEADDIM).reshape(1, -1)
    row2 = lambda v: v.reshape(1, -1)
    lane_vec = lambda p: jnp.pad(p.astype(F32), (0, LANES - p.shape[0])).reshape(1, LANES)

    def pair_tile(p):
        t = jnp.repeat(p.astype(F32).reshape(SSD_GROUPS, SSD_PAIRS, 2, 1), CHUNK, axis=-1)
        t = t.reshape(SSD_GROUPS, SSD_PAIRS, LANES)
        return jnp.pad(t, ((0, 0), (0, SUBLANES - SSD_PAIRS), (0, 0)))
    x_off = n_xblk
    b_off = 2 * n_xblk * (g_w // LANES)
    c_off = b_off + SSD_GROUPS
    p_w = 2 * g_w + 2 * LANES
    w_wide = lambda o: pl.BlockSpec((1, g_w, k), lambda g, i, t: (layer, o + g, 0))
    w_narrow = lambda o: pl.BlockSpec((1, LANES, k), lambda g, i, t: (layer, o + g, 0))
    return pl.pallas_call(
        functools.partial(_ssd_kernel, tt=tt),
        grid=(SSD_GROUPS, b, n_t),
        in_specs=[
            pl.BlockSpec((tt, k), lambda g, i, t: (0, 0)),
            pl.BlockSpec((tt, k), lambda g, i, t: ((i * n_t + t + 1) % (b * n_t), 0)),
            w_wide(0), w_wide(x_off), w_narrow(b_off), w_narrow(c_off),
            pl.BlockSpec((1, tt, LANES), lambda g, i, t: (i, t, 0)),
            pl.BlockSpec((1, 1, SUBLANES, 2 * tt), lambda g, i, t: (i, g, 0, t)),
            pl.BlockSpec((CONV_K, g_w), lambda g, i, t: (0, g)),
            pl.BlockSpec((CONV_K, LANES), lambda g, i, t: (0, n_xblk * (g_w // LANES) + g)),
            pl.BlockSpec((CONV_K, LANES), lambda g, i, t: (0, n_xblk * (g_w // LANES) + SSD_GROUPS + g)),
            pl.BlockSpec((1, g_w), lambda g, i, t: (0, g)),
            pl.BlockSpec((1, LANES), lambda g, i, t: (0, n_xblk * (g_w // LANES) + g)),
            pl.BlockSpec((1, LANES), lambda g, i, t: (0, n_xblk * (g_w // LANES) + SSD_GROUPS + g)),
            pl.BlockSpec((1, LANES), lambda g, i, t: (0, 0)),
            pl.BlockSpec((1, LANES), lambda g, i, t: (0, 0)),
            pl.BlockSpec((1, SUBLANES, LANES), lambda g, i, t: (g, 0, 0)),
            pl.BlockSpec((1, SUBLANES, LANES), lambda g, i, t: (g, 0, 0)),
            pl.BlockSpec((1, g_w), lambda g, i, t: (0, g)),
            pl.BlockSpec((1, 3 * LANES, g_w), lambda g, i, t: (g, 0, 0)),
        ],
        out_specs=pl.BlockSpec((1, tt, g_w), lambda g, i, t: (i, t, g)),
        out_shape=jax.ShapeDtypeStruct((b, s, SSD_GROUPS * g_w), F32),
        scratch_shapes=[
            pltpu.VMEM((SSD_DSTATE, g_w), F32),
            pltpu.VMEM((SUBLANES, g_w), F32),
            pltpu.VMEM((SUBLANES, LANES), F32),
            pltpu.VMEM((SUBLANES, LANES), F32),
            pltpu.VMEM((k, p_w), BF16),
            pltpu.VMEM((tt, p_w), F32),
            pltpu.VMEM((tt, p_w), F32),
        ],
        compiler_params=_cparams("arbitrary", "arbitrary", "arbitrary"),
        name="ssd_groups",
    )(xn, xn, w_t, w_t, w_t, w_t, dt_col, dt_pairs, conv_w, conv_w, conv_w,
      row2(conv_b), row2(conv_b), row2(conv_b), lane_vec(dt_bias), lane_vec(a_log),
      pair_tile(dt_bias), pair_tile(a_log), rep(d_skip), expand)


def _tail_weight(w_t, layer, n_main):
    tail = w_t[layer, n_main:, :]
    return jnp.pad(tail, ((0, LANES - tail.shape[0]), (0, 0)))[None]


def _tile(n, pref):
    t = min(n, pref)
    while n % t:
        t //= 2
    return t


def kernel(x, mem, norm_mix, norm_xattn, norm_mem, norm_ffn, norm_final, hy_w_in, hgrn_lb_logits, hgrn_norm,
           gdn_conv_w, gdn_a_log, gdn_dt_bias, gdn_norm, hy_w_out, ssd_w_in, ssd_conv_w, ssd_conv_b,
           ssd_dt_bias, ssd_a_log, ssd_d, ssd_norm, ssd_w_out, xa_wq, xa_wk, xa_wv, xa_wo, ffn_w_gate,
           ffn_w_up, ffn_w_down):
    bsz, seq, d = x.shape
    tok = bsz * seq
    depth = norm_mix.shape[0]
    tm = _tile(tok, TOKEN_TILE)
    tt = _tile(seq, SEQ_TILE)
    hy_w_in_t = jnp.swapaxes(hy_w_in, 1, 2)
    ssd_w_in_t = jnp.swapaxes(ssd_w_in, 1, 2)

    h = x.reshape(tok, d)
    for layer in range(depth):
        k_mem, v_mem = mem_kv(mem, norm_mem[layer], xa_wk, xa_wv, layer)
        if layer % 2 == 0:
            e = layer // 2
            n_main = 8 * N_HEADS * HEAD_DIM
            xn, tail = norm_cast(h, norm_mix[layer], _tail_weight(hy_w_in_t, e, n_main), tm)
            proj = ws_matmul([xn], hy_w_in_t, e, n_main, tm, _tile(n_main, COLUMN_TILE),
                             w_transposed=True).reshape(bsz, seq, n_main)
            gates_col = tail.reshape(bsz, seq, LANES)
            gates_row = jnp.swapaxes(gates_col[:, :, :2 * N_HEADS], 1, 2)
            y_a = hgrn_heads(proj, hgrn_lb_logits.astype(F32), hgrn_norm[e], e, 0, tt)
            y_b = gdn_heads(proj, gates_col, gates_row, gdn_conv_w[e].astype(F32), gdn_a_log[e].astype(F32),
                            gdn_dt_bias[e].astype(F32), gdn_norm[e], 4, _tile(seq, GDN_SEQ_TILE))
            h = outproj_xattn_block(y_a, y_b, h.reshape(bsz, seq, d), hy_w_out, norm_xattn[layer], xa_wq, k_mem,
                                    v_mem, xa_wo, e, layer, _tile(seq, SEQ_TILE)).reshape(tok, d)
        else:
            o = layer // 2
            n_heads = ssd_dt_bias.shape[1]
            n_main = ssd_w_in.shape[2] - n_heads
            xn, tail = norm_cast(h, norm_mix[layer], _tail_weight(ssd_w_in_t, o, n_main), tm)
            dt_col = tail.reshape(bsz, seq, LANES)
            dt_pairs = dt_col[:, :, :n_heads].reshape(bsz, seq // CHUNK, CHUNK, SSD_GROUPS, SSD_PAIRS, 2)
            dt_pairs = dt_pairs.transpose(0, 3, 4, 1, 5, 2).reshape(bsz, SSD_GROUPS, SSD_PAIRS, 2 * seq)
            dt_pairs = jnp.pad(dt_pairs, ((0, 0), (0, 0), (0, SUBLANES - SSD_PAIRS), (0, 0)))
            y = ssd_groups(xn, ssd_w_in_t, o, dt_col, dt_pairs, ssd_conv_w[o].astype(F32), ssd_conv_b[o].astype(F32),
                           ssd_dt_bias[o], ssd_a_log[o], ssd_d[o], tt)
            h = ws_matmul([y.reshape(tok, -1)], ssd_w_out, o, d, _tile(tok, OUT_PROJ_TOKEN_TILE),
                          _tile(d, COLUMN_TILE), gain=ssd_norm[o], resid=h, single_buffer_w=True)
            h = xattn_block(h.reshape(bsz, seq, d), norm_xattn[layer], xa_wq, k_mem, v_mem, xa_wo, layer,
                            _tile(seq, SEQ_TILE)).reshape(tok, d)
        h = ffn_block(h, norm_ffn[layer], ffn_w_gate, ffn_w_up, ffn_w_down, layer, norm_final,
                      _tile(tok, FFN_TOKEN_TILE), _tile(ffn_w_gate.shape[2], FFN_TILE), layer == depth - 1)
    return h.reshape(bsz, seq, d)
```
